```python
import math
import jax, jax.numpy as jnp
from jax import lax
import numpy as np

D_MODEL = 1024
BATCH = 1
SEQ = 16384
DEPTH = 2

GRID_W = 64
CTX_LEN = 256
ALPHA = (2 * DEPTH) ** 0.25
BETA = (8 * DEPTH) ** -0.25
LN_EPS = 1e-6
D_FF = ((math.ceil(8 * D_MODEL / 3) + 255) // 256) * 256

RWKV_HEAD = 64
D_A = 3 * D_MODEL // 4
H_A = D_A // RWKV_HEAD
DECAY_LORA = 64
ICL_LORA = 64
GATE_LORA = 128
GN_EPS = 64e-5
D_B = D_MODEL - D_A
POOL_WINDOWS = (2, 4, 8, 16)
POOL_GROUP = D_B // len(POOL_WINDOWS)
C_RW = 3 * D_A + 2 * DECAY_LORA + 2 * ICL_LORA + GATE_LORA
D_IN_EVEN = C_RW + D_B
NA_HEAD = 64
H_C = D_MODEL // NA_HEAD
NA_KH = 8
NA_KW = 16

N_EVEN = (DEPTH + 1) // 2
N_ODD = DEPTH // 2

kernel_name = 'rwkv7_pool_natten_hybrid_dit'


def layer_norm(x, g, b):
    xf = x.astype(jnp.float32)
    mu = xf.mean(-1, keepdims=True)
    var = jnp.mean(jnp.square(xf - mu), -1, keepdims=True)
    return ((xf - mu) * lax.rsqrt(var + LN_EPS)).astype(x.dtype) * g + b


def swiglu(h, w1, w3, w2):
    return (jax.nn.silu(h @ w1) * (h @ w3)) @ w2


def centred_shift_mix(p, mu):
    zero = jnp.zeros_like(p[:, :1])
    prev = jnp.concatenate([zero, p[:, :-1]], axis=1)
    nxt = jnp.concatenate([p[:, 1:], zero], axis=1)
    return p + (prev - p) * mu[0] + (nxt - p) * mu[1]


def rwkv_prepare(p, mu, w0, w2, a0, a2, g2, k_k, k_a):
    B, T, _ = p.shape
    p = centred_shift_mix(p, mu)
    s1 = D_A; s2 = 2 * D_A; s3 = 3 * D_A; s4 = s3 + 2 * DECAY_LORA; s5 = s4 + 2 * ICL_LORA
    r, k, v, wd, ad, gd = jnp.split(p, [s1, s2, s3, s4, s5], axis=-1)
    wd = wd.reshape(B, T, 2, DECAY_LORA)
    ad = ad.reshape(B, T, 2, ICL_LORA)
    f32 = jnp.float32
    logw = -jax.nn.softplus(-(w0 + jnp.einsum('btdr,drc->btdc', jnp.tanh(wd), w2)).astype(f32)) - 0.5
    decay = jnp.exp(-jnp.exp(logw))
    a = jax.nn.sigmoid((a0 + jnp.einsum('btdr,drc->btdc', ad, a2)).astype(f32))
    g = jax.nn.sigmoid(gd) @ g2
    r = r.astype(f32); k = k.astype(f32); v = v.astype(f32)
    kk = (k * k_k).reshape(B, T, H_A, RWKV_HEAD)
    kk = kk / jnp.maximum(jnp.linalg.norm(kk, axis=-1, keepdims=True), 1e-12)
    kd = k[:, :, None, :] * (1.0 + (a - 1.0) * k_a)
    bd = kk.reshape(B, T, 1, D_A) * a
    hs = (B, T, 2, H_A, RWKV_HEAD)
    return (r.reshape(B, T, H_A, RWKV_HEAD), v.reshape(B, T, H_A, RWKV_HEAD), kk,
            decay.reshape(hs), kd.reshape(hs), bd.reshape(hs), g)


def wkv_scan(S0, r, w, k, v, kk, b, reverse):
    def step(S, inp):
        r_t, w_t, k_t, v_t, kk_t, b_t = inp
        sa = -jnp.einsum('bhij,bhj->bhi', S, kk_t)
        S = S * w_t[:, :, None, :] + sa[..., :, None] * b_t[:, :, None, :] + v_t[..., :, None] * k_t[:, :, None, :]
        return S, jnp.einsum('bhij,bhj->bhi', S, r_t)
    xs = tuple(jnp.swapaxes(t, 0, 1) for t in (r, w, k, v, kk, b))
    S, ys = lax.scan(step, S0, xs, reverse=reverse)
    return S, jnp.swapaxes(ys, 0, 1)


def rwkv_readout(y, r, v, kd, g, r_k, lnx_g, lnx_b):
    B, T = y.shape[:2]
    mu = y.mean(-1, keepdims=True)
    var = jnp.mean(jnp.square(y - mu), -1, keepdims=True)
    yn = ((y - mu) * lax.rsqrt(var + GN_EPS)).reshape(B, T, D_A) * lnx_g + lnx_b
    coef = jnp.sum(r[:, :, None] * kd * r_k, axis=(2, 4))
    bonus = (coef[..., None] * v).reshape(B, T, D_A)
    return ((yn + bonus) * g).astype(g.dtype)


def multiscale_pool(p, pool_w, pool_scale):
    B, T, _ = p.shape
    pf = p.astype(jnp.float32)
    cs = jnp.concatenate([jnp.zeros((B, 1, D_B), jnp.float32), jnp.cumsum(pf, axis=1)], axis=1)
    t = jnp.arange(T)
    groups = []
    for gi, win in enumerate(POOL_WINDOWS):
        lo = jnp.clip(t - win // 2, 0, T)
        hi = jnp.clip(t + win // 2, 0, T)
        sl = slice(gi * POOL_GROUP, (gi + 1) * POOL_GROUP)
        cg = cs[:, :, sl]
        mean = (cg[:, hi] - cg[:, lo]) / (hi - lo).astype(jnp.float32)[None, :, None]
        groups.append(mean - pf[:, :, sl])
    pooled = jnp.stack(groups, axis=2).astype(p.dtype)
    y = jnp.einsum('btgc,gcd->btgd', pooled, pool_w).reshape(B, T, D_B)
    return y * pool_scale


def rwkv_pool_mixer(h, hc, want_ctx, w_in, shift_mu, w0, w2, a0, a2, g2, k_k, k_a, r_k,
                    lnx_g, lnx_b, pool_w, pool_scale, w_out):
    B = h.shape[0]
    p_lat = h @ w_in
    p_ctx = hc @ w_in
    lat = rwkv_prepare(p_lat[..., :C_RW], shift_mu, w0, w2, a0, a2, g2, k_k, k_a)
    cx = rwkv_prepare(p_ctx[..., :C_RW], shift_mu, w0, w2, a0, a2, g2, k_k, k_a)
    S0 = jnp.zeros((B, H_A, RWKV_HEAD, RWKV_HEAD), jnp.float32)
    y_lat = 0.0
    y_ctx = 0.0
    for d in range(2):
        rev = d == 1
        S_c, yc = wkv_scan(S0, cx[0], cx[3][:, :, d], cx[4][:, :, d], cx[1], cx[2], cx[5][:, :, d], rev)
        _, yl = wkv_scan(S_c, lat[0], lat[3][:, :, d], lat[4][:, :, d], lat[1], lat[2], lat[5][:, :, d], rev)
        y_lat = y_lat + yl
        if want_ctx:
            y_ctx = y_ctx + yc
    a_lat = rwkv_readout(y_lat, lat[0], lat[1], lat[4], lat[6], r_k, lnx_g, lnx_b)
    b_lat = multiscale_pool(p_lat[..., C_RW:], pool_w, pool_scale)
    out_lat = jnp.concatenate([a_lat, b_lat], axis=-1) @ w_out
    if not want_ctx:
        return out_lat, None
    a_ctx = rwkv_readout(y_ctx, cx[0], cx[1], cx[4], cx[6], r_k, lnx_g, lnx_b)
    b_ctx = multiscale_pool(p_ctx[..., C_RW:], pool_w, pool_scale)
    out_ctx = jnp.concatenate([a_ctx, b_ctx], axis=-1) @ w_out
    return out_lat, out_ctx


def neighbourhood_attention(q, k, v, kc, vc, rpb):
    B, rows = q.shape[:2]
    kh = min(NA_KH, rows)
    kw = NA_KW
    n_loc = kh * kw
    scale = NA_HEAD ** -0.5
    cols = jnp.arange(GRID_W)
    col_start = jnp.clip(cols - kw // 2, 0, GRID_W - kw)
    col_idx = col_start[:, None] + jnp.arange(kw)[None, :]
    col_off = col_idx - cols[:, None] + (NA_KW - 1)

    def row_block(r):
        sr = jnp.clip(r - kh // 2, 0, rows - kh)
        qr = lax.dynamic_index_in_dim(q, r, axis=1, keepdims=False)
        kr = lax.dynamic_slice_in_dim(k, sr, kh, axis=1)
        vr = lax.dynamic_slice_in_dim(v, sr, kh, axis=1)
        kg = jnp.take(kr, col_idx, axis=2).transpose(0, 2, 1, 3, 4, 5).reshape(B, GRID_W, n_loc, H_C, NA_HEAD)
        vg = jnp.take(vr, col_idx, axis=2).transpose(0, 2, 1, 3, 4, 5).reshape(B, GRID_W, n_loc, H_C, NA_HEAD)
        row_off = sr - r + jnp.arange(kh) + (NA_KH - 1)
        bias = rpb[:, row_off[:, None, None], col_off[None, :, :]]
        bias = bias.transpose(0, 2, 1, 3).reshape(H_C, GRID_W, n_loc).astype(jnp.float32)
        s_loc = jnp.einsum('bqhd,bqnhd->bhqn', qr, kg).astype(jnp.float32) * scale + bias
        s_ctx = jnp.einsum('bqhd,bnhd->bhqn', qr, kc).astype(jnp.float32) * scale
        p = jax.nn.softmax(jnp.concatenate([s_loc, s_ctx], axis=-1), axis=-1).astype(v.dtype)
        return (jnp.einsum('bhqn,bqnhd->bqhd', p[..., :n_loc], vg)
                + jnp.einsum('bhqn,bnhd->bqhd', p[..., n_loc:], vc))

    o = lax.map(row_block, jnp.arange(rows))
    return jnp.moveaxis(o, 0, 1)


def na_mixer(h, hc, want_ctx, w_in, rpb, w_out):
    B, L, D = h.shape
    rows = L // GRID_W
    n_ctx = hc.shape[1]
    q, k, v = jnp.split(h @ w_in, 3, axis=-1)
    kc, vc = jnp.split(hc @ w_in[:, D:], 2, axis=-1)
    kc = kc.reshape(B, n_ctx, H_C, NA_HEAD)
    vc = vc.reshape(B, n_ctx, H_C, NA_HEAD)
    gs = (B, rows, GRID_W, H_C, NA_HEAD)
    o = neighbourhood_attention(q.reshape(gs), k.reshape(gs), v.reshape(gs), kc, vc, rpb)
    out_lat = o.reshape(B, L, D) @ w_out
    if not want_ctx:
        return out_lat, None
    qc = (hc @ w_in[:, :D]).reshape(B, n_ctx, H_C, NA_HEAD)
    s = jnp.einsum('bqhd,bkhd->bhqk', qc, kc).astype(jnp.float32) * NA_HEAD ** -0.5
    p = jax.nn.softmax(s, axis=-1).astype(vc.dtype)
    oc = jnp.einsum('bhqk,bkhd->bqhd', p, vc).reshape(B, n_ctx, D)
    return out_lat, oc @ w_out


def setup_inputs(seed: int = 0) -> dict:
    key = jax.random.key(seed)
    ks = jax.random.split(key, 32)
    D = D_MODEL
    f32 = jnp.float32

    def n(k, shape, s):
        return jax.random.normal(k, shape, f32) * s

    return {
        'x': n(ks[0], (BATCH, SEQ, D), 1.0),
        'c': n(ks[1], (BATCH, D), 1.0),
        'ctx': n(ks[2], (BATCH, CTX_LEN, D), 1.0),
        'c_ctx': n(ks[3], (D,), 1.0),
        'ada_w': n(ks[4], (DEPTH, D, 6 * D), 0.5 * D ** -0.5),
        'ada_b': n(ks[5], (DEPTH, 6 * D), 0.02),
        'ln_g': 1.0 + n(ks[6], (DEPTH, 2, D), 0.05),
        'ln_b': n(ks[7], (DEPTH, 2, D), 0.02),
        'ffn_w1': n(ks[8], (DEPTH, D, D_FF), D ** -0.5),
        'ffn_w3': n(ks[9], (DEPTH, D, D_FF), D ** -0.5),
        'ffn_w2': n(ks[10], (DEPTH, D_FF, D), BETA * D_FF ** -0.5),
        'ev_w_in': n(ks[11], (N_EVEN, D, D_IN_EVEN), D ** -0.5),
        'ev_shift_mu': jax.random.uniform(ks[12], (N_EVEN, 2, C_RW), f32, 0.0, 0.5),
        'ev_w0': jax.random.uniform(ks[13], (N_EVEN, 2, D_A), f32, -6.5, -1.5),
        'ev_w2': n(ks[14], (N_EVEN, 2, DECAY_LORA, D_A), 0.5 * DECAY_LORA ** -0.5),
        'ev_a0': n(ks[15], (N_EVEN, 2, D_A), 0.1),
        'ev_a2': n(ks[16], (N_EVEN, 2, ICL_LORA, D_A), 0.5 * ICL_LORA ** -0.5),
        'ev_g2': n(ks[17], (N_EVEN, GATE_LORA, D_A), GATE_LORA ** -0.5),
        'ev_k_k': 0.85 + n(ks[18], (N_EVEN, D_A), 0.05),
        'ev_k_a': 1.0 + n(ks[19], (N_EVEN, D_A), 0.05),
        'ev_r_k': n(ks[20], (N_EVEN, H_A, RWKV_HEAD), 0.1),
        'ev_lnx_g': 1.0 + n(ks[21], (N_EVEN, D_A), 0.05),
        'ev_lnx_b': n(ks[22], (N_EVEN, D_A), 0.02),
        'ev_pool_w': n(ks[23], (N_EVEN, len(POOL_WINDOWS), POOL_GROUP, POOL_GROUP), POOL_GROUP ** -0.5),
        'ev_pool_scale': 1.0 + n(ks[24], (N_EVEN, D_B), 0.1),
        'ev_w_out': n(ks[25], (N_EVEN, D, D), BETA * D ** -0.5),
        'od_w_in': n(ks[26], (N_ODD, D, 3 * D), D ** -0.5),
        'od_rpb': n(ks[27], (N_ODD, H_C, 2 * NA_KH - 1, 2 * NA_KW - 1), 0.05),
        'od_w_out': n(ks[28], (N_ODD, D, D), BETA * D ** -0.5),
    }


def reference(x, c, ctx, c_ctx, ada_w, ada_b, ln_g, ln_b, ffn_w1, ffn_w3, ffn_w2,
              ev_w_in, ev_shift_mu, ev_w0, ev_w2, ev_a0, ev_a2, ev_g2, ev_k_k, ev_k_a, ev_r_k,
              ev_lnx_g, ev_lnx_b, ev_pool_w, ev_pool_scale, ev_w_out,
              od_w_in, od_rpb, od_w_out):
    xc = ctx
    for i in range(DEPTH):
        last = i == DEPTH - 1
        want_ctx = not last
        j = i // 2
        mod = jax.nn.silu(c) @ ada_w[i] + ada_b[i]
        mod_c = jax.nn.silu(c_ctx) @ ada_w[i] + ada_b[i]
        sh_m, sc_m, g_m, sh_f, sc_f, g_f = jnp.split(mod[:, None, :], 6, axis=-1)
        shc_m, scc_m, gc_m, shc_f, scc_f, gc_f = jnp.split(mod_c, 6, axis=-1)
        h = x * (1.0 + sc_m) + sh_m
        hc = xc * (1.0 + scc_m) + shc_m
        if i % 2 == 0:
            y, yc = rwkv_pool_mixer(h, hc, want_ctx, ev_w_in[j], ev_shift_mu[j], ev_w0[j], ev_w2[j],
                                    ev_a0[j], ev_a2[j], ev_g2[j], ev_k_k[j], ev_k_a[j], ev_r_k[j],
                                    ev_lnx_g[j], ev_lnx_b[j], ev_pool_w[j], ev_pool_scale[j], ev_w_out[j])
        else:
            y, yc = na_mixer(h, hc, want_ctx, od_w_in[j], od_rpb[j], od_w_out[j])
        x = layer_norm(ALPHA * x + g_m * y, ln_g[i, 0], ln_b[i, 0])
        h = x * (1.0 + sc_f) + sh_f
        x = layer_norm(ALPHA * x + g_f * swiglu(h, ffn_w1[i], ffn_w3[i], ffn_w2[i]), ln_g[i, 1], ln_b[i, 1])
        if want_ctx:
            xc = layer_norm(ALPHA * xc + gc_m * yc, ln_g[i, 0], ln_b[i, 0])
            hc = xc * (1.0 + scc_f) + shc_f
            xc = layer_norm(ALPHA * xc + gc_f * swiglu(hc, ffn_w1[i], ffn_w3[i], ffn_w2[i]), ln_g[i, 1], ln_b[i, 1])
    return x
```

```python
import functools
import math

import jax
import jax.numpy as jnp
from jax import lax
from jax.experimental import pallas as pl
from jax.experimental.pallas import tpu as pltpu

F32 = jnp.float32
BF16 = jnp.bfloat16
HIGHEST = lax.Precision.HIGHEST

D_MODEL = 1024
DEPTH = 2
GRID_W = 64
ALPHA = (2 * DEPTH) ** 0.25
LN_EPS = 1e-6
HEAD = 64
D_A = 3 * D_MODEL // 4
H_A = D_A // HEAD
LORA_W = 64
LORA_A = 64
LORA_G = 128
GN_EPS = 64e-5
D_B = D_MODEL - D_A
POOL_WINDOWS = (2, 4, 8, 16)
POOL_GROUP = D_B // len(POOL_WINDOWS)
C_RW = 3 * D_A + 2 * LORA_W + 2 * LORA_A + LORA_G
D_IN_EVEN = C_RW + D_B
H_C = D_MODEL // HEAD
NA_KH = 8
NA_KW = 16

LANES = 128
SUBLANES = 8
VMEM_LIMIT_BYTES = 56 * 1024 * 1024

TILE_ROWS = 256
CHUNK = 64
NEG_BIAS = -1e30


def _sigmoid(x):
    return 1.0 / (1.0 + jnp.exp(-x))


def _layer_norm(z, g, b):
    mu = jnp.mean(z, axis=-1, keepdims=True)
    zc = z - mu
    var = jnp.mean(zc * zc, axis=-1, keepdims=True)
    return zc * lax.rsqrt(var + LN_EPS) * g + b


def _dot(a, b):
    return jnp.dot(a, b, preferred_element_type=F32)


def _dot_hi(a, b):
    return jnp.dot(a, b, precision=HIGHEST, preferred_element_type=F32)


def _dot_nt_hi(a, b):
    return lax.dot_general(a, b, (((1,), (1,)), ((), ())), precision=HIGHEST, preferred_element_type=F32)


def _dot_nt(a, b):
    return lax.dot_general(a, b, (((1,), (1,)), ((), ())), preferred_element_type=F32)


def _seg_sum(x, seg):
    hi = x.astype(BF16)
    lo = (x - hi.astype(F32)).astype(BF16)
    return _dot(hi, seg) + _dot(lo, seg)


def _params(*sem):
    return pltpu.CompilerParams(dimension_semantics=sem, vmem_limit_bytes=VMEM_LIMIT_BYTES)


def _const_spec(shape):
    nd = len(shape)
    return pl.BlockSpec(shape, lambda *_: (0,) * nd)


def _ada_kernel(cs_ref, w_ref, b_ref, o_ref):
    s = cs_ref[...]
    s = s * _sigmoid(s)
    o_ref[0] = _dot_hi(s, w_ref[0]) + b_ref[0]


def _ada_modulation(cs, ada_w, ada_b):
    depth, d, n = ada_w.shape
    nb = 1536
    return pl.pallas_call(
        _ada_kernel,
        grid=(depth, n // nb),
        in_specs=[
            pl.BlockSpec((SUBLANES, d), lambda i, j: (0, 0)),
            pl.BlockSpec((1, d, nb), lambda i, j: (i, 0, j)),
            pl.BlockSpec((1, 1, nb), lambda i, j: (i, 0, j)),
        ],
        out_specs=pl.BlockSpec((1, SUBLANES, nb), lambda i, j: (i, 0, j)),
        out_shape=jax.ShapeDtypeStruct((depth, SUBLANES, n), F32),
        compiler_params=_params("arbitrary", "arbitrary"),
        name="ada_modulation",
    )(cs, ada_w, ada_b.reshape(depth, 1, n))


def _modmm_kernel(x_ref, mod_ref, w_ref, o_ref):
    m = mod_ref[0]
    h = x_ref[...] * (1.0 + m[1:2]) + m[0:1]
    o_ref[...] = _dot(h.astype(BF16), w_ref[...]).astype(o_ref.dtype)


def _mod_matmul(x, mod, w, *, n_ctx_tiles, out_dtype):
    t, d = x.shape
    n = w.shape[1]
    tm = TILE_ROWS
    return pl.pallas_call(
        _modmm_kernel,
        grid=(t // tm,),
        in_specs=[
            pl.BlockSpec((tm, d), lambda i: (i, 0)),
            pl.BlockSpec((1, SUBLANES, d), lambda i: ((i >= n_ctx_tiles).astype(jnp.int32), 0, 0)),
            _const_spec((d, n)),
        ],
        out_specs=pl.BlockSpec((tm, n), lambda i: (i, 0)),
        out_shape=jax.ShapeDtypeStruct((t, n), out_dtype),
        compiler_params=_params("arbitrary"),
        name="mod_matmul",
    )(x, mod, w)


def _prep_kernel(p_ref, pp_ref, pn_ref, mu_ref, w0_ref, w2_ref, a0_ref, a2_ref, g2_ref, kk_ref, ka_ref,
                 rk_ref, seg_ref,
                 r_o, v_o, kn_o, ld_o, kd_o, bd_o, g_o, bonus_o, pool_o, *, tm, n_ctx_tiles, n_tiles):
    i = pl.program_id(0)
    first = jnp.logical_or(i == 0, i == n_ctx_tiles)
    last = jnp.logical_or(i == n_ctx_tiles - 1, i == n_tiles - 1)
    row = lax.broadcasted_iota(jnp.int32, (tm, 1), 0)

    pm = p_ref[:, :C_RW]
    prev_row = jnp.where(first, 0.0, pp_ref[SUBLANES - 1:SUBLANES, :C_RW])
    next_row = jnp.where(last, 0.0, pn_ref[0:1, :C_RW])
    prev = jnp.where(row == 0, prev_row, pltpu.roll(pm, 1, 0))
    nxt = jnp.where(row == tm - 1, next_row, pltpu.roll(pm, tm - 1, 0))
    pm = pm + (prev - pm) * mu_ref[0:1] + (nxt - pm) * mu_ref[1:2]

    s1, s2, s3 = D_A, 2 * D_A, 3 * D_A
    s4 = s3 + 2 * LORA_W
    s5 = s4 + 2 * LORA_A
    r = pm[:, :s1]
    k = pm[:, s1:s2]
    v = pm[:, s2:s3]
    wd = pm[:, s3:s4]
    ad = pm[:, s4:s5]
    gd = pm[:, s5:]

    lw = _dot_hi(jnp.tanh(wd), w2_ref[...])
    la = _dot_hi(ad, a2_ref[...])
    g = _dot_hi(_sigmoid(gd), g2_ref[...])

    kr = k * kk_ref[...]
    ss = _seg_sum(kr * kr, seg_ref[...])
    kn = kr / jnp.maximum(jnp.sqrt(ss), 1e-12)

    decay_scale = math.exp(-0.5)
    kd_sum = jnp.zeros_like(k)
    for d in range(2):
        sl = slice(d * D_A, (d + 1) * D_A)
        ld_o[d] = -decay_scale * _sigmoid(w0_ref[d:d + 1] + lw[:, sl])
        a = _sigmoid(a0_ref[d:d + 1] + la[:, sl])
        kd = k * (1.0 + (a - 1.0) * ka_ref[...])
        kd_o[d] = kd
        bd_o[d] = kn * a
        kd_sum = kd_sum + kd
    coef = _seg_sum(r * kd_sum * rk_ref[...], seg_ref[...])

    r_o[...] = r
    v_o[...] = v
    kn_o[...] = kn
    g_o[...] = g
    bonus_o[...] = coef * v

    xp = p_ref[:, C_RW:]
    xprev = jnp.where(first, 0.0, pp_ref[:, C_RW:])
    xnext = jnp.where(last, 0.0, pn_ref[:, C_RW:])
    xe = jnp.concatenate([xprev, xp, xnext], axis=0)
    n_ext = tm + 2 * SUBLANES

    def at_offset(a, s):
        return pltpu.roll(a, (-s) % n_ext, 0)

    sums = {}
    acc = xe + at_offset(xe, -1)
    sums[2] = acc
    half = 1
    for win in (4, 8, 16):
        acc = at_offset(acc, -half) + at_offset(acc, half)
        sums[win] = acc
        half *= 2

    in_ctx = i < n_ctx_tiles
    tile0 = jnp.where(in_ctx, 0, n_ctx_tiles)
    seq_len = jnp.where(in_ctx, n_ctx_tiles * tm, (n_tiles - n_ctx_tiles) * tm)
    t_seq = (i - tile0) * tm + row
    grp = lax.broadcasted_iota(jnp.int32, (1, D_B), 1) // POOL_GROUP
    mean = jnp.zeros((tm, D_B), F32)
    for gi, win in enumerate(POOL_WINDOWS):
        cnt = jnp.minimum(t_seq + win // 2, seq_len) - jnp.maximum(t_seq - win // 2, 0)
        m = sums[win][SUBLANES:SUBLANES + tm] / cnt.astype(F32)
        mean = jnp.where(grp == gi, m, mean)
    pool_o[...] = mean - xp


def _rwkv_prepare(p, mu, w0, w2bd, a0, a2bd, g2, k_k, k_a, r_k, seg, *, n_ctx_tiles):
    t, n_in = p.shape
    tm = TILE_ROWS
    n_tiles = t // tm
    halo_per_tile = tm // SUBLANES
    n_halo = t // SUBLANES
    tok = jax.ShapeDtypeStruct((t, D_A), F32)
    tok2 = jax.ShapeDtypeStruct((2, t, D_A), F32)
    tok_spec = pl.BlockSpec((tm, D_A), lambda i: (i, 0))
    tok2_spec = pl.BlockSpec((2, tm, D_A), lambda i: (0, i, 0))
    kern = functools.partial(_prep_kernel, tm=tm, n_ctx_tiles=n_ctx_tiles, n_tiles=n_tiles)
    return pl.pallas_call(
        kern,
        grid=(n_tiles,),
        in_specs=[
            pl.BlockSpec((tm, n_in), lambda i: (i, 0)),
            pl.BlockSpec((SUBLANES, n_in), lambda i: (jnp.maximum(i * halo_per_tile - 1, 0), 0)),
            pl.BlockSpec((SUBLANES, n_in), lambda i: (jnp.minimum((i + 1) * halo_per_tile, n_halo - 1), 0)),
            _const_spec(mu.shape), _const_spec(w0.shape), _const_spec(w2bd.shape), _const_spec(a0.shape),
            _const_spec(a2bd.shape), _const_spec(g2.shape), _const_spec(k_k.shape), _const_spec(k_a.shape),
            _const_spec(r_k.shape), _const_spec(seg.shape),
        ],
        out_specs=[tok_spec, tok_spec, tok_spec, tok2_spec, tok2_spec, tok2_spec, tok_spec, tok_spec,
                   pl.BlockSpec((tm, D_B), lambda i: (i, 0))],
        out_shape=[tok, tok, tok, tok2, tok2, tok2, tok, tok, jax.ShapeDtypeStruct((t, D_B), F32)],
        compiler_params=_params("arbitrary"),
        name="rwkv_prepare",
    )(p, p, p, mu, w0, w2bd, a0, a2bd, g2, k_k, k_a, r_k, seg)


def _wkv_kernel(r_ref, v_ref, kn_ref, ld_ref, kd_ref, bd_ref, y_ref, h_ref, *, chunks_per_block):
    rev = pl.program_id(0) == 1
    step = pl.program_id(2)

    @pl.when(step == 0)
    def _():
        h_ref[...] = jnp.zeros_like(h_ref)

    c = CHUNK
    sgn = jnp.where(rev, -1, 1)
    ri = lax.broadcasted_iota(jnp.int32, (c, c), 0)
    ci = lax.broadcasted_iota(jnp.int32, (c, c), 1)
    order = (ri - ci) * sgn
    strict = order > 0
    incl = order >= 0
    tri = incl.astype(F32)
    eye_c = (ri == ci).astype(F32)
    lane = lax.broadcasted_iota(jnp.int32, (1, LANES), 1)
    head0 = lane < HEAD
    r2 = lax.broadcasted_iota(jnp.int32, (LANES, LANES), 0)
    c2 = lax.broadcasted_iota(jnp.int32, (LANES, LANES), 1)
    same_head = (r2 < HEAD) == (c2 < HEAD)
    eye2 = r2 == c2

    for j in range(chunks_per_block):
        cj = jnp.where(rev, chunks_per_block - 1 - j, j)
        rows = pl.ds(pl.multiple_of(cj * c, c), c)
        r = r_ref[rows, :]
        v = v_ref[rows, :]
        kn = kn_ref[rows, :]
        ld = ld_ref[0, rows, :]
        kd = kd_ref[0, rows, :]
        bd = bd_ref[0, rows, :]

        cum = _dot_hi(tri, ld)
        cum_ex = cum - ld
        total = jnp.sum(ld, axis=0, keepdims=True)
        kq = kn * jnp.exp(cum_ex)
        rq = r * jnp.exp(cum)
        inv = jnp.exp(-cum)
        bk = bd * inv
        kk = kd * inv
        to_end = jnp.exp(total - cum)
        b_end = bd * to_end
        k_end = kd * to_end

        w_parts, ut_parts, pkv_parts, pb_parts = [], [], [], []
        for h in range(2):
            mh = head0 if h == 0 else jnp.logical_not(head0)
            kq_h = jnp.where(mh, kq, 0.0)
            rq_h = jnp.where(mh, rq, 0.0)
            a = jnp.where(strict, _dot_nt_hi(kq_h, bk), 0.0)
            bm = jnp.where(strict, _dot_nt_hi(kq_h, kk), 0.0)
            pb = jnp.where(incl, _dot_nt_hi(rq_h, bk), 0.0)
            pk = jnp.where(incl, _dot_nt_hi(rq_h, kk), 0.0)
            t_inv = eye_c - a
            pw = a
            for _ in range(int(math.log2(c)) - 1):
                pw = _dot_hi(pw, pw)
                t_inv = t_inv + _dot_hi(t_inv, pw)
            w_parts.append(_dot_hi(t_inv, kq))
            ut_parts.append(_dot_hi(t_inv, _dot_hi(bm, v)))
            pkv_parts.append(_dot_hi(pk, v))
            pb_parts.append(pb)
        w = jnp.where(head0, w_parts[0], w_parts[1])
        ut = jnp.where(head0, ut_parts[0], ut_parts[1])
        pkv = jnp.where(head0, pkv_parts[0], pkv_parts[1])

        hs = h_ref[...]
        u = -_dot_hi(w, hs) - ut
        y = _dot_hi(rq, hs) + pkv + jnp.where(head0, _dot_hi(pb_parts[0], u), _dot_hi(pb_parts[1], u))
        y_ref[0, rows, :] = y

        decay = jnp.where(eye2, jnp.exp(total), 0.0)
        ends_t = jnp.concatenate([b_end, k_end], axis=0).T
        upd = _dot_hi(ends_t, jnp.concatenate([u, v], axis=0))
        h_ref[...] = _dot_hi(decay, hs) + jnp.where(same_head, upd, 0.0)


def _wkv_scan(r, v, kn, ld, kd, bd, *, n_ctx_blocks):
    t, da = r.shape
    rows = TILE_ROWS
    n_blocks = t // rows
    n_pairs = da // LANES

    def blk(d, s):
        rev_blk = jnp.where(s < n_ctx_blocks, n_ctx_blocks - 1 - s, n_blocks - 1 - (s - n_ctx_blocks))
        return jnp.where(d == 0, s, rev_blk)

    shared = pl.BlockSpec((rows, LANES), lambda d, p, s: (blk(d, s), p))
    per_dir = pl.BlockSpec((1, rows, LANES), lambda d, p, s: (d, blk(d, s), p))
    kern = functools.partial(_wkv_kernel, chunks_per_block=rows // CHUNK)
    return pl.pallas_call(
        kern,
        grid=(2, n_pairs, n_blocks),
        in_specs=[shared, shared, shared, per_dir, per_dir, per_dir],
        out_specs=per_dir,
        out_shape=jax.ShapeDtypeStruct((2, t, da), F32),
        scratch_shapes=[pltpu.VMEM((LANES, LANES), F32)],
        compiler_params=_params("arbitrary", "arbitrary", "arbitrary"),
        name="wkv_scan",
    )(r, v, kn, ld, kd, bd)


def _readout_kernel(y_ref, bonus_ref, g_ref, pool_ref, x_ref, mod_ref, seg_ref, lnxg_ref, lnxb_ref,
                    poolw_ref, pools_ref, wout_ref, lng_ref, lnb_ref, o_ref):
    y = y_ref[0] + y_ref[1]
    seg = seg_ref[...]
    mu = _seg_sum(y, seg) * (1.0 / HEAD)
    yc = y - mu
    var = _seg_sum(yc * yc, seg) * (1.0 / HEAD)
    yn = yc * lax.rsqrt(var + GN_EPS) * lnxg_ref[...] + lnxb_ref[...]
    a = (yn + bonus_ref[...]) * g_ref[...]
    b = _dot(pool_ref[...].astype(BF16), poolw_ref[...]) * pools_ref[...]
    out = _dot(a.astype(BF16), wout_ref[:D_A, :]) + _dot(b.astype(BF16), wout_ref[D_A:, :])
    m = mod_ref[0]
    o_ref[...] = _layer_norm(ALPHA * x_ref[...] + m[2:3] * out, lng_ref[...], lnb_ref[...])


def _rwkv_readout(y, bonus, g, pooled, x, mod, seg, lnx_g, lnx_b, pool_wbd, pool_scale, w_out, ln_g, ln_b,
                  *, n_ctx_tiles):
    t, d = x.shape
    tm = TILE_ROWS
    tok = pl.BlockSpec((tm, D_A), lambda i: (i, 0))
    return pl.pallas_call(
        _readout_kernel,
        grid=(t // tm,),
        in_specs=[
            pl.BlockSpec((2, tm, D_A), lambda i: (0, i, 0)), tok, tok,
            pl.BlockSpec((tm, D_B), lambda i: (i, 0)),
            pl.BlockSpec((tm, d), lambda i: (i, 0)),
            pl.BlockSpec((1, SUBLANES, d), lambda i: ((i >= n_ctx_tiles).astype(jnp.int32), 0, 0)),
            _const_spec(seg.shape), _const_spec(lnx_g.shape), _const_spec(lnx_b.shape),
            _const_spec(pool_wbd.shape), _const_spec(pool_scale.shape), _const_spec(w_out.shape),
            _const_spec(ln_g.shape), _const_spec(ln_b.shape),
        ],
        out_specs=pl.BlockSpec((tm, d), lambda i: (i, 0)),
        out_shape=jax.ShapeDtypeStruct((t, d), F32),
        compiler_params=_params("arbitrary"),
        name="rwkv_readout",
    )(y, bonus, g, pooled, x, mod, seg, lnx_g, lnx_b, pool_wbd, pool_scale, w_out, ln_g, ln_b)


def _ffn_kernel(x_ref, mod_ref, w1_ref, w3_ref, w2_ref, lng_ref, lnb_ref, o_ref):
    m = mod_ref[0]
    x = x_ref[...]
    h = (x * (1.0 + m[4:5]) + m[3:4]).astype(BF16)
    a = _dot(h, w1_ref[...])
    b = _dot(h, w3_ref[...])
    act = (a * _sigmoid(a) * b).astype(BF16)
    out = _dot(act, w2_ref[...])
    o_ref[...] = _layer_norm(ALPHA * x + m[5:6] * out, lng_ref[...], lnb_ref[...])


def _ffn(x, mod, w1, w3, w2, ln_g, ln_b, *, n_ctx_tiles, row_block_offset, n_tiles):
    d = x.shape[1]
    tm = TILE_ROWS
    single = pl.Buffered(1)
    return pl.pallas_call(
        _ffn_kernel,
        grid=(n_tiles,),
        in_specs=[
            pl.BlockSpec((tm, d), lambda i: (i + row_block_offset, 0)),
            pl.BlockSpec((1, SUBLANES, d),
                         lambda i: ((i + row_block_offset >= n_ctx_tiles).astype(jnp.int32), 0, 0)),
            pl.BlockSpec(w1.shape, lambda i: (0, 0), pipeline_mode=single),
            pl.BlockSpec(w3.shape, lambda i: (0, 0), pipeline_mode=single),
            pl.BlockSpec(w2.shape, lambda i: (0, 0), pipeline_mode=single),
            _const_spec(ln_g.shape), _const_spec(ln_b.shape),
        ],
        out_specs=pl.BlockSpec((tm, d), lambda i: (i, 0)),
        out_shape=jax.ShapeDtypeStruct((n_tiles * tm, d), F32),
        compiler_params=_params("arbitrary"),
        name="ffn",
    )(x, mod, w1, w3, w2, ln_g, ln_b)


def _attn_kernel(q_ref, k_ref, v_ref, bias_ref, o_ref, *, q_rows, n_rows, n_ctx, kh):
    rb = pl.program_id(1)
    scale = HEAD ** -0.5
    lane = lax.broadcasted_iota(jnp.int32, (1, LANES), 1)
    head0 = lane < HEAD
    kc = k_ref[0:n_ctx, :]
    vc = v_ref[0:n_ctx, :]
    n_loc = kh * GRID_W

    def body(j, carry):
        rr = rb * q_rows + j
        sr = jnp.clip(rr - kh // 2, 0, n_rows - kh)
        t_var = sr - rr + (NA_KH - 1)
        qrow = pl.ds(pl.multiple_of(j * GRID_W, GRID_W), GRID_W)
        q = q_ref[qrow, :]
        krows = pl.ds(pl.multiple_of(n_ctx + sr * GRID_W, GRID_W), n_loc)
        kb = k_ref[krows, :]
        vb = v_ref[krows, :]
        outs = []
        for h in range(2):
            mh = head0 if h == 0 else jnp.logical_not(head0)
            qh = jnp.where(mh, q, jnp.zeros_like(q))
            s_loc = _dot_nt(qh, kb) * scale + bias_ref[h, t_var]
            s_ctx = _dot_nt(qh, kc) * scale
            m = jnp.maximum(jnp.max(s_loc, axis=-1, keepdims=True), jnp.max(s_ctx, axis=-1, keepdims=True))
            p_loc = jnp.exp(s_loc - m)
            p_ctx = jnp.exp(s_ctx - m)
            denom = jnp.sum(p_loc, axis=-1, keepdims=True) + jnp.sum(p_ctx, axis=-1, keepdims=True)
            o = _dot(p_loc.astype(BF16), vb) + _dot(p_ctx.astype(BF16), vc)
            outs.append(o / denom)
        o_ref[qrow, :] = jnp.where(head0, outs[0], outs[1]).astype(o_ref.dtype)
        return carry

    lax.fori_loop(0, q_rows, body, 0)


def _neighbourhood_attention(qkv, bias_tab, *, n_ctx, n_lat):
    t_all = qkv.shape[0]
    d = D_MODEL
    n_rows = n_lat // GRID_W
    kh = min(NA_KH, n_rows)
    q_rows = TILE_ROWS // GRID_W
    n_pairs = d // LANES
    ctx_blocks = n_ctx // TILE_ROWS
    kern = functools.partial(_attn_kernel, q_rows=q_rows, n_rows=n_rows, n_ctx=n_ctx, kh=kh)
    n_var = bias_tab.shape[1]
    return pl.pallas_call(
        kern,
        grid=(n_pairs, n_lat // TILE_ROWS),
        in_specs=[
            pl.BlockSpec((TILE_ROWS, LANES), lambda p, b: (b + ctx_blocks, p)),
            pl.BlockSpec((t_all, LANES), lambda p, b: (0, n_pairs + p)),
            pl.BlockSpec((t_all, LANES), lambda p, b: (0, 2 * n_pairs + p)),
            pl.BlockSpec((2, n_var, GRID_W, kh * GRID_W), lambda p, b: (p, 0, 0, 0)),
        ],
        out_specs=pl.BlockSpec((TILE_ROWS, LANES), lambda p, b: (b, p)),
        out_shape=jax.ShapeDtypeStruct((n_lat, d), BF16),
        compiler_params=_params("arbitrary", "arbitrary"),
        name="neighbourhood_attention",
    )(qkv, qkv, qkv, bias_tab)


def _attention_bias_table(rpb, n_rows):
    kh = min(NA_KH, n_rows)
    cols = jnp.arange(GRID_W)
    col_start = jnp.clip(cols - NA_KW // 2, 0, GRID_W - NA_KW)
    key_col = jnp.arange(GRID_W)
    in_win = (key_col[None, :] >= col_start[:, None]) & (key_col[None, :] < col_start[:, None] + NA_KW)
    col_off = jnp.clip(key_col[None, :] - cols[:, None] + (NA_KW - 1), 0, 2 * NA_KW - 2)
    n_var = NA_KH
    row_off = jnp.clip(jnp.arange(n_var)[:, None] + jnp.arange(kh)[None, :], 0, 2 * NA_KH - 2)
    tab = rpb[:, row_off[:, :, None, None], col_off[None, None, :, :]]
    tab = jnp.where(in_win[None, None, None], tab, NEG_BIAS)
    tab = tab.transpose(0, 1, 3, 2, 4).reshape(rpb.shape[0], n_var, GRID_W, kh * GRID_W)
    return tab.astype(F32)


def _proj_ln_kernel(o_ref, x_ref, mod_ref, w_ref, lng_ref, lnb_ref, out_ref):
    m = mod_ref[0]
    y = _dot(o_ref[...], w_ref[...])
    out_ref[...] = _layer_norm(ALPHA * x_ref[...] + m[2:3] * y, lng_ref[...], lnb_ref[...])


def _proj_ln(o, x, mod, w, ln_g, ln_b, *, row_block_offset):
    t, d = o.shape
    tm = TILE_ROWS
    return pl.pallas_call(
        _proj_ln_kernel,
        grid=(t // tm,),
        in_specs=[
            pl.BlockSpec((tm, d), lambda i: (i, 0)),
            pl.BlockSpec((tm, d), lambda i: (i + row_block_offset, 0)),
            pl.BlockSpec((1, SUBLANES, d), lambda i: (1, 0, 0)),
            _const_spec(w.shape), _const_spec(ln_g.shape), _const_spec(ln_b.shape),
        ],
        out_specs=pl.BlockSpec((tm, d), lambda i: (i, 0)),
        out_shape=jax.ShapeDtypeStruct((t, d), F32),
        compiler_params=_params("arbitrary"),
        name="attn_proj_ln",
    )(o, x, mod, w, ln_g, ln_b)


def _block_diag2(w):
    z = jnp.zeros_like(w[0])
    return jnp.concatenate([jnp.concatenate([w[0], z], axis=1), jnp.concatenate([z, w[1]], axis=1)], axis=0)


def kernel(x, c, ctx, c_ctx, ada_w, ada_b, ln_g, ln_b, ffn_w1, ffn_w3, ffn_w2, ev_w_in, ev_shift_mu, ev_w0,
           ev_w2, ev_a0, ev_a2, ev_g2, ev_k_k, ev_k_a, ev_r_k, ev_lnx_g, ev_lnx_b, ev_pool_w, ev_pool_scale,
           ev_w_out, od_w_in, od_rpb, od_w_out):
    batch, n_lat, d = x.shape
    n_ctx = ctx.shape[1]
    assert batch == 1 and d == D_MODEL
    assert n_ctx % TILE_ROWS == 0 and n_lat % TILE_ROWS == 0 and n_lat % GRID_W == 0
    n_ctx_tiles = n_ctx // TILE_ROWS
    n_lat_tiles = n_lat // TILE_ROWS

    cs = jnp.zeros((SUBLANES, d), F32).at[0].set(c_ctx).at[1].set(c[0])
    mod_all = _ada_modulation(cs, ada_w, ada_b)
    mod_all = mod_all[:, :2].reshape(DEPTH, 2, 6, d)
    mod_all = jnp.pad(mod_all, ((0, 0), (0, 0), (0, SUBLANES - 6), (0, 0)))

    xa = jnp.concatenate([ctx[0], x[0]], axis=0)

    mod = mod_all[0]
    p = _mod_matmul(xa, mod, ev_w_in[0].astype(BF16), n_ctx_tiles=n_ctx_tiles, out_dtype=F32)
    seg = (jnp.arange(D_A)[:, None] // HEAD == jnp.arange(D_A)[None, :] // HEAD).astype(BF16)
    r, v, kn, ld, kd, bd, g, bonus, pooled = _rwkv_prepare(
        p, ev_shift_mu[0], ev_w0[0], _block_diag2(ev_w2[0]), ev_a0[0], _block_diag2(ev_a2[0]), ev_g2[0],
        ev_k_k[0].reshape(1, D_A), ev_k_a[0].reshape(1, D_A), ev_r_k[0].reshape(1, D_A), seg,
        n_ctx_tiles=n_ctx_tiles)
    y = _wkv_scan(r, v, kn, ld, kd, bd, n_ctx_blocks=n_ctx_tiles)
    pw = ev_pool_w[0]
    pool_wbd = jnp.zeros((D_B, D_B), F32)
    for gi in range(len(POOL_WINDOWS)):
        sl = slice(gi * POOL_GROUP, (gi + 1) * POOL_GROUP)
        pool_wbd = pool_wbd.at[sl, sl].set(pw[gi])
    xa = _rwkv_readout(y, bonus, g, pooled, xa, mod, seg, ev_lnx_g[0].reshape(1, D_A),
                       ev_lnx_b[0].reshape(1, D_A), pool_wbd.astype(BF16), ev_pool_scale[0].reshape(1, D_B),
                       ev_w_out[0].astype(BF16), ln_g[0, 0].reshape(1, d), ln_b[0, 0].reshape(1, d),
                       n_ctx_tiles=n_ctx_tiles)
    xa = _ffn(xa, mod, ffn_w1[0].astype(BF16), ffn_w3[0].astype(BF16), ffn_w2[0].astype(BF16),
              ln_g[0, 1].reshape(1, d), ln_b[0, 1].reshape(1, d), n_ctx_tiles=n_ctx_tiles,
              row_block_offset=0, n_tiles=n_ctx_tiles + n_lat_tiles)

    mod = mod_all[1]
    qkv = _mod_matmul(xa, mod, od_w_in[0].astype(BF16), n_ctx_tiles=n_ctx_tiles, out_dtype=BF16)
    bias_tab = _attention_bias_table(od_rpb[0], n_lat // GRID_W)
    o = _neighbourhood_attention(qkv, bias_tab, n_ctx=n_ctx, n_lat=n_lat)
    xl = _proj_ln(o, xa, mod, od_w_out[0].astype(BF16), ln_g[1, 0].reshape(1, d), ln_b[1, 0].reshape(1, d),
                  row_block_offset=n_ctx_tiles)
    xl = _ffn(xl, mod, ffn_w1[1].astype(BF16), ffn_w3[1].astype(BF16), ffn_w2[1].astype(BF16),
              ln_g[1, 1].reshape(1, d), ln_b[1, 1].reshape(1, d), n_ctx_tiles=0, row_block_offset=0,
              n_tiles=n_lat_tiles)
    return xl[None]
```

```python
import functools
import math

import jax
import jax.numpy as jnp
import numpy as np
from jax import lax
from jax.experimental import pallas as pl
from jax.experimental.pallas import tpu as pltpu

F32 = jnp.float32
BF16 = jnp.bfloat16
HIGHEST = lax.Precision.HIGHEST

D_MODEL = 1024
DEPTH = 2
GRID_W = 64
ALPHA = (2 * DEPTH) ** 0.25
LN_EPS = 1e-6
HEAD = 64
D_A = 3 * D_MODEL // 4
H_A = D_A // HEAD
LORA_W = 64
LORA_A = 64
LORA_G = 128
GN_EPS = 64e-5
D_B = D_MODEL - D_A
POOL_WINDOWS = (2, 4, 8, 16)
POOL_GROUP = D_B // len(POOL_WINDOWS)
C_RW = 3 * D_A + 2 * LORA_W + 2 * LORA_A + LORA_G
D_IN_EVEN = C_RW + D_B
H_C = D_MODEL // HEAD
NA_KH = 8
NA_KW = 16

LANES = 128
SUBLANES = 8
VMEM_LIMIT_BYTES = 56 * 1024 * 1024

TILE_ROWS = 256
CHUNK = 64
NEG_BIAS = -1e30


def _sigmoid(x):
    return 1.0 / (1.0 + jnp.exp(-x))


def _layer_norm(z, g, b):
    mu = jnp.mean(z, axis=-1, keepdims=True)
    zc = z - mu
    var = jnp.mean(zc * zc, axis=-1, keepdims=True)
    return zc * lax.rsqrt(var + LN_EPS) * g + b


def _dot(a, b):
    return jnp.dot(a, b, preferred_element_type=F32)


def _dot_hi(a, b):
    return jnp.dot(a, b, precision=HIGHEST, preferred_element_type=F32)


def _dot_nt_hi(a, b):
    return lax.dot_general(a, b, (((1,), (1,)), ((), ())), precision=HIGHEST, preferred_element_type=F32)


def _dot_nt(a, b):
    return lax.dot_general(a, b, (((1,), (1,)), ((), ())), preferred_element_type=F32)


def _seg_sum(x, seg):
    hi = x.astype(BF16)
    lo = (x - hi.astype(F32)).astype(BF16)
    return _dot(hi, seg) + _dot(lo, seg)


def _params(*sem):
    return pltpu.CompilerParams(dimension_semantics=sem, vmem_limit_bytes=VMEM_LIMIT_BYTES)


def _const_spec(shape):
    nd = len(shape)
    return pl.BlockSpec(shape, lambda *_: (0,) * nd)


def _ada_kernel(cs_ref, w_ref, b_ref, o_ref):
    s = cs_ref[...]
    s = s * _sigmoid(s)
    o_ref[0] = _dot_hi(s, w_ref[0]) + b_ref[0]


def _ada_modulation(cs, ada_w, ada_b):
    depth, d, n = ada_w.shape
    nb = 1536
    return pl.pallas_call(
        _ada_kernel,
        grid=(depth, n // nb),
        in_specs=[
            pl.BlockSpec((SUBLANES, d), lambda i, j: (0, 0)),
            pl.BlockSpec((1, d, nb), lambda i, j: (i, 0, j)),
            pl.BlockSpec((1, 1, nb), lambda i, j: (i, 0, j)),
        ],
        out_specs=pl.BlockSpec((1, SUBLANES, nb), lambda i, j: (i, 0, j)),
        out_shape=jax.ShapeDtypeStruct((depth, SUBLANES, n), F32),
        compiler_params=_params("arbitrary", "arbitrary"),
        name="ada_modulation",
    )(cs, ada_w, ada_b.reshape(depth, 1, n))


def _modmm_kernel(x_ref, mod_ref, w_ref, o_ref):
    m = mod_ref[0]
    h = x_ref[...] * (1.0 + m[1:2]) + m[0:1]
    o_ref[...] = _dot(h.astype(BF16), w_ref[...]).astype(o_ref.dtype)


def _mod_matmul(x, mod, w, *, n_ctx_tiles, out_dtype):
    t, d = x.shape
    n = w.shape[1]
    tm = TILE_ROWS
    return pl.pallas_call(
        _modmm_kernel,
        grid=(t // tm,),
        in_specs=[
            pl.BlockSpec((tm, d), lambda i: (i, 0)),
            pl.BlockSpec((1, SUBLANES, d), lambda i: ((i >= n_ctx_tiles).astype(jnp.int32), 0, 0)),
            _const_spec((d, n)),
        ],
        out_specs=pl.BlockSpec((tm, n), lambda i: (i, 0)),
        out_shape=jax.ShapeDtypeStruct((t, n), out_dtype),
        compiler_params=_params("arbitrary"),
        name="mod_matmul",
    )(x, mod, w)


def _prep_kernel(p_ref, pp_ref, pn_ref, mu_ref, w0_ref, w2_ref, a0_ref, a2_ref, g2_ref, kk_ref, ka_ref,
                 rk_ref, seg_ref,
                 r_o, v_o, kn_o, ld_o, kd_o, bd_o, g_o, bonus_o, pool_o, *, tm, n_ctx_tiles, n_tiles):
    i = pl.program_id(0)
    first = jnp.logical_or(i == 0, i == n_ctx_tiles)
    last = jnp.logical_or(i == n_ctx_tiles - 1, i == n_tiles - 1)
    row = lax.broadcasted_iota(jnp.int32, (tm, 1), 0)

    pm = p_ref[:, :C_RW]
    prev_row = jnp.where(first, 0.0, pp_ref[SUBLANES - 1:SUBLANES, :C_RW])
    next_row = jnp.where(last, 0.0, pn_ref[0:1, :C_RW])
    prev = jnp.where(row == 0, prev_row, pltpu.roll(pm, 1, 0))
    nxt = jnp.where(row == tm - 1, next_row, pltpu.roll(pm, tm - 1, 0))
    pm = pm + (prev - pm) * mu_ref[0:1] + (nxt - pm) * mu_ref[1:2]

    s1, s2, s3 = D_A, 2 * D_A, 3 * D_A
    s4 = s3 + 2 * LORA_W
    s5 = s4 + 2 * LORA_A
    r = pm[:, :s1]
    k = pm[:, s1:s2]
    v = pm[:, s2:s3]
    wd = pm[:, s3:s4]
    ad = pm[:, s4:s5]
    gd = pm[:, s5:]

    lw = _dot_hi(jnp.tanh(wd), w2_ref[...])
    la = _dot_hi(ad, a2_ref[...])
    g = _dot_hi(_sigmoid(gd), g2_ref[...])

    kr = k * kk_ref[...]
    ss = _seg_sum(kr * kr, seg_ref[...])
    kn = kr / jnp.maximum(jnp.sqrt(ss), 1e-12)

    decay_scale = math.exp(-0.5)
    kd_sum = jnp.zeros_like(k)
    for d in range(2):
        sl = slice(d * D_A, (d + 1) * D_A)
        ld_o[d] = -decay_scale * _sigmoid(w0_ref[d:d + 1] + lw[:, sl])
        a = _sigmoid(a0_ref[d:d + 1] + la[:, sl])
        kd = k * (1.0 + (a - 1.0) * ka_ref[...])
        kd_o[d] = kd
        bd_o[d] = kn * a
        kd_sum = kd_sum + kd
    coef = _seg_sum(r * kd_sum * rk_ref[...], seg_ref[...])

    r_o[...] = r
    v_o[...] = v
    kn_o[...] = kn
    g_o[...] = g
    bonus_o[...] = coef * v

    xp = p_ref[:, C_RW:]
    xprev = jnp.where(first, 0.0, pp_ref[:, C_RW:])
    xnext = jnp.where(last, 0.0, pn_ref[:, C_RW:])
    xe = jnp.concatenate([xprev, xp, xnext], axis=0)
    n_ext = tm + 2 * SUBLANES

    def at_offset(a, s):
        return pltpu.roll(a, (-s) % n_ext, 0)

    sums = {}
    acc = xe + at_offset(xe, -1)
    sums[2] = acc
    half = 1
    for win in (4, 8, 16):
        acc = at_offset(acc, -half) + at_offset(acc, half)
        sums[win] = acc
        half *= 2

    in_ctx = i < n_ctx_tiles
    tile0 = jnp.where(in_ctx, 0, n_ctx_tiles)
    seq_len = jnp.where(in_ctx, n_ctx_tiles * tm, (n_tiles - n_ctx_tiles) * tm)
    t_seq = (i - tile0) * tm + row
    grp = lax.broadcasted_iota(jnp.int32, (1, D_B), 1) // POOL_GROUP
    mean = jnp.zeros((tm, D_B), F32)
    for gi, win in enumerate(POOL_WINDOWS):
        cnt = jnp.minimum(t_seq + win // 2, seq_len) - jnp.maximum(t_seq - win // 2, 0)
        m = sums[win][SUBLANES:SUBLANES + tm] / cnt.astype(F32)
        mean = jnp.where(grp == gi, m, mean)
    pool_o[...] = mean - xp


def _rwkv_prepare(p, mu, w0, w2bd, a0, a2bd, g2, k_k, k_a, r_k, seg, *, n_ctx_tiles):
    t, n_in = p.shape
    tm = TILE_ROWS
    n_tiles = t // tm
    halo_per_tile = tm // SUBLANES
    n_halo = t // SUBLANES
    tok = jax.ShapeDtypeStruct((t, D_A), F32)
    tok2 = jax.ShapeDtypeStruct((2, t, D_A), F32)
    tok_spec = pl.BlockSpec((tm, D_A), lambda i: (i, 0))
    tok2_spec = pl.BlockSpec((2, tm, D_A), lambda i: (0, i, 0))
    kern = functools.partial(_prep_kernel, tm=tm, n_ctx_tiles=n_ctx_tiles, n_tiles=n_tiles)
    return pl.pallas_call(
        kern,
        grid=(n_tiles,),
        in_specs=[
            pl.BlockSpec((tm, n_in), lambda i: (i, 0)),
            pl.BlockSpec((SUBLANES, n_in), lambda i: (jnp.maximum(i * halo_per_tile - 1, 0), 0)),
            pl.BlockSpec((SUBLANES, n_in), lambda i: (jnp.minimum((i + 1) * halo_per_tile, n_halo - 1), 0)),
            _const_spec(mu.shape), _const_spec(w0.shape), _const_spec(w2bd.shape), _const_spec(a0.shape),
            _const_spec(a2bd.shape), _const_spec(g2.shape), _const_spec(k_k.shape), _const_spec(k_a.shape),
            _const_spec(r_k.shape), _const_spec(seg.shape),
        ],
        out_specs=[tok_spec, tok_spec, tok_spec, tok2_spec, tok2_spec, tok2_spec, tok_spec, tok_spec,
                   pl.BlockSpec((tm, D_B), lambda i: (i, 0))],
        out_shape=[tok, tok, tok, tok2, tok2, tok2, tok, tok, jax.ShapeDtypeStruct((t, D_B), F32)],
        compiler_params=_params("arbitrary"),
        name="rwkv_prepare",
    )(p, p, p, mu, w0, w2bd, a0, a2bd, g2, k_k, k_a, r_k, seg)


def _wkv_kernel(r_ref, v_ref, kn_ref, ld_ref, kd_ref, bd_ref, y_ref, h_ref, *, chunks_per_block):
    rev = pl.program_id(0) == 1
    step = pl.program_id(2)

    @pl.when(step == 0)
    def _():
        h_ref[...] = jnp.zeros_like(h_ref)

    c = CHUNK
    sgn = jnp.where(rev, -1, 1)
    ri = lax.broadcasted_iota(jnp.int32, (c, c), 0)
    ci = lax.broadcasted_iota(jnp.int32, (c, c), 1)
    tri = jnp.where((ri - ci) * sgn >= 0, 1.0, 0.0).astype(BF16)
    r2 = lax.broadcasted_iota(jnp.int32, (LANES, LANES), 0)
    c2 = lax.broadcasted_iota(jnp.int32, (LANES, LANES), 1)
    order = jnp.where((r2 < c) == (c2 < c), ((r2 & (c - 1)) - (c2 & (c - 1))) * sgn, -1)
    strict = order > 0
    incl = order >= 0
    eye2 = jnp.where(r2 == c2, 1.0, 0.0)
    head0 = lax.broadcasted_iota(jnp.int32, (1, LANES), 1) < HEAD

    def stack(x):
        return jnp.concatenate([jnp.where(head0, x, 0.0), jnp.where(head0, 0.0, x)], axis=0)

    for j in range(chunks_per_block):
        cj = jnp.where(rev, chunks_per_block - 1 - j, j)
        rows = pl.ds(pl.multiple_of(cj * c, c), c)
        r = r_ref[rows, :]
        v = v_ref[rows, :]
        kn = kn_ref[rows, :]
        ld = ld_ref[0, rows, :]
        kd = kd_ref[0, rows, :]
        bd = bd_ref[0, rows, :]

        ld_hi = ld.astype(BF16)
        rem = ld - ld_hi.astype(F32)
        ld_mid = rem.astype(BF16)
        ld_lo = (rem - ld_mid.astype(F32)).astype(BF16)
        cum = _dot(tri, ld_hi) + _dot(tri, ld_mid) + _dot(tri, ld_lo)
        total = jnp.sum(ld, axis=0, keepdims=True)
        kq = kn * jnp.exp(cum - ld)
        rq = r * jnp.exp(cum)
        inv = jnp.exp(-cum)
        to_end = jnp.exp(total - cum)

        kq2 = stack(kq).astype(BF16)
        rq2 = stack(rq).astype(BF16)
        v2 = stack(v).astype(BF16)
        bkb = (bd * inv).astype(BF16)
        kkb = (kd * inv).astype(BF16)
        g = _dot_nt(jnp.concatenate([kq2, rq2], axis=0), jnp.concatenate([bkb, bkb, kkb, kkb], axis=0))
        a = jnp.where(strict, g[:LANES, :LANES], 0.0)
        bm = jnp.where(strict, g[:LANES, LANES:], 0.0)
        pb = jnp.where(incl, g[LANES:, :LANES], 0.0)
        pk = jnp.where(incl, g[LANES:, LANES:], 0.0)

        t_inv = eye2 - a
        pw = a
        for _ in range(int(math.log2(c)) - 1):
            pwb = pw.astype(BF16)
            pw = _dot(pwb, pwb)
            t_inv = t_inv + _dot(t_inv.astype(BF16), pw.astype(BF16))
        bmv = _dot(bm.astype(BF16), v2)
        wu = _dot(t_inv.astype(BF16), jnp.concatenate([kq2, bmv.astype(BF16)], axis=1))
        pkv = _dot(pk.astype(BF16), v2)

        hs = h_ref[...]
        m1 = _dot(jnp.concatenate([wu[:, :LANES].astype(BF16), rq2], axis=0), hs.astype(BF16))
        u2 = -m1[:LANES] - wu[:, LANES:]
        u2b = u2.astype(BF16)
        y2 = m1[LANES:] + pkv + _dot(pb.astype(BF16), u2b)
        y_ref[0, rows, :] = y2[:c] + y2[c:]

        ends_t = jnp.concatenate([stack(bd * to_end), stack(kd * to_end)], axis=0).T.astype(BF16)
        upd = _dot(ends_t, jnp.concatenate([u2b, v2], axis=0))
        decay = jnp.exp(jnp.sum(ld.T, axis=1, keepdims=True))
        h_ref[...] = hs * decay + upd


def _wkv_scan(r, v, kn, ld, kd, bd, *, n_ctx_blocks):
    t, da = r.shape
    rows = TILE_ROWS
    n_blocks = t // rows
    n_pairs = da // LANES

    def blk(d, s):
        rev_blk = jnp.where(s < n_ctx_blocks, n_ctx_blocks - 1 - s, n_blocks - 1 - (s - n_ctx_blocks))
        return jnp.where(d == 0, s, rev_blk)

    shared = pl.BlockSpec((rows, LANES), lambda d, p, s: (blk(d, s), p))
    per_dir = pl.BlockSpec((1, rows, LANES), lambda d, p, s: (d, blk(d, s), p))
    kern = functools.partial(_wkv_kernel, chunks_per_block=rows // CHUNK)
    return pl.pallas_call(
        kern,
        grid=(2, n_pairs, n_blocks),
        in_specs=[shared, shared, shared, per_dir, per_dir, per_dir],
        out_specs=per_dir,
        out_shape=jax.ShapeDtypeStruct((2, t, da), F32),
        scratch_shapes=[pltpu.VMEM((LANES, LANES), F32)],
        compiler_params=_params("arbitrary", "arbitrary", "arbitrary"),
        name="wkv_scan",
    )(r, v, kn, ld, kd, bd)


def _readout_kernel(y_ref, bonus_ref, g_ref, pool_ref, x_ref, mod_ref, seg_ref, lnxg_ref, lnxb_ref,
                    poolw_ref, pools_ref, wout_ref, lng_ref, lnb_ref, o_ref):
    y = y_ref[0] + y_ref[1]
    seg = seg_ref[...]
    mu = _seg_sum(y, seg) * (1.0 / HEAD)
    yc = y - mu
    var = _seg_sum(yc * yc, seg) * (1.0 / HEAD)
    yn = yc * lax.rsqrt(var + GN_EPS) * lnxg_ref[...] + lnxb_ref[...]
    a = (yn + bonus_ref[...]) * g_ref[...]
    b = _dot(pool_ref[...].astype(BF16), poolw_ref[...]) * pools_ref[...]
    out = _dot(a.astype(BF16), wout_ref[:D_A, :]) + _dot(b.astype(BF16), wout_ref[D_A:, :])
    m = mod_ref[0]
    o_ref[...] = _layer_norm(ALPHA * x_ref[...] + m[2:3] * out, lng_ref[...], lnb_ref[...])


def _rwkv_readout(y, bonus, g, pooled, x, mod, seg, lnx_g, lnx_b, pool_wbd, pool_scale, w_out, ln_g, ln_b,
                  *, n_ctx_tiles):
    t, d = x.shape
    tm = TILE_ROWS
    tok = pl.BlockSpec((tm, D_A), lambda i: (i, 0))
    return pl.pallas_call(
        _readout_kernel,
        grid=(t // tm,),
        in_specs=[
            pl.BlockSpec((2, tm, D_A), lambda i: (0, i, 0)), tok, tok,
            pl.BlockSpec((tm, D_B), lambda i: (i, 0)),
            pl.BlockSpec((tm, d), lambda i: (i, 0)),
            pl.BlockSpec((1, SUBLANES, d), lambda i: ((i >= n_ctx_tiles).astype(jnp.int32), 0, 0)),
            _const_spec(seg.shape), _const_spec(lnx_g.shape), _const_spec(lnx_b.shape),
            _const_spec(pool_wbd.shape), _const_spec(pool_scale.shape), _const_spec(w_out.shape),
            _const_spec(ln_g.shape), _const_spec(ln_b.shape),
        ],
        out_specs=pl.BlockSpec((tm, d), lambda i: (i, 0)),
        out_shape=jax.ShapeDtypeStruct((t, d), F32),
        compiler_params=_params("arbitrary"),
        name="rwkv_readout",
    )(y, bonus, g, pooled, x, mod, seg, lnx_g, lnx_b, pool_wbd, pool_scale, w_out, ln_g, ln_b)


def _ffn_kernel(x_ref, mod_ref, w1_ref, w3_ref, w2_ref, lng_ref, lnb_ref, o_ref):
    m = mod_ref[0]
    x = x_ref[...]
    h = (x * (1.0 + m[4:5]) + m[3:4]).astype(BF16)
    a = _dot(h, w1_ref[...])
    b = _dot(h, w3_ref[...])
    act = (a * _sigmoid(a) * b).astype(BF16)
    out = _dot(act, w2_ref[...])
    o_ref[...] = _layer_norm(ALPHA * x + m[5:6] * out, lng_ref[...], lnb_ref[...])


def _ffn(x, mod, w1, w3, w2, ln_g, ln_b, *, n_ctx_tiles, row_block_offset, n_tiles):
    d = x.shape[1]
    tm = TILE_ROWS
    single = pl.Buffered(1)
    return pl.pallas_call(
        _ffn_kernel,
        grid=(n_tiles,),
        in_specs=[
            pl.BlockSpec((tm, d), lambda i: (i + row_block_offset, 0)),
            pl.BlockSpec((1, SUBLANES, d),
                         lambda i: ((i + row_block_offset >= n_ctx_tiles).astype(jnp.int32), 0, 0)),
            pl.BlockSpec(w1.shape, lambda i: (0, 0), pipeline_mode=single),
            pl.BlockSpec(w3.shape, lambda i: (0, 0), pipeline_mode=single),
            pl.BlockSpec(w2.shape, lambda i: (0, 0), pipeline_mode=single),
            _const_spec(ln_g.shape), _const_spec(ln_b.shape),
        ],
        out_specs=pl.BlockSpec((tm, d), lambda i: (i, 0)),
        out_shape=jax.ShapeDtypeStruct((n_tiles * tm, d), F32),
        compiler_params=_params("arbitrary"),
        name="ffn",
    )(x, mod, w1, w3, w2, ln_g, ln_b)


def _attn_kernel(q_ref, k_ref, v_ref, bias_ref, o_ref, *, q_rows, n_rows, n_ctx, kh):
    rb = pl.program_id(1)
    scale = HEAD ** -0.5
    lane = lax.broadcasted_iota(jnp.int32, (1, LANES), 1)
    head0 = lane < HEAD
    kc = k_ref[0:n_ctx, :]
    vc = v_ref[0:n_ctx, :]
    n_loc = kh * GRID_W

    def body(j, carry):
        rr = rb * q_rows + j
        sr = jnp.clip(rr - kh // 2, 0, n_rows - kh)
        t_var = sr - rr + (NA_KH - 1)
        qrow = pl.ds(pl.multiple_of(j * GRID_W, GRID_W), GRID_W)
        q = q_ref[qrow, :]
        krows = pl.ds(pl.multiple_of(n_ctx + sr * GRID_W, GRID_W), n_loc)
        kb = k_ref[krows, :]
        vb = v_ref[krows, :]
        outs = []
        for h in range(2):
            mh = head0 if h == 0 else jnp.logical_not(head0)
            qh = jnp.where(mh, q, jnp.zeros_like(q))
            s_loc = _dot_nt(qh, kb) * scale + bias_ref[h, t_var]
            s_ctx = _dot_nt(qh, kc) * scale
            m = jnp.maximum(jnp.max(s_loc, axis=-1, keepdims=True), jnp.max(s_ctx, axis=-1, keepdims=True))
            p_loc = jnp.exp(s_loc - m)
            p_ctx = jnp.exp(s_ctx - m)
            denom = jnp.sum(p_loc, axis=-1, keepdims=True) + jnp.sum(p_ctx, axis=-1, keepdims=True)
            o = _dot(p_loc.astype(BF16), vb) + _dot(p_ctx.astype(BF16), vc)
            outs.append(o / denom)
        o_ref[qrow, :] = jnp.where(head0, outs[0], outs[1]).astype(o_ref.dtype)
        return carry

    lax.fori_loop(0, q_rows, body, 0)


def _neighbourhood_attention(qkv, bias_tab, *, n_ctx, n_lat):
    t_all = qkv.shape[0]
    d = D_MODEL
    n_rows = n_lat // GRID_W
    kh = min(NA_KH, n_rows)
    q_rows = TILE_ROWS // GRID_W
    n_pairs = d // LANES
    ctx_blocks = n_ctx // TILE_ROWS
    kern = functools.partial(_attn_kernel, q_rows=q_rows, n_rows=n_rows, n_ctx=n_ctx, kh=kh)
    n_var = bias_tab.shape[1]
    return pl.pallas_call(
        kern,
        grid=(n_pairs, n_lat // TILE_ROWS),
        in_specs=[
            pl.BlockSpec((TILE_ROWS, LANES), lambda p, b: (b + ctx_blocks, p)),
            pl.BlockSpec((t_all, LANES), lambda p, b: (0, n_pairs + p)),
            pl.BlockSpec((t_all, LANES), lambda p, b: (0, 2 * n_pairs + p)),
            pl.BlockSpec((2, n_var, GRID_W, kh * GRID_W), lambda p, b: (p, 0, 0, 0)),
        ],
        out_specs=pl.BlockSpec((TILE_ROWS, LANES), lambda p, b: (b, p)),
        out_shape=jax.ShapeDtypeStruct((n_lat, d), BF16),
        compiler_params=_params("arbitrary", "arbitrary"),
        name="neighbourhood_attention",
    )(qkv, qkv, qkv, bias_tab)


def _attention_bias_table(rpb, n_rows):
    kh = min(NA_KH, n_rows)
    cols = np.arange(GRID_W)
    col_start = np.clip(cols - NA_KW // 2, 0, GRID_W - NA_KW)
    in_win = (cols[None, :] >= col_start[:, None]) & (cols[None, :] < col_start[:, None] + NA_KW)
    col_off = cols[None, :] - cols[:, None] + (NA_KW - 1)
    onehot = (col_off[None] == np.arange(2 * NA_KW - 1)[:, None, None]) & in_win[None]
    dense = jnp.einsum('hrj,jck->hrck', rpb, jnp.asarray(onehot, F32), precision=HIGHEST)
    dense = jnp.where(jnp.asarray(in_win)[None, None], dense, NEG_BIAS)
    tab = jnp.stack([dense[:, t:t + kh] for t in range(NA_KH)], axis=1)
    tab = tab.transpose(0, 1, 3, 2, 4).reshape(rpb.shape[0], NA_KH, GRID_W, kh * GRID_W)
    return tab.astype(F32)


def _proj_ln_kernel(o_ref, x_ref, mod_ref, w_ref, lng_ref, lnb_ref, out_ref):
    m = mod_ref[0]
    y = _dot(o_ref[...], w_ref[...])
    out_ref[...] = _layer_norm(ALPHA * x_ref[...] + m[2:3] * y, lng_ref[...], lnb_ref[...])


def _proj_ln(o, x, mod, w, ln_g, ln_b, *, row_block_offset):
    t, d = o.shape
    tm = TILE_ROWS
    return pl.pallas_call(
        _proj_ln_kernel,
        grid=(t // tm,),
        in_specs=[
            pl.BlockSpec((tm, d), lambda i: (i, 0)),
            pl.BlockSpec((tm, d), lambda i: (i + row_block_offset, 0)),
            pl.BlockSpec((1, SUBLANES, d), lambda i: (1, 0, 0)),
            _const_spec(w.shape), _const_spec(ln_g.shape), _const_spec(ln_b.shape),
        ],
        out_specs=pl.BlockSpec((tm, d), lambda i: (i, 0)),
        out_shape=jax.ShapeDtypeStruct((t, d), F32),
        compiler_params=_params("arbitrary"),
        name="attn_proj_ln",
    )(o, x, mod, w, ln_g, ln_b)


def _block_diag2(w):
    z = jnp.zeros_like(w[0])
    return jnp.concatenate([jnp.concatenate([w[0], z], axis=1), jnp.concatenate([z, w[1]], axis=1)], axis=0)


def kernel(x, c, ctx, c_ctx, ada_w, ada_b, ln_g, ln_b, ffn_w1, ffn_w3, ffn_w2, ev_w_in, ev_shift_mu, ev_w0,
           ev_w2, ev_a0, ev_a2, ev_g2, ev_k_k, ev_k_a, ev_r_k, ev_lnx_g, ev_lnx_b, ev_pool_w, ev_pool_scale,
           ev_w_out, od_w_in, od_rpb, od_w_out):
    batch, n_lat, d = x.shape
    n_ctx = ctx.shape[1]
    assert batch == 1 and d == D_MODEL
    assert n_ctx % TILE_ROWS == 0 and n_lat % TILE_ROWS == 0 and n_lat % GRID_W == 0
    n_ctx_tiles = n_ctx // TILE_ROWS
    n_lat_tiles = n_lat // TILE_ROWS

    cs = jnp.zeros((SUBLANES, d), F32).at[0].set(c_ctx).at[1].set(c[0])
    mod_all = _ada_modulation(cs, ada_w, ada_b)
    mod_all = mod_all[:, :2].reshape(DEPTH, 2, 6, d)
    mod_all = jnp.pad(mod_all, ((0, 0), (0, 0), (0, SUBLANES - 6), (0, 0)))

    xa = jnp.concatenate([ctx[0], x[0]], axis=0)

    mod = mod_all[0]
    p = _mod_matmul(xa, mod, ev_w_in[0].astype(BF16), n_ctx_tiles=n_ctx_tiles, out_dtype=F32)
    seg = (jnp.arange(D_A)[:, None] // HEAD == jnp.arange(D_A)[None, :] // HEAD).astype(BF16)
    r, v, kn, ld, kd, bd, g, bonus, pooled = _rwkv_prepare(
        p, ev_shift_mu[0], ev_w0[0], _block_diag2(ev_w2[0]), ev_a0[0], _block_diag2(ev_a2[0]), ev_g2[0],
        ev_k_k[0].reshape(1, D_A), ev_k_a[0].reshape(1, D_A), ev_r_k[0].reshape(1, D_A), seg,
        n_ctx_tiles=n_ctx_tiles)
    y = _wkv_scan(r, v, kn, ld, kd, bd, n_ctx_blocks=n_ctx_tiles)
    pw = ev_pool_w[0]
    pool_wbd = jnp.zeros((D_B, D_B), F32)
    for gi in range(len(POOL_WINDOWS)):
        sl = slice(gi * POOL_GROUP, (gi + 1) * POOL_GROUP)
        pool_wbd = pool_wbd.at[sl, sl].set(pw[gi])
    xa = _rwkv_readout(y, bonus, g, pooled, xa, mod, seg, ev_lnx_g[0].reshape(1, D_A),
                       ev_lnx_b[0].reshape(1, D_A), pool_wbd.astype(BF16), ev_pool_scale[0].reshape(1, D_B),
                       ev_w_out[0].astype(BF16), ln_g[0, 0].reshape(1, d), ln_b[0, 0].reshape(1, d),
                       n_ctx_tiles=n_ctx_tiles)
    xa = _ffn(xa, mod, ffn_w1[0].astype(BF16), ffn_w3[0].astype(BF16), ffn_w2[0].astype(BF16),
              ln_g[0, 1].reshape(1, d), ln_b[0, 1].reshape(1, d), n_ctx_tiles=n_ctx_tiles,
              row_block_offset=0, n_tiles=n_ctx_tiles + n_lat_tiles)

    mod = mod_all[1]
    qkv = _mod_matmul(xa, mod, od_w_in[0].astype(BF16), n_ctx_tiles=n_ctx_tiles, out_dtype=BF16)
    bias_tab = _attention_bias_table(od_rpb[0], n_lat // GRID_W)
    o = _neighbourhood_attention(qkv, bias_tab, n_ctx=n_ctx, n_lat=n_lat)
    xl = _proj_ln(o, xa, mod, od_w_out[0].astype(BF16), ln_g[1, 0].reshape(1, d), ln_b[1, 0].reshape(1, d),
                  row_block_offset=n_ctx_tiles)
    xl = _ffn(xl, mod, ffn_w1[1].astype(BF16), ffn_w3[1].astype(BF16), ffn_w2[1].astype(BF16),
              ln_g[1, 1].reshape(1, d), ln_b[1, 1].reshape(1, d), n_ctx_tiles=0, row_block_offset=0,
              n_tiles=n_lat_tiles)
    return xl[None]
```

```python
import functools
import math

import jax
import jax.numpy as jnp
import numpy as np
from jax import lax
from jax.experimental import pallas as pl
from jax.experimental.pallas import tpu as pltpu

F32 = jnp.float32
BF16 = jnp.bfloat16
HIGHEST = lax.Precision.HIGHEST

D_MODEL = 1024
DEPTH = 2
GRID_W = 64
ALPHA = (2 * DEPTH) ** 0.25
LN_EPS = 1e-6
HEAD = 64
D_A = 3 * D_MODEL // 4
H_A = D_A // HEAD
LORA_W = 64
LORA_A = 64
LORA_G = 128
GN_EPS = 64e-5
D_B = D_MODEL - D_A
POOL_WINDOWS = (2, 4, 8, 16)
POOL_GROUP = D_B // len(POOL_WINDOWS)
C_RW = 3 * D_A + 2 * LORA_W + 2 * LORA_A + LORA_G
D_IN_EVEN = C_RW + D_B
H_C = D_MODEL // HEAD
NA_KH = 8
NA_KW = 16

LANES = 128
SUBLANES = 8
VMEM_LIMIT_BYTES = 56 * 1024 * 1024

TILE_ROWS = 256
CHUNK = 64
NEG_BIAS = -1e30


def _sigmoid(x):
    return 1.0 / (1.0 + jnp.exp(-x))


def _layer_norm(z, g, b):
    mu = jnp.mean(z, axis=-1, keepdims=True)
    zc = z - mu
    var = jnp.mean(zc * zc, axis=-1, keepdims=True)
    return zc * lax.rsqrt(var + LN_EPS) * g + b


def _dot(a, b):
    return jnp.dot(a, b, preferred_element_type=F32)


def _dot_hi(a, b):
    return jnp.dot(a, b, precision=HIGHEST, preferred_element_type=F32)


def _dot_nt_hi(a, b):
    return lax.dot_general(a, b, (((1,), (1,)), ((), ())), precision=HIGHEST, preferred_element_type=F32)


def _dot_nt(a, b):
    return lax.dot_general(a, b, (((1,), (1,)), ((), ())), preferred_element_type=F32)


def _seg_sum(x, seg):
    hi = x.astype(BF16)
    lo = (x - hi.astype(F32)).astype(BF16)
    return _dot(hi, seg) + _dot(lo, seg)


def _params(*sem):
    return pltpu.CompilerParams(dimension_semantics=sem, vmem_limit_bytes=VMEM_LIMIT_BYTES)


def _const_spec(shape):
    nd = len(shape)
    return pl.BlockSpec(shape, lambda *_: (0,) * nd)


def _ada_kernel(cs_ref, w_ref, b_ref, o_ref):
    s = cs_ref[...]
    s = s * _sigmoid(s)
    o_ref[0] = _dot_hi(s, w_ref[0]) + b_ref[0]


def _ada_modulation(cs, ada_w, ada_b):
    depth, d, n = ada_w.shape
    nb = 1536
    return pl.pallas_call(
        _ada_kernel,
        grid=(depth, n // nb),
        in_specs=[
            pl.BlockSpec((SUBLANES, d), lambda i, j: (0, 0)),
            pl.BlockSpec((1, d, nb), lambda i, j: (i, 0, j)),
            pl.BlockSpec((1, 1, nb), lambda i, j: (i, 0, j)),
        ],
        out_specs=pl.BlockSpec((1, SUBLANES, nb), lambda i, j: (i, 0, j)),
        out_shape=jax.ShapeDtypeStruct((depth, SUBLANES, n), F32),
        compiler_params=_params("arbitrary", "arbitrary"),
        name="ada_modulation",
    )(cs, ada_w, ada_b.reshape(depth, 1, n))


def _modmm_kernel(x_ref, mod_ref, w_ref, o_ref):
    m = mod_ref[0]
    h = x_ref[...] * (1.0 + m[1:2]) + m[0:1]
    o_ref[...] = _dot(h.astype(BF16), w_ref[...]).astype(o_ref.dtype)


def _mod_matmul(x, mod, w, *, n_ctx_tiles, out_dtype):
    t, d = x.shape
    n = w.shape[1]
    tm = TILE_ROWS
    return pl.pallas_call(
        _modmm_kernel,
        grid=(t // tm,),
        in_specs=[
            pl.BlockSpec((tm, d), lambda i: (i, 0)),
            pl.BlockSpec((1, SUBLANES, d), lambda i: ((i >= n_ctx_tiles).astype(jnp.int32), 0, 0)),
            _const_spec((d, n)),
        ],
        out_specs=pl.BlockSpec((tm, n), lambda i: (i, 0)),
        out_shape=jax.ShapeDtypeStruct((t, n), out_dtype),
        compiler_params=_params("arbitrary"),
        name="mod_matmul",
    )(x, mod, w)


def _prep_kernel(p_ref, pp_ref, pn_ref, mu_ref, w0_ref, w2_ref, a0_ref, a2_ref, g2_ref, kk_ref, ka_ref,
                 rk_ref, seg_ref,
                 r_o, v_o, kn_o, ld_o, kd_o, bd_o, g_o, bonus_o, pool_o, *, tm, n_ctx_tiles, n_tiles):
    i = pl.program_id(0)
    first = jnp.logical_or(i == 0, i == n_ctx_tiles)
    last = jnp.logical_or(i == n_ctx_tiles - 1, i == n_tiles - 1)
    row = lax.broadcasted_iota(jnp.int32, (tm, 1), 0)

    pm = p_ref[:, :C_RW]
    prev_row = jnp.where(first, 0.0, pp_ref[SUBLANES - 1:SUBLANES, :C_RW])
    next_row = jnp.where(last, 0.0, pn_ref[0:1, :C_RW])
    prev = jnp.where(row == 0, prev_row, pltpu.roll(pm, 1, 0))
    nxt = jnp.where(row == tm - 1, next_row, pltpu.roll(pm, tm - 1, 0))
    pm = pm + (prev - pm) * mu_ref[0:1] + (nxt - pm) * mu_ref[1:2]

    s1, s2, s3 = D_A, 2 * D_A, 3 * D_A
    s4 = s3 + 2 * LORA_W
    s5 = s4 + 2 * LORA_A
    r = pm[:, :s1]
    k = pm[:, s1:s2]
    v = pm[:, s2:s3]
    wd = pm[:, s3:s4]
    ad = pm[:, s4:s5]
    gd = pm[:, s5:]

    lw = _dot_hi(jnp.tanh(wd), w2_ref[...])
    la = _dot_hi(ad, a2_ref[...])
    g = _dot_hi(_sigmoid(gd), g2_ref[...])

    kr = k * kk_ref[...]
    ss = _seg_sum(kr * kr, seg_ref[...])
    kn = kr / jnp.maximum(jnp.sqrt(ss), 1e-12)

    decay_scale = math.exp(-0.5)
    kd_sum = jnp.zeros_like(k)
    for d in range(2):
        sl = slice(d * D_A, (d + 1) * D_A)
        ld_o[d] = -decay_scale * _sigmoid(w0_ref[d:d + 1] + lw[:, sl])
        a = _sigmoid(a0_ref[d:d + 1] + la[:, sl])
        kd = k * (1.0 + (a - 1.0) * ka_ref[...])
        kd_o[d] = kd
        bd_o[d] = kn * a
        kd_sum = kd_sum + kd
    coef = _seg_sum(r * kd_sum * rk_ref[...], seg_ref[...])

    r_o[...] = r
    v_o[...] = v
    kn_o[...] = kn
    g_o[...] = g
    bonus_o[...] = coef * v

    xp = p_ref[:, C_RW:]
    xprev = jnp.where(first, 0.0, pp_ref[:, C_RW:])
    xnext = jnp.where(last, 0.0, pn_ref[:, C_RW:])
    xe = jnp.concatenate([xprev, xp, xnext], axis=0)
    n_ext = tm + 2 * SUBLANES

    def at_offset(a, s):
        return pltpu.roll(a, (-s) % n_ext, 0)

    sums = {}
    acc = xe + at_offset(xe, -1)
    sums[2] = acc
    half = 1
    for win in (4, 8, 16):
        acc = at_offset(acc, -half) + at_offset(acc, half)
        sums[win] = acc
        half *= 2

    in_ctx = i < n_ctx_tiles
    tile0 = jnp.where(in_ctx, 0, n_ctx_tiles)
    seq_len = jnp.where(in_ctx, n_ctx_tiles * tm, (n_tiles - n_ctx_tiles) * tm)
    t_seq = (i - tile0) * tm + row
    grp = lax.broadcasted_iota(jnp.int32, (1, D_B), 1) // POOL_GROUP
    mean = jnp.zeros((tm, D_B), F32)
    for gi, win in enumerate(POOL_WINDOWS):
        cnt = jnp.minimum(t_seq + win // 2, seq_len) - jnp.maximum(t_seq - win // 2, 0)
        m = sums[win][SUBLANES:SUBLANES + tm] / cnt.astype(F32)
        mean = jnp.where(grp == gi, m, mean)
    pool_o[...] = mean - xp


def _rwkv_prepare(p, mu, w0, w2bd, a0, a2bd, g2, k_k, k_a, r_k, seg, *, n_ctx_tiles):
    t, n_in = p.shape
    tm = TILE_ROWS
    n_tiles = t // tm
    halo_per_tile = tm // SUBLANES
    n_halo = t // SUBLANES
    tok = jax.ShapeDtypeStruct((t, D_A), F32)
    tok2 = jax.ShapeDtypeStruct((2, t, D_A), F32)
    tok_spec = pl.BlockSpec((tm, D_A), lambda i: (i, 0))
    tok2_spec = pl.BlockSpec((2, tm, D_A), lambda i: (0, i, 0))
    kern = functools.partial(_prep_kernel, tm=tm, n_ctx_tiles=n_ctx_tiles, n_tiles=n_tiles)
    return pl.pallas_call(
        kern,
        grid=(n_tiles,),
        in_specs=[
            pl.BlockSpec((tm, n_in), lambda i: (i, 0)),
            pl.BlockSpec((SUBLANES, n_in), lambda i: (jnp.maximum(i * halo_per_tile - 1, 0), 0)),
            pl.BlockSpec((SUBLANES, n_in), lambda i: (jnp.minimum((i + 1) * halo_per_tile, n_halo - 1), 0)),
            _const_spec(mu.shape), _const_spec(w0.shape), _const_spec(w2bd.shape), _const_spec(a0.shape),
            _const_spec(a2bd.shape), _const_spec(g2.shape), _const_spec(k_k.shape), _const_spec(k_a.shape),
            _const_spec(r_k.shape), _const_spec(seg.shape),
        ],
        out_specs=[tok_spec, tok_spec, tok_spec, tok2_spec, tok2_spec, tok2_spec, tok_spec, tok_spec,
                   pl.BlockSpec((tm, D_B), lambda i: (i, 0))],
        out_shape=[tok, tok, tok, tok2, tok2, tok2, tok, tok, jax.ShapeDtypeStruct((t, D_B), F32)],
        compiler_params=_params("arbitrary"),
        name="rwkv_prepare",
    )(p, p, p, mu, w0, w2bd, a0, a2bd, g2, k_k, k_a, r_k, seg)


WKV_PAIRS_PER_STEP = 3


def _wkv_kernel(rf_ref, vf_ref, knf_ref, ldf_ref, kdf_ref, bdf_ref, rr_ref, vr_ref, knr_ref, ldr_ref, kdr_ref,
                bdr_ref, yf_ref, yr_ref, h_ref, *, chunks_per_block, pairs_per_step):
    @pl.when(pl.program_id(1) == 0)
    def _():
        h_ref[...] = jnp.zeros_like(h_ref)

    c = CHUNK
    ri = lax.broadcasted_iota(jnp.int32, (c, c), 0)
    ci = lax.broadcasted_iota(jnp.int32, (c, c), 1)
    r2 = lax.broadcasted_iota(jnp.int32, (LANES, LANES), 0)
    c2 = lax.broadcasted_iota(jnp.int32, (LANES, LANES), 1)
    same_head = (r2 < c) == (c2 < c)
    delta = (r2 & (c - 1)) - (c2 & (c - 1))
    eye2 = jnp.where(r2 == c2, 1.0, 0.0)
    head0 = lax.broadcasted_iota(jnp.int32, (1, LANES), 1) < HEAD

    def stack(x):
        return jnp.concatenate([jnp.where(head0, x, 0.0), jnp.where(head0, 0.0, x)], axis=0)

    masks = {}
    for rev in (False, True):
        sgn = -1 if rev else 1
        order = jnp.where(same_head, delta * sgn, -1)
        masks[rev] = (jnp.where((ri - ci) * sgn >= 0, 1.0, 0.0).astype(BF16), order > 0, order >= 0)

    chains = []
    for rev, refs, y_ref in ((False, (rf_ref, vf_ref, knf_ref, ldf_ref, kdf_ref, bdf_ref), yf_ref),
                             (True, (rr_ref, vr_ref, knr_ref, ldr_ref, kdr_ref, bdr_ref), yr_ref)):
        for q in range(pairs_per_step):
            chains.append((rev, refs, y_ref, slice(q * LANES, (q + 1) * LANES)))

    insts = []
    for ch, (rev, refs, y_ref, lanes) in enumerate(chains):
        for j in range(chunks_per_block):
            cj = chunks_per_block - 1 - j if rev else j
            insts.append(dict(ch=ch, j=j, rev=rev, refs=refs, lanes=lanes, rows=slice(cj * c, (cj + 1) * c)))

    for it in insts:
        ld = it["refs"][3][0, it["rows"], it["lanes"]]
        tri = masks[it["rev"]][0]
        ld_hi = ld.astype(BF16)
        rem = ld - ld_hi.astype(F32)
        ld_mid = rem.astype(BF16)
        ld_lo = (rem - ld_mid.astype(F32)).astype(BF16)
        it["cum"] = _dot(tri, ld_hi) + _dot(tri, ld_mid) + _dot(tri, ld_lo)

    for it in insts:
        r_ref, v_ref, kn_ref, ld_ref, kd_ref, bd_ref = it["refs"]
        rows, lanes = it["rows"], it["lanes"]
        _, strict, incl = masks[it["rev"]]
        ld = ld_ref[0, rows, lanes]
        kd = kd_ref[0, rows, lanes]
        bd = bd_ref[0, rows, lanes]
        cum = it.pop("cum")
        total = jnp.sum(ld, axis=0, keepdims=True)
        inv = jnp.exp(-cum)
        to_end = jnp.exp(total - cum)
        kq2 = stack(kn_ref[rows, lanes] * jnp.exp(cum - ld)).astype(BF16)
        rq2 = stack(r_ref[rows, lanes] * jnp.exp(cum)).astype(BF16)
        bkb = (bd * inv).astype(BF16)
        kkb = (kd * inv).astype(BF16)
        g = _dot_nt(jnp.concatenate([kq2, rq2], axis=0), jnp.concatenate([bkb, bkb, kkb, kkb], axis=0))
        it["a"] = jnp.where(strict, g[:LANES, :LANES], 0.0)
        it["bm"] = jnp.where(strict, g[:LANES, LANES:], 0.0).astype(BF16)
        it["pb"] = jnp.where(incl, g[LANES:, :LANES], 0.0).astype(BF16)
        it["pk"] = jnp.where(incl, g[LANES:, LANES:], 0.0).astype(BF16)
        it["kq2"], it["rq2"] = kq2, rq2
        it["v2"] = stack(v_ref[rows, lanes]).astype(BF16)
        it["ends_t"] = jnp.concatenate([stack(bd * to_end), stack(kd * to_end)], axis=0).T.astype(BF16)
        it["decay"] = jnp.exp(jnp.sum(ld.T, axis=1, keepdims=True))

    for it in insts:
        a = it.pop("a")
        it["t_inv"] = eye2 - a
        it["pw"] = a.astype(BF16)
    for _ in range(int(math.log2(c)) - 1):
        for it in insts:
            it["pw"] = _dot(it["pw"], it["pw"]).astype(BF16)
        for it in insts:
            it["t_inv"] = it["t_inv"] + _dot(it["t_inv"].astype(BF16), it["pw"])
    for it in insts:
        it["bmv"] = _dot(it.pop("bm"), it["v2"]).astype(BF16)
    for it in insts:
        wu = _dot(it.pop("t_inv").astype(BF16), jnp.concatenate([it.pop("kq2"), it.pop("bmv")], axis=1))
        it["w2"] = wu[:, :LANES].astype(BF16)
        it["ut2"] = wu[:, LANES:]
    for it in insts:
        it["pkv"] = _dot(it.pop("pk"), it["v2"])

    states = [h_ref[ch] for ch in range(len(chains))]
    by_key = {(it["ch"], it["j"]): it for it in insts}
    for j in range(chunks_per_block):
        m1s = []
        for ch in range(len(chains)):
            it = by_key[(ch, j)]
            m1s.append(_dot(jnp.concatenate([it["w2"], it["rq2"]], axis=0), states[ch].astype(BF16)))
        for ch, (rev, refs, y_ref, lanes) in enumerate(chains):
            it = by_key[(ch, j)]
            m1 = m1s[ch]
            u2b = (-m1[:LANES] - it["ut2"]).astype(BF16)
            y2 = m1[LANES:] + it["pkv"] + _dot(it["pb"], u2b)
            y_ref[it["rows"], lanes] = y2[:c] + y2[c:]
            upd = _dot(it["ends_t"], jnp.concatenate([u2b, it["v2"]], axis=0))
            states[ch] = states[ch] * it["decay"] + upd
    for ch in range(len(chains)):
        h_ref[ch] = states[ch]


def _wkv_scan(r, v, kn, ld, kd, bd, *, n_ctx_blocks):
    t, da = r.shape
    rows = TILE_ROWS
    n_blocks = t // rows
    pps = WKV_PAIRS_PER_STEP
    width = pps * LANES
    n_groups = da // width

    def rev_blk(s):
        return jnp.where(s < n_ctx_blocks, n_ctx_blocks - 1 - s, n_blocks - 1 - (s - n_ctx_blocks))

    fwd = pl.BlockSpec((rows, width), lambda g, s: (s, g))
    bwd = pl.BlockSpec((rows, width), lambda g, s: (rev_blk(s), g))
    fwd_d = pl.BlockSpec((1, rows, width), lambda g, s: (0, s, g))
    bwd_d = pl.BlockSpec((1, rows, width), lambda g, s: (1, rev_blk(s), g))
    kern = functools.partial(_wkv_kernel, chunks_per_block=rows // CHUNK, pairs_per_step=pps)
    out = jax.ShapeDtypeStruct((t, da), F32)
    return pl.pallas_call(
        kern,
        grid=(n_groups, n_blocks),
        in_specs=[fwd, fwd, fwd, fwd_d, fwd_d, fwd_d, bwd, bwd, bwd, bwd_d, bwd_d, bwd_d],
        out_specs=[fwd, bwd],
        out_shape=[out, out],
        scratch_shapes=[pltpu.VMEM((2 * pps, LANES, LANES), F32)],
        compiler_params=_params("arbitrary", "arbitrary"),
        name="wkv_scan",
    )(r, v, kn, ld, kd, bd, r, v, kn, ld, kd, bd)


def _readout_kernel(yf_ref, yr_ref, bonus_ref, g_ref, pool_ref, x_ref, mod_ref, seg_ref, lnxg_ref, lnxb_ref,
                    poolw_ref, pools_ref, wout_ref, lng_ref, lnb_ref, o_ref):
    y = yf_ref[...] + yr_ref[...]
    seg = seg_ref[...]
    mu = _seg_sum(y, seg) * (1.0 / HEAD)
    yc = y - mu
    var = _seg_sum(yc * yc, seg) * (1.0 / HEAD)
    yn = yc * lax.rsqrt(var + GN_EPS) * lnxg_ref[...] + lnxb_ref[...]
    a = (yn + bonus_ref[...]) * g_ref[...]
    b = _dot(pool_ref[...].astype(BF16), poolw_ref[...]) * pools_ref[...]
    out = _dot(a.astype(BF16), wout_ref[:D_A, :]) + _dot(b.astype(BF16), wout_ref[D_A:, :])
    m = mod_ref[0]
    o_ref[...] = _layer_norm(ALPHA * x_ref[...] + m[2:3] * out, lng_ref[...], lnb_ref[...])


def _rwkv_readout(y_fwd, y_rev, bonus, g, pooled, x, mod, seg, lnx_g, lnx_b, pool_wbd, pool_scale, w_out, ln_g, ln_b,
                  *, n_ctx_tiles):
    t, d = x.shape
    tm = TILE_ROWS
    tok = pl.BlockSpec((tm, D_A), lambda i: (i, 0))
    return pl.pallas_call(
        _readout_kernel,
        grid=(t // tm,),
        in_specs=[
            tok, tok, tok, tok,
            pl.BlockSpec((tm, D_B), lambda i: (i, 0)),
            pl.BlockSpec((tm, d), lambda i: (i, 0)),
            pl.BlockSpec((1, SUBLANES, d), lambda i: ((i >= n_ctx_tiles).astype(jnp.int32), 0, 0)),
            _const_spec(seg.shape), _const_spec(lnx_g.shape), _const_spec(lnx_b.shape),
            _const_spec(pool_wbd.shape), _const_spec(pool_scale.shape), _const_spec(w_out.shape),
            _const_spec(ln_g.shape), _const_spec(ln_b.shape),
        ],
        out_specs=pl.BlockSpec((tm, d), lambda i: (i, 0)),
        out_shape=jax.ShapeDtypeStruct((t, d), F32),
        compiler_params=_params("arbitrary"),
        name="rwkv_readout",
    )(y_fwd, y_rev, bonus, g, pooled, x, mod, seg, lnx_g, lnx_b, pool_wbd, pool_scale, w_out, ln_g, ln_b)


def _ffn_kernel(x_ref, mod_ref, w1_ref, w3_ref, w2_ref, lng_ref, lnb_ref, o_ref):
    m = mod_ref[0]
    x = x_ref[...]
    h = (x * (1.0 + m[4:5]) + m[3:4]).astype(BF16)
    a = _dot(h, w1_ref[...])
    b = _dot(h, w3_ref[...])
    act = (a * _sigmoid(a) * b).astype(BF16)
    out = _dot(act, w2_ref[...])
    o_ref[...] = _layer_norm(ALPHA * x + m[5:6] * out, lng_ref[...], lnb_ref[...])


def _ffn(x, mod, w1, w3, w2, ln_g, ln_b, *, n_ctx_tiles, row_block_offset, n_tiles):
    d = x.shape[1]
    tm = TILE_ROWS
    single = pl.Buffered(1)
    return pl.pallas_call(
        _ffn_kernel,
        grid=(n_tiles,),
        in_specs=[
            pl.BlockSpec((tm, d), lambda i: (i + row_block_offset, 0)),
            pl.BlockSpec((1, SUBLANES, d),
                         lambda i: ((i + row_block_offset >= n_ctx_tiles).astype(jnp.int32), 0, 0)),
            pl.BlockSpec(w1.shape, lambda i: (0, 0), pipeline_mode=single),
            pl.BlockSpec(w3.shape, lambda i: (0, 0), pipeline_mode=single),
            pl.BlockSpec(w2.shape, lambda i: (0, 0), pipeline_mode=single),
            _const_spec(ln_g.shape), _const_spec(ln_b.shape),
        ],
        out_specs=pl.BlockSpec((tm, d), lambda i: (i, 0)),
        out_shape=jax.ShapeDtypeStruct((n_tiles * tm, d), F32),
        compiler_params=_params("arbitrary"),
        name="ffn",
    )(x, mod, w1, w3, w2, ln_g, ln_b)


def _attn_kernel(q_ref, k_ref, v_ref, bias_ref, o_ref, *, q_rows, n_rows, n_ctx, kh):
    rb = pl.program_id(1)
    head0 = lax.broadcasted_iota(jnp.int32, (1, LANES), 1) < HEAD
    kc = k_ref[0:n_ctx, :]
    vc = v_ref[0:n_ctx, :]
    n_loc = kh * GRID_W
    q_all = q_ref[...] * jnp.asarray(HEAD ** -0.5, q_ref.dtype)
    zero = jnp.zeros((GRID_W, LANES), q_all.dtype)

    rows = []
    for j in range(q_rows):
        rr = rb * q_rows + j
        sr = jnp.clip(rr - kh // 2, 0, n_rows - kh)
        t_var = sr - rr + (NA_KH - 1)
        q = q_all[j * GRID_W:(j + 1) * GRID_W]
        q2 = jnp.concatenate([jnp.where(head0, q, zero), jnp.where(head0, zero, q)], axis=0)
        krows = pl.ds(pl.multiple_of(n_ctx + sr * GRID_W, GRID_W), n_loc)
        bias = jnp.concatenate([bias_ref[0, t_var], bias_ref[1, t_var]], axis=0)
        s_loc = _dot_nt(q2, k_ref[krows, :]) + bias
        s_ctx = _dot_nt(q2, kc)
        rows.append(dict(krows=krows, s_loc=s_loc, s_ctx=s_ctx))
    for it in rows:
        s_loc, s_ctx = it.pop("s_loc"), it.pop("s_ctx")
        m = jnp.maximum(jnp.max(s_loc, axis=-1, keepdims=True), jnp.max(s_ctx, axis=-1, keepdims=True))
        p_loc = jnp.exp(s_loc - m)
        p_ctx = jnp.exp(s_ctx - m)
        it["denom"] = jnp.sum(p_loc, axis=-1, keepdims=True) + jnp.sum(p_ctx, axis=-1, keepdims=True)
        it["p_loc"] = p_loc.astype(BF16)
        it["p_ctx"] = p_ctx.astype(BF16)
    for j, it in enumerate(rows):
        o2 = (_dot(it["p_loc"], v_ref[it["krows"], :]) + _dot(it["p_ctx"], vc)) / it["denom"]
        o_ref[j * GRID_W:(j + 1) * GRID_W, :] = jnp.where(head0, o2[:GRID_W], o2[GRID_W:]).astype(o_ref.dtype)


def _neighbourhood_attention(qkv, bias_tab, *, n_ctx, n_lat):
    t_all = qkv.shape[0]
    d = D_MODEL
    n_rows = n_lat // GRID_W
    kh = min(NA_KH, n_rows)
    q_rows = TILE_ROWS // GRID_W
    n_pairs = d // LANES
    ctx_blocks = n_ctx // TILE_ROWS
    kern = functools.partial(_attn_kernel, q_rows=q_rows, n_rows=n_rows, n_ctx=n_ctx, kh=kh)
    n_var = bias_tab.shape[1]
    return pl.pallas_call(
        kern,
        grid=(n_pairs, n_lat // TILE_ROWS),
        in_specs=[
            pl.BlockSpec((TILE_ROWS, LANES), lambda p, b: (b + ctx_blocks, p)),
            pl.BlockSpec((t_all, LANES), lambda p, b: (0, n_pairs + p)),
            pl.BlockSpec((t_all, LANES), lambda p, b: (0, 2 * n_pairs + p)),
            pl.BlockSpec((2, n_var, GRID_W, kh * GRID_W), lambda p, b: (p, 0, 0, 0)),
        ],
        out_specs=pl.BlockSpec((TILE_ROWS, LANES), lambda p, b: (b, p)),
        out_shape=jax.ShapeDtypeStruct((n_lat, d), BF16),
        compiler_params=_params("arbitrary", "arbitrary"),
        name="neighbourhood_attention",
    )(qkv, qkv, qkv, bias_tab)


def _attention_bias_table(rpb, n_rows):
    kh = min(NA_KH, n_rows)
    cols = np.arange(GRID_W)
    col_start = np.clip(cols - NA_KW // 2, 0, GRID_W - NA_KW)
    in_win = (cols[None, :] >= col_start[:, None]) & (cols[None, :] < col_start[:, None] + NA_KW)
    col_off = cols[None, :] - cols[:, None] + (NA_KW - 1)
    onehot = (col_off[None] == np.arange(2 * NA_KW - 1)[:, None, None]) & in_win[None]
    dense = jnp.einsum('hrj,jck->hrck', rpb, jnp.asarray(onehot, F32), precision=HIGHEST)
    dense = jnp.where(jnp.asarray(in_win)[None, None], dense, NEG_BIAS)
    tab = jnp.stack([dense[:, t:t + kh] for t in range(NA_KH)], axis=1)
    tab = tab.transpose(0, 1, 3, 2, 4).reshape(rpb.shape[0], NA_KH, GRID_W, kh * GRID_W)
    return tab.astype(F32)


def _proj_ln_kernel(o_ref, x_ref, mod_ref, w_ref, lng_ref, lnb_ref, out_ref):
    m = mod_ref[0]
    y = _dot(o_ref[...], w_ref[...])
    out_ref[...] = _layer_norm(ALPHA * x_ref[...] + m[2:3] * y, lng_ref[...], lnb_ref[...])


def _proj_ln(o, x, mod, w, ln_g, ln_b, *, row_block_offset):
    t, d = o.shape
    tm = TILE_ROWS
    return pl.pallas_call(
        _proj_ln_kernel,
        grid=(t // tm,),
        in_specs=[
            pl.BlockSpec((tm, d), lambda i: (i, 0)),
            pl.BlockSpec((tm, d), lambda i: (i + row_block_offset, 0)),
            pl.BlockSpec((1, SUBLANES, d), lambda i: (1, 0, 0)),
            _const_spec(w.shape), _const_spec(ln_g.shape), _const_spec(ln_b.shape),
        ],
        out_specs=pl.BlockSpec((tm, d), lambda i: (i, 0)),
        out_shape=jax.ShapeDtypeStruct((t, d), F32),
        compiler_params=_params("arbitrary"),
        name="attn_proj_ln",
    )(o, x, mod, w, ln_g, ln_b)


def _block_diag2(w):
    z = jnp.zeros_like(w[0])
    return jnp.concatenate([jnp.concatenate([w[0], z], axis=1), jnp.concatenate([z, w[1]], axis=1)], axis=0)


def kernel(x, c, ctx, c_ctx, ada_w, ada_b, ln_g, ln_b, ffn_w1, ffn_w3, ffn_w2, ev_w_in, ev_shift_mu, ev_w0,
           ev_w2, ev_a0, ev_a2, ev_g2, ev_k_k, ev_k_a, ev_r_k, ev_lnx_g, ev_lnx_b, ev_pool_w, ev_pool_scale,
           ev_w_out, od_w_in, od_rpb, od_w_out):
    batch, n_lat, d = x.shape
    n_ctx = ctx.shape[1]
    assert batch == 1 and d == D_MODEL
    assert n_ctx % TILE_ROWS == 0 and n_lat % TILE_ROWS == 0 and n_lat % GRID_W == 0
    n_ctx_tiles = n_ctx // TILE_ROWS
    n_lat_tiles = n_lat // TILE_ROWS

    cs = jnp.zeros((SUBLANES, d), F32).at[0].set(c_ctx).at[1].set(c[0])
    mod_all = _ada_modulation(cs, ada_w, ada_b)
    mod_all = mod_all[:, :2].reshape(DEPTH, 2, 6, d)
    mod_all = jnp.pad(mod_all, ((0, 0), (0, 0), (0, SUBLANES - 6), (0, 0)))

    xa = jnp.concatenate([ctx[0], x[0]], axis=0)

    mod = mod_all[0]
    p = _mod_matmul(xa, mod, ev_w_in[0].astype(BF16), n_ctx_tiles=n_ctx_tiles, out_dtype=F32)
    seg = (jnp.arange(D_A)[:, None] // HEAD == jnp.arange(D_A)[None, :] // HEAD).astype(BF16)
    r, v, kn, ld, kd, bd, g, bonus, pooled = _rwkv_prepare(
        p, ev_shift_mu[0], ev_w0[0], _block_diag2(ev_w2[0]), ev_a0[0], _block_diag2(ev_a2[0]), ev_g2[0],
        ev_k_k[0].reshape(1, D_A), ev_k_a[0].reshape(1, D_A), ev_r_k[0].reshape(1, D_A), seg,
        n_ctx_tiles=n_ctx_tiles)
    y_fwd, y_rev = _wkv_scan(r, v, kn, ld, kd, bd, n_ctx_blocks=n_ctx_tiles)
    pw = ev_pool_w[0]
    pool_wbd = jnp.zeros((D_B, D_B), F32)
    for gi in range(len(POOL_WINDOWS)):
        sl = slice(gi * POOL_GROUP, (gi + 1) * POOL_GROUP)
        pool_wbd = pool_wbd.at[sl, sl].set(pw[gi])
    xa = _rwkv_readout(y_fwd, y_rev, bonus, g, pooled, xa, mod, seg, ev_lnx_g[0].reshape(1, D_A),
                       ev_lnx_b[0].reshape(1, D_A), pool_wbd.astype(BF16), ev_pool_scale[0].reshape(1, D_B),
                       ev_w_out[0].astype(BF16), ln_g[0, 0].reshape(1, d), ln_b[0, 0].reshape(1, d),
                       n_ctx_tiles=n_ctx_tiles)
    xa = _ffn(xa, mod, ffn_w1[0].astype(BF16), ffn_w3[0].astype(BF16), ffn_w2[0].astype(BF16),
              ln_g[0, 1].reshape(1, d), ln_b[0, 1].reshape(1, d), n_ctx_tiles=n_ctx_tiles,
              row_block_offset=0, n_tiles=n_ctx_tiles + n_lat_tiles)

    mod = mod_all[1]
    qkv = _mod_matmul(xa, mod, od_w_in[0].astype(BF16), n_ctx_tiles=n_ctx_tiles, out_dtype=BF16)
    bias_tab = _attention_bias_table(od_rpb[0], n_lat // GRID_W)
    o = _neighbourhood_attention(qkv, bias_tab, n_ctx=n_ctx, n_lat=n_lat)
    xl = _proj_ln(o, xa, mod, od_w_out[0].astype(BF16), ln_g[1, 0].reshape(1, d), ln_b[1, 0].reshape(1, d),
                  row_block_offset=n_ctx_tiles)
    xl = _ffn(xl, mod, ffn_w1[1].astype(BF16), ffn_w3[1].astype(BF16), ffn_w2[1].astype(BF16),
              ln_g[1, 1].reshape(1, d), ln_b[1, 1].reshape(1, d), n_ctx_tiles=0, row_block_offset=0,
              n_tiles=n_lat_tiles)
    return xl[None]
```

```python
import functools
import math

import jax
import jax.numpy as jnp
import numpy as np
from jax import lax
from jax.experimental import pallas as pl
from jax.experimental.pallas import tpu as pltpu

F32 = jnp.float32
BF16 = jnp.bfloat16
HIGHEST = lax.Precision.HIGHEST

D_MODEL = 1024
DEPTH = 2
GRID_W = 64
ALPHA = (2 * DEPTH) ** 0.25
LN_EPS = 1e-6
HEAD = 64
D_A = 3 * D_MODEL // 4
H_A = D_A // HEAD
LORA_W = 64
LORA_A = 64
LORA_G = 128
GN_EPS = 64e-5
D_B = D_MODEL - D_A
POOL_WINDOWS = (2, 4, 8, 16)
POOL_GROUP = D_B // len(POOL_WINDOWS)
C_RW = 3 * D_A + 2 * LORA_W + 2 * LORA_A + LORA_G
D_IN_EVEN = C_RW + D_B
H_C = D_MODEL // HEAD
NA_KH = 8
NA_KW = 16

LANES = 128
SUBLANES = 8
VMEM_LIMIT_BYTES = 56 * 1024 * 1024

TILE_ROWS = 256
LAT_TILE_ROWS = 512
CHUNK = 64
NEG_BIAS = -1e30


def _sigmoid(x):
    return 1.0 / (1.0 + jnp.exp(-x))


def _layer_norm(z, g, b):
    mu = jnp.mean(z, axis=-1, keepdims=True)
    zc = z - mu
    var = jnp.mean(zc * zc, axis=-1, keepdims=True)
    return zc * lax.rsqrt(var + LN_EPS) * g + b


def _dot(a, b):
    return jnp.dot(a, b, preferred_element_type=F32)


def _dot_hi(a, b):
    return jnp.dot(a, b, precision=HIGHEST, preferred_element_type=F32)


def _dot_nt(a, b):
    return lax.dot_general(a, b, (((1,), (1,)), ((), ())), preferred_element_type=F32)


def _seg_sum(x, seg):
    hi = x.astype(BF16)
    lo = (x - hi.astype(F32)).astype(BF16)
    return _dot(hi, seg) + _dot(lo, seg)


def _params(*sem):
    return pltpu.CompilerParams(dimension_semantics=sem, vmem_limit_bytes=VMEM_LIMIT_BYTES)


def _const_spec(shape):
    nd = len(shape)
    return pl.BlockSpec(shape, lambda *_: (0,) * nd)


def _mod_spec(d, n_lat_tiles):
    return pl.BlockSpec((1, SUBLANES, d), lambda i: ((i < n_lat_tiles).astype(jnp.int32), 0, 0))


def _ada_kernel(cs_ref, w_ref, b_ref, o_ref):
    s = cs_ref[...]
    s = s * _sigmoid(s)
    o_ref[0] = _dot_hi(s, w_ref[0]) + b_ref[0]


def _ada_modulation(cs, ada_w, ada_b):
    depth, d, n = ada_w.shape
    nb = 1536
    return pl.pallas_call(
        _ada_kernel,
        grid=(depth, n // nb),
        in_specs=[
            pl.BlockSpec((SUBLANES, d), lambda i, j: (0, 0)),
            pl.BlockSpec((1, d, nb), lambda i, j: (i, 0, j)),
            pl.BlockSpec((1, 1, nb), lambda i, j: (i, 0, j)),
        ],
        out_specs=pl.BlockSpec((1, SUBLANES, nb), lambda i, j: (i, 0, j)),
        out_shape=jax.ShapeDtypeStruct((depth, SUBLANES, n), F32),
        compiler_params=_params("arbitrary", "arbitrary"),
        name="ada_modulation",
    )(cs, ada_w, ada_b.reshape(depth, 1, n))


def _modmm_kernel(x_ref, mod_ref, w_ref, o_ref):
    m = mod_ref[0]
    h = x_ref[...] * (1.0 + m[1:2]) + m[0:1]
    o_ref[...] = _dot(h.astype(BF16), w_ref[...]).astype(o_ref.dtype)


def _mod_matmul(x, mod, w, *, n_lat_tiles, out_dtype):
    t, d = x.shape
    n = w.shape[1]
    tm = TILE_ROWS
    return pl.pallas_call(
        _modmm_kernel,
        grid=(t // tm,),
        in_specs=[
            pl.BlockSpec((tm, d), lambda i: (i, 0)),
            _mod_spec(d, n_lat_tiles),
            _const_spec((d, n)),
        ],
        out_specs=pl.BlockSpec((tm, n), lambda i: (i, 0)),
        out_shape=jax.ShapeDtypeStruct((t, n), out_dtype),
        compiler_params=_params("arbitrary"),
        name="mod_matmul",
    )(x, mod, w)


def _prep_kernel(p_ref, pp_ref, pn_ref, mu_ref, w0_ref, w2_ref, a0_ref, a2_ref, g2_ref, kk_ref, ka_ref,
                 rk_ref, seg_ref,
                 r_o, v_o, kn_o, ld_o, kd_o, bd_o, g_o, bonus_o, pool_o, *, tm, n_lat_tiles, n_tiles):
    i = pl.program_id(0)
    first = jnp.logical_or(i == 0, i == n_lat_tiles)
    last = jnp.logical_or(i == n_lat_tiles - 1, i == n_tiles - 1)
    row = lax.broadcasted_iota(jnp.int32, (tm, 1), 0)

    pm = p_ref[:, :C_RW]
    prev_row = jnp.where(first, 0.0, pp_ref[SUBLANES - 1:SUBLANES, :C_RW])
    next_row = jnp.where(last, 0.0, pn_ref[0:1, :C_RW])
    prev = jnp.where(row == 0, prev_row, pltpu.roll(pm, 1, 0))
    nxt = jnp.where(row == tm - 1, next_row, pltpu.roll(pm, tm - 1, 0))
    pm = pm + (prev - pm) * mu_ref[0:1] + (nxt - pm) * mu_ref[1:2]

    s1, s2, s3 = D_A, 2 * D_A, 3 * D_A
    s4 = s3 + 2 * LORA_W
    s5 = s4 + 2 * LORA_A
    r = pm[:, :s1]
    k = pm[:, s1:s2]
    v = pm[:, s2:s3]
    wd = pm[:, s3:s4]
    ad = pm[:, s4:s5]
    gd = pm[:, s5:]

    lw = _dot(jnp.tanh(wd).astype(BF16), w2_ref[...])
    la = _dot(ad.astype(BF16), a2_ref[...])
    g = _dot(_sigmoid(gd).astype(BF16), g2_ref[...])

    kr = k * kk_ref[...]
    ss = _seg_sum(kr * kr, seg_ref[...])
    kn = kr / jnp.maximum(jnp.sqrt(ss), 1e-12)

    decay_scale = math.exp(-0.5)
    kd_sum = jnp.zeros_like(k)
    for d in range(2):
        sl = slice(d * D_A, (d + 1) * D_A)
        ld_o[d] = -decay_scale * _sigmoid(w0_ref[d:d + 1] + lw[:, sl])
        a = _sigmoid(a0_ref[d:d + 1] + la[:, sl])
        kd = k * (1.0 + (a - 1.0) * ka_ref[...])
        kd_o[d] = kd.astype(kd_o.dtype)
        bd_o[d] = (kn * a).astype(bd_o.dtype)
        kd_sum = kd_sum + kd
    coef = _seg_sum(r * kd_sum * rk_ref[...], seg_ref[...])

    r_o[...] = r.astype(r_o.dtype)
    v_o[...] = v.astype(v_o.dtype)
    kn_o[...] = kn.astype(kn_o.dtype)
    g_o[...] = g.astype(g_o.dtype)
    bonus_o[...] = (coef * v).astype(bonus_o.dtype)

    xp = p_ref[:, C_RW:]
    xprev = jnp.where(first, 0.0, pp_ref[:, C_RW:])
    xnext = jnp.where(last, 0.0, pn_ref[:, C_RW:])
    xe = jnp.concatenate([xprev, xp, xnext], axis=0)
    n_ext = tm + 2 * SUBLANES

    def at_offset(a, s):
        return pltpu.roll(a, (-s) % n_ext, 0)

    sums = {}
    acc = xe + at_offset(xe, -1)
    sums[2] = acc
    half = 1
    for win in (4, 8, 16):
        acc = at_offset(acc, -half) + at_offset(acc, half)
        sums[win] = acc
        half *= 2

    in_ctx = i >= n_lat_tiles
    tile0 = jnp.where(in_ctx, n_lat_tiles, 0)
    seq_len = jnp.where(in_ctx, (n_tiles - n_lat_tiles) * tm, n_lat_tiles * tm)
    t_seq = (i - tile0) * tm + row
    grp = lax.broadcasted_iota(jnp.int32, (1, D_B), 1) // POOL_GROUP
    mean = jnp.zeros((tm, D_B), F32)
    for gi, win in enumerate(POOL_WINDOWS):
        cnt = jnp.minimum(t_seq + win // 2, seq_len) - jnp.maximum(t_seq - win // 2, 0)
        m = sums[win][SUBLANES:SUBLANES + tm] / cnt.astype(F32)
        mean = jnp.where(grp == gi, m, mean)
    pool_o[...] = mean - xp


def _rwkv_prepare(p, mu, w0, w2bd, a0, a2bd, g2, k_k, k_a, r_k, seg, *, n_lat_tiles):
    t, n_in = p.shape
    tm = TILE_ROWS
    n_tiles = t // tm
    halo_per_tile = tm // SUBLANES
    n_halo = t // SUBLANES
    tok = jax.ShapeDtypeStruct((t, D_A), BF16)
    tok2 = jax.ShapeDtypeStruct((2, t, D_A), BF16)
    ld2 = jax.ShapeDtypeStruct((2, t, D_A), F32)
    tok_spec = pl.BlockSpec((tm, D_A), lambda i: (i, 0))
    tok2_spec = pl.BlockSpec((2, tm, D_A), lambda i: (0, i, 0))
    kern = functools.partial(_prep_kernel, tm=tm, n_lat_tiles=n_lat_tiles, n_tiles=n_tiles)
    return pl.pallas_call(
        kern,
        grid=(n_tiles,),
        in_specs=[
            pl.BlockSpec((tm, n_in), lambda i: (i, 0)),
            pl.BlockSpec((SUBLANES, n_in), lambda i: (jnp.maximum(i * halo_per_tile - 1, 0), 0)),
            pl.BlockSpec((SUBLANES, n_in), lambda i: (jnp.minimum((i + 1) * halo_per_tile, n_halo - 1), 0)),
            _const_spec(mu.shape), _const_spec(w0.shape), _const_spec(w2bd.shape), _const_spec(a0.shape),
            _const_spec(a2bd.shape), _const_spec(g2.shape), _const_spec(k_k.shape), _const_spec(k_a.shape),
            _const_spec(r_k.shape), _const_spec(seg.shape),
        ],
        out_specs=[tok_spec, tok_spec, tok_spec, tok2_spec, tok2_spec, tok2_spec, tok_spec, tok_spec,
                   pl.BlockSpec((tm, D_B), lambda i: (i, 0))],
        out_shape=[tok, tok, tok, ld2, tok2, tok2, tok, tok, jax.ShapeDtypeStruct((t, D_B), F32)],
        compiler_params=_params("arbitrary"),
        name="rwkv_prepare",
    )(p, p, p, mu, w0, w2bd, a0, a2bd, g2, k_k, k_a, r_k, seg)


WKV_PAIRS_PER_STEP = 3


def _wkv_kernel(rf_ref, vf_ref, knf_ref, ldf_ref, kdf_ref, bdf_ref, rr_ref, vr_ref, knr_ref, ldr_ref, kdr_ref,
                bdr_ref, yf_ref, yr_ref, h_ref, *, chunks_per_block, pairs_per_step):
    @pl.when(pl.program_id(1) == 0)
    def _():
        h_ref[...] = jnp.zeros_like(h_ref)

    c = CHUNK
    ri = lax.broadcasted_iota(jnp.int32, (c, c), 0)
    ci = lax.broadcasted_iota(jnp.int32, (c, c), 1)
    rp = lax.broadcasted_iota(jnp.int32, (c, LANES), 0)
    cp = lax.broadcasted_iota(jnp.int32, (c, LANES), 1) & (c - 1)
    eye_p = jnp.where(rp == cp, 1.0, 0.0)
    r2 = lax.broadcasted_iota(jnp.int32, (LANES, LANES), 0)
    c2 = lax.broadcasted_iota(jnp.int32, (LANES, LANES), 1)
    same_head = (r2 < HEAD) == (c2 < HEAD)
    head0 = lax.broadcasted_iota(jnp.int32, (1, LANES), 1) < HEAD

    def bdiag(x):
        zero = jnp.zeros_like(x)
        return jnp.concatenate([jnp.where(head0, x, zero), jnp.where(head0, zero, x)], axis=0)

    masks = {}
    for rev in (False, True):
        sgn = -1 if rev else 1
        order = (rp - cp) * sgn
        masks[rev] = (jnp.where((ri - ci) * sgn >= 0, 1.0, 0.0).astype(BF16), order > 0, order >= 0)

    chains = []
    for rev, refs, y_ref in ((False, (rf_ref, vf_ref, knf_ref, ldf_ref, kdf_ref, bdf_ref), yf_ref),
                             (True, (rr_ref, vr_ref, knr_ref, ldr_ref, kdr_ref, bdr_ref), yr_ref)):
        for q in range(pairs_per_step):
            chains.append((rev, refs, y_ref, slice(q * LANES, (q + 1) * LANES)))

    insts = []
    for ch, (rev, refs, y_ref, lanes) in enumerate(chains):
        for j in range(chunks_per_block):
            cj = chunks_per_block - 1 - j if rev else j
            insts.append(dict(ch=ch, j=j, rev=rev, refs=refs, lanes=lanes, rows=slice(cj * c, (cj + 1) * c)))

    for it in insts:
        ld = it["refs"][3][0, it["rows"], it["lanes"]]
        tri = masks[it["rev"]][0]
        ld_hi = ld.astype(BF16)
        rem = ld - ld_hi.astype(F32)
        ld_mid = rem.astype(BF16)
        ld_lo = (rem - ld_mid.astype(F32)).astype(BF16)
        part = _dot(tri, jnp.concatenate([ld_hi, ld_mid], axis=1))
        it["cum"] = part[:, :LANES] + part[:, LANES:] + _dot(tri, ld_lo)

    for it in insts:
        r_ref, v_ref, kn_ref, ld_ref, kd_ref, bd_ref = it["refs"]
        rows, lanes = it["rows"], it["lanes"]
        _, strict, incl = masks[it["rev"]]
        ld = ld_ref[0, rows, lanes]
        kd = kd_ref[0, rows, lanes].astype(F32)
        bd = bd_ref[0, rows, lanes].astype(F32)
        cum = it.pop("cum")
        total = jnp.sum(ld, axis=0, keepdims=True)
        inv = jnp.exp(-cum)
        to_end = jnp.exp(total - cum)
        kq = (kn_ref[rows, lanes].astype(F32) * jnp.exp(cum - ld)).astype(BF16)
        rq = (r_ref[rows, lanes].astype(F32) * jnp.exp(cum)).astype(BF16)
        g = _dot_nt(jnp.concatenate([kq, rq], axis=0),
                    jnp.concatenate([bdiag((bd * inv).astype(BF16)), bdiag((kd * inv).astype(BF16))], axis=0))
        it["q"] = jnp.where(strict, -g[:c, :LANES], 0.0)
        it["bm"] = jnp.where(strict, g[:c, LANES:], 0.0).astype(BF16)
        it["pb"] = jnp.where(incl, g[c:, :LANES], 0.0).astype(BF16)
        it["pk"] = jnp.where(incl, g[c:, LANES:], 0.0).astype(BF16)
        it["kq_bd"], it["rq"] = bdiag(kq), rq
        it["v"] = v_ref[rows, lanes]
        it["v_bd"] = bdiag(it["v"])
        it["ends_t"] = jnp.concatenate([bd * to_end, kd * to_end], axis=0).T.astype(BF16)
        it["decay"] = jnp.exp(jnp.sum(ld.T, axis=1, keepdims=True))

    for it in insts:
        q = it.pop("q")
        it["t_inv"] = eye_p + q
        qb = q.astype(BF16)
        it["pw"] = _dot(qb, bdiag(qb)).astype(BF16)
    n_levels = int(math.log2(c)) - 1
    for level in range(n_levels):
        for it in insts:
            pw = it["pw"]
            t_bd = bdiag(it["t_inv"].astype(BF16))
            if level < n_levels - 1:
                res = _dot(pw, jnp.concatenate([bdiag(pw), t_bd], axis=1))
                it["pw"] = res[:, :LANES].astype(BF16)
                it["t_inv"] = it["t_inv"] + res[:, LANES:]
            else:
                it["t_inv"] = it["t_inv"] + _dot(pw, t_bd)
    for it in insts:
        it["bmv"] = _dot(it.pop("bm"), it["v_bd"]).astype(BF16)
    for it in insts:
        wu = _dot(it.pop("t_inv").astype(BF16), jnp.concatenate([it.pop("kq_bd"), bdiag(it.pop("bmv"))], axis=1))
        it["w"] = wu[:, :LANES].astype(BF16)
        it["ut"] = wu[:, LANES:]
    for it in insts:
        it["pkv"] = _dot(it.pop("pk"), it["v_bd"])

    states = [h_ref[ch] for ch in range(len(chains))]
    by_key = {(it["ch"], it["j"]): it for it in insts}
    for j in range(chunks_per_block):
        m1s = []
        for ch in range(len(chains)):
            it = by_key[(ch, j)]
            m1s.append(_dot(jnp.concatenate([it["w"], it["rq"]], axis=0), states[ch].astype(BF16)))
        for ch, (rev, refs, y_ref, lanes) in enumerate(chains):
            it = by_key[(ch, j)]
            m1 = m1s[ch]
            ub = (-m1[:c] - it["ut"]).astype(BF16)
            y_ref[it["rows"], lanes] = m1[c:] + it["pkv"] + _dot(it["pb"], bdiag(ub))
            upd = _dot(it["ends_t"], jnp.concatenate([ub, it["v"]], axis=0))
            states[ch] = states[ch] * it["decay"] + jnp.where(same_head, upd, 0.0)
    for ch in range(len(chains)):
        h_ref[ch] = states[ch]


def _wkv_scan(r, v, kn, ld, kd, bd, *, n_lat_blocks):
    t, da = r.shape
    rows = TILE_ROWS
    n_blocks = t // rows
    pps = WKV_PAIRS_PER_STEP
    width = pps * LANES
    n_groups = da // width

    n_ctx_blocks = n_blocks - n_lat_blocks

    def fwd_blk(s):
        return jnp.where(s < n_ctx_blocks, n_lat_blocks + s, s - n_ctx_blocks)

    def rev_blk(s):
        return jnp.where(s < n_ctx_blocks, n_blocks - 1 - s, n_lat_blocks - 1 - (s - n_ctx_blocks))

    fwd = pl.BlockSpec((rows, width), lambda g, s: (fwd_blk(s), g))
    bwd = pl.BlockSpec((rows, width), lambda g, s: (rev_blk(s), g))
    fwd_d = pl.BlockSpec((1, rows, width), lambda g, s: (0, fwd_blk(s), g))
    bwd_d = pl.BlockSpec((1, rows, width), lambda g, s: (1, rev_blk(s), g))
    kern = functools.partial(_wkv_kernel, chunks_per_block=rows // CHUNK, pairs_per_step=pps)
    out = jax.ShapeDtypeStruct((t, da), F32)
    return pl.pallas_call(
        kern,
        grid=(n_groups, n_blocks),
        in_specs=[fwd, fwd, fwd, fwd_d, fwd_d, fwd_d, bwd, bwd, bwd, bwd_d, bwd_d, bwd_d],
        out_specs=[fwd, bwd],
        out_shape=[out, out],
        scratch_shapes=[pltpu.VMEM((2 * pps, LANES, LANES), F32)],
        compiler_params=_params("arbitrary", "arbitrary"),
        name="wkv_scan",
    )(r, v, kn, ld, kd, bd, r, v, kn, ld, kd, bd)


def _readout_kernel(yf_ref, yr_ref, bonus_ref, g_ref, pool_ref, x_ref, mod_ref, seg_ref, lnxg_ref, lnxb_ref,
                    poolw_ref, pools_ref, wout_ref, lng_ref, lnb_ref, o_ref):
    y = yf_ref[...] + yr_ref[...]
    seg = seg_ref[...]
    mu = _seg_sum(y, seg) * (1.0 / HEAD)
    yc = y - mu
    var = _seg_sum(yc * yc, seg) * (1.0 / HEAD)
    yn = yc * lax.rsqrt(var + GN_EPS) * lnxg_ref[...] + lnxb_ref[...]
    a = (yn + bonus_ref[...].astype(F32)) * g_ref[...].astype(F32)
    b = _dot(pool_ref[...].astype(BF16), poolw_ref[...]) * pools_ref[...]
    out = _dot(a.astype(BF16), wout_ref[:D_A, :]) + _dot(b.astype(BF16), wout_ref[D_A:, :])
    m = mod_ref[0]
    o_ref[...] = _layer_norm(ALPHA * x_ref[...] + m[2:3] * out, lng_ref[...], lnb_ref[...])


def _rwkv_readout(y_fwd, y_rev, bonus, g, pooled, x, mod, seg, lnx_g, lnx_b, pool_wbd, pool_scale, w_out, ln_g, ln_b,
                  *, n_lat_tiles):
    t, d = x.shape
    tm = TILE_ROWS
    tok = pl.BlockSpec((tm, D_A), lambda i: (i, 0))
    return pl.pallas_call(
        _readout_kernel,
        grid=(t // tm,),
        in_specs=[
            tok, tok, tok, tok,
            pl.BlockSpec((tm, D_B), lambda i: (i, 0)),
            pl.BlockSpec((tm, d), lambda i: (i, 0)),
            _mod_spec(d, n_lat_tiles),
            _const_spec(seg.shape), _const_spec(lnx_g.shape), _const_spec(lnx_b.shape),
            _const_spec(pool_wbd.shape), _const_spec(pool_scale.shape), _const_spec(w_out.shape),
            _const_spec(ln_g.shape), _const_spec(ln_b.shape),
        ],
        out_specs=pl.BlockSpec((tm, d), lambda i: (i, 0)),
        out_shape=jax.ShapeDtypeStruct((t, d), F32),
        compiler_params=_params("arbitrary"),
        name="rwkv_readout",
    )(y_fwd, y_rev, bonus, g, pooled, x, mod, seg, lnx_g, lnx_b, pool_wbd, pool_scale, w_out, ln_g, ln_b)


def _ffn_kernel(x_ref, mod_ref, w1_ref, w3_ref, w2_ref, lng_ref, lnb_ref, o_ref):
    m = mod_ref[0]
    x = x_ref[...]
    h = (x * (1.0 + m[4:5]) + m[3:4]).astype(BF16)
    a = _dot(h, w1_ref[...])
    b = _dot(h, w3_ref[...])
    act = (a * _sigmoid(a) * b).astype(BF16)
    out = _dot(act, w2_ref[...])
    o_ref[...] = _layer_norm(ALPHA * x + m[5:6] * out, lng_ref[...], lnb_ref[...])


def _ffn(x, mod, w1, w3, w2, ln_g, ln_b, *, tm, n_tiles, n_lat_tiles):
    d = x.shape[1]
    single = pl.Buffered(1)
    return pl.pallas_call(
        _ffn_kernel,
        grid=(n_tiles,),
        in_specs=[
            pl.BlockSpec((tm, d), lambda i: (i, 0)),
            _mod_spec(d, n_lat_tiles),
            pl.BlockSpec(w1.shape, lambda i: (0, 0), pipeline_mode=single),
            pl.BlockSpec(w3.shape, lambda i: (0, 0), pipeline_mode=single),
            pl.BlockSpec(w2.shape, lambda i: (0, 0), pipeline_mode=single),
            _const_spec(ln_g.shape), _const_spec(ln_b.shape),
        ],
        out_specs=pl.BlockSpec((tm, d), lambda i: (i, 0)),
        out_shape=jax.ShapeDtypeStruct((n_tiles * tm, d), F32),
        compiler_params=_params("arbitrary"),
        name="ffn",
    )(x, mod, w1, w3, w2, ln_g, ln_b)


ATTN_ROWS_PER_STEP = 8


def _attn_kernel(q_ref, k_ref, v_ref, bias_ref, o_ref, *, q_rows, n_rows, n_lat, n_ctx, kh):
    rb = pl.program_id(1)
    head0 = lax.broadcasted_iota(jnp.int32, (1, LANES), 1) < HEAD
    kc = k_ref[n_lat:n_lat + n_ctx, :]
    vc = v_ref[n_lat:n_lat + n_ctx, :]
    n_loc = kh * GRID_W
    q_all = q_ref[...] * jnp.asarray(HEAD ** -0.5, q_ref.dtype)
    zero = jnp.zeros((GRID_W, LANES), q_all.dtype)

    rows = []
    for j in range(q_rows):
        rr = rb * q_rows + j
        sr = jnp.clip(rr - kh // 2, 0, n_rows - kh)
        t_var = sr - rr + (NA_KH - 1)
        q = q_all[j * GRID_W:(j + 1) * GRID_W]
        q2 = jnp.concatenate([jnp.where(head0, q, zero), jnp.where(head0, zero, q)], axis=0)
        krows = pl.ds(pl.multiple_of(sr * GRID_W, GRID_W), n_loc)
        bias = jnp.concatenate([bias_ref[0, t_var], bias_ref[1, t_var]], axis=0)
        s_loc = _dot_nt(q2, k_ref[krows, :]) + bias
        s_ctx = _dot_nt(q2, kc)
        rows.append(dict(krows=krows, s_loc=s_loc, s_ctx=s_ctx))
    for it in rows:
        s_loc, s_ctx = it.pop("s_loc"), it.pop("s_ctx")
        m = jnp.maximum(jnp.max(s_loc, axis=-1, keepdims=True), jnp.max(s_ctx, axis=-1, keepdims=True))
        p_loc = jnp.exp(s_loc - m)
        p_ctx = jnp.exp(s_ctx - m)
        it["denom"] = jnp.sum(p_loc, axis=-1, keepdims=True) + jnp.sum(p_ctx, axis=-1, keepdims=True)
        it["p_loc"] = p_loc.astype(BF16)
        it["p_ctx"] = p_ctx.astype(BF16)
    for j, it in enumerate(rows):
        o2 = (_dot(it["p_loc"], v_ref[it["krows"], :]) + _dot(it["p_ctx"], vc)) / it["denom"]
        o_ref[j * GRID_W:(j + 1) * GRID_W, :] = jnp.where(head0, o2[:GRID_W], o2[GRID_W:]).astype(o_ref.dtype)


def _neighbourhood_attention(qkv, bias_tab, *, n_ctx, n_lat):
    t_all = qkv.shape[0]
    d = D_MODEL
    n_rows = n_lat // GRID_W
    kh = min(NA_KH, n_rows)
    q_rows = ATTN_ROWS_PER_STEP
    q_tile = q_rows * GRID_W
    n_pairs = d // LANES
    kern = functools.partial(_attn_kernel, q_rows=q_rows, n_rows=n_rows, n_lat=n_lat, n_ctx=n_ctx, kh=kh)
    n_var = bias_tab.shape[1]
    return pl.pallas_call(
        kern,
        grid=(n_pairs, n_lat // q_tile),
        in_specs=[
            pl.BlockSpec((q_tile, LANES), lambda p, b: (b, p)),
            pl.BlockSpec((t_all, LANES), lambda p, b: (0, n_pairs + p)),
            pl.BlockSpec((t_all, LANES), lambda p, b: (0, 2 * n_pairs + p)),
            pl.BlockSpec((2, n_var, GRID_W, kh * GRID_W), lambda p, b: (p, 0, 0, 0)),
        ],
        out_specs=pl.BlockSpec((q_tile, LANES), lambda p, b: (b, p)),
        out_shape=jax.ShapeDtypeStruct((n_lat, d), BF16),
        compiler_params=_params("arbitrary", "arbitrary"),
        name="neighbourhood_attention",
    )(qkv, qkv, qkv, bias_tab)


def _attention_bias_table(rpb, n_rows):
    kh = min(NA_KH, n_rows)
    cols = np.arange(GRID_W)
    col_start = np.clip(cols - NA_KW // 2, 0, GRID_W - NA_KW)
    in_win = (cols[None, :] >= col_start[:, None]) & (cols[None, :] < col_start[:, None] + NA_KW)
    col_off = cols[None, :] - cols[:, None] + (NA_KW - 1)
    onehot = (col_off[None] == np.arange(2 * NA_KW - 1)[:, None, None]) & in_win[None]
    dense = jnp.einsum('hrj,jck->hrck', rpb, jnp.asarray(onehot, F32), precision=HIGHEST)
    dense = jnp.where(jnp.asarray(in_win)[None, None], dense, NEG_BIAS)
    tab = jnp.stack([dense[:, t:t + kh] for t in range(NA_KH)], axis=1)
    tab = tab.transpose(0, 1, 3, 2, 4).reshape(rpb.shape[0], NA_KH, GRID_W, kh * GRID_W)
    return tab.astype(F32)


def _proj_ln_kernel(o_ref, x_ref, mod_ref, w_ref, lng_ref, lnb_ref, out_ref):
    m = mod_ref[0]
    y = _dot(o_ref[...], w_ref[...])
    out_ref[...] = _layer_norm(ALPHA * x_ref[...] + m[2:3] * y, lng_ref[...], lnb_ref[...])


def _proj_ln(o, x, mod, w, ln_g, ln_b, *, tm):
    t, d = o.shape
    return pl.pallas_call(
        _proj_ln_kernel,
        grid=(t // tm,),
        in_specs=[
            pl.BlockSpec((tm, d), lambda i: (i, 0)),
            pl.BlockSpec((tm, d), lambda i: (i, 0)),
            pl.BlockSpec((1, SUBLANES, d), lambda i: (1, 0, 0)),
            _const_spec(w.shape), _const_spec(ln_g.shape), _const_spec(ln_b.shape),
        ],
        out_specs=pl.BlockSpec((tm, d), lambda i: (i, 0)),
        out_shape=jax.ShapeDtypeStruct((t, d), F32),
        compiler_params=_params("arbitrary"),
        name="attn_proj_ln",
    )(o, x, mod, w, ln_g, ln_b)


def _block_diag2(w):
    z = jnp.zeros_like(w[0])
    return jnp.concatenate([jnp.concatenate([w[0], z], axis=1), jnp.concatenate([z, w[1]], axis=1)], axis=0)


def kernel(x, c, ctx, c_ctx, ada_w, ada_b, ln_g, ln_b, ffn_w1, ffn_w3, ffn_w2, ev_w_in, ev_shift_mu, ev_w0,
           ev_w2, ev_a0, ev_a2, ev_g2, ev_k_k, ev_k_a, ev_r_k, ev_lnx_g, ev_lnx_b, ev_pool_w, ev_pool_scale,
           ev_w_out, od_w_in, od_rpb, od_w_out):
    batch, n_lat, d = x.shape
    n_ctx = ctx.shape[1]
    assert batch == 1 and d == D_MODEL
    assert n_ctx % TILE_ROWS == 0 and n_lat % LAT_TILE_ROWS == 0 and n_lat % (ATTN_ROWS_PER_STEP * GRID_W) == 0
    n_lat_tiles = n_lat // TILE_ROWS

    cs = jnp.zeros((SUBLANES, d), F32).at[0].set(c_ctx).at[1].set(c[0])
    mod_all = _ada_modulation(cs, ada_w, ada_b)
    mod_all = mod_all[:, :2].reshape(DEPTH, 2, 6, d)
    mod_all = jnp.pad(mod_all, ((0, 0), (0, 0), (0, SUBLANES - 6), (0, 0)))

    xa = jnp.concatenate([x[0], ctx[0]], axis=0)

    mod = mod_all[0]
    p = _mod_matmul(xa, mod, ev_w_in[0].astype(BF16), n_lat_tiles=n_lat_tiles, out_dtype=F32)
    seg = (jnp.arange(D_A)[:, None] // HEAD == jnp.arange(D_A)[None, :] // HEAD).astype(BF16)
    r, v, kn, ld, kd, bd, g, bonus, pooled = _rwkv_prepare(
        p, ev_shift_mu[0], ev_w0[0], _block_diag2(ev_w2[0]).astype(BF16), ev_a0[0],
        _block_diag2(ev_a2[0]).astype(BF16), ev_g2[0].astype(BF16),
        ev_k_k[0].reshape(1, D_A), ev_k_a[0].reshape(1, D_A), ev_r_k[0].reshape(1, D_A), seg,
        n_lat_tiles=n_lat_tiles)
    y_fwd, y_rev = _wkv_scan(r, v, kn, ld, kd, bd, n_lat_blocks=n_lat_tiles)
    pw = ev_pool_w[0]
    pool_wbd = jnp.zeros((D_B, D_B), F32)
    for gi in range(len(POOL_WINDOWS)):
        sl = slice(gi * POOL_GROUP, (gi + 1) * POOL_GROUP)
        pool_wbd = pool_wbd.at[sl, sl].set(pw[gi])
    xa = _rwkv_readout(y_fwd, y_rev, bonus, g, pooled, xa, mod, seg, ev_lnx_g[0].reshape(1, D_A),
                       ev_lnx_b[0].reshape(1, D_A), pool_wbd.astype(BF16), ev_pool_scale[0].reshape(1, D_B),
                       ev_w_out[0].astype(BF16), ln_g[0, 0].reshape(1, d), ln_b[0, 0].reshape(1, d),
                       n_lat_tiles=n_lat_tiles)
    xa = _ffn(xa, mod, ffn_w1[0].astype(BF16), ffn_w3[0].astype(BF16), ffn_w2[0].astype(BF16),
              ln_g[0, 1].reshape(1, d), ln_b[0, 1].reshape(1, d), tm=TILE_ROWS,
              n_tiles=(n_lat + n_ctx) // TILE_ROWS, n_lat_tiles=n_lat_tiles)

    mod = mod_all[1]
    qkv = _mod_matmul(xa, mod, od_w_in[0].astype(BF16), n_lat_tiles=n_lat_tiles, out_dtype=BF16)
    bias_tab = _attention_bias_table(od_rpb[0], n_lat // GRID_W)
    o = _neighbourhood_attention(qkv, bias_tab, n_ctx=n_ctx, n_lat=n_lat)
    xl = _proj_ln(o, xa, mod, od_w_out[0].astype(BF16), ln_g[1, 0].reshape(1, d), ln_b[1, 0].reshape(1, d),
                  tm=LAT_TILE_ROWS)
    xl = _ffn(xl, mod, ffn_w1[1].astype(BF16), ffn_w3[1].astype(BF16), ffn_w2[1].astype(BF16),
              ln_g[1, 1].reshape(1, d), ln_b[1, 1].reshape(1, d), tm=LAT_TILE_ROWS,
              n_tiles=n_lat // LAT_TILE_ROWS, n_lat_tiles=n_lat // LAT_TILE_ROWS)
    return xl[None]
```

```python
import functools
import math

import jax
import jax.numpy as jnp
import numpy as np
from jax import lax
from jax.experimental import pallas as pl
from jax.experimental.pallas import tpu as pltpu

F32 = jnp.float32
BF16 = jnp.bfloat16
HIGHEST = lax.Precision.HIGHEST

D_MODEL = 1024
DEPTH = 2
GRID_W = 64
ALPHA = (2 * DEPTH) ** 0.25
LN_EPS = 1e-6
HEAD = 64
D_A = 3 * D_MODEL // 4
H_A = D_A // HEAD
LORA_W = 64
LORA_A = 64
LORA_G = 128
GN_EPS = 64e-5
D_B = D_MODEL - D_A
POOL_WINDOWS = (2, 4, 8, 16)
POOL_GROUP = D_B // len(POOL_WINDOWS)
C_RW = 3 * D_A + 2 * LORA_W + 2 * LORA_A + LORA_G
D_IN_EVEN = C_RW + D_B
H_C = D_MODEL // HEAD
NA_KH = 8
NA_KW = 16

LANES = 128
SUBLANES = 8
VMEM_LIMIT_BYTES = 56 * 1024 * 1024

TILE_ROWS = 256
LAT_TILE_ROWS = 512
CHUNK = 64
NEG_BIAS = -1e30


def _sigmoid(x):
    return 1.0 / (1.0 + jnp.exp(-x))


def _layer_norm(z, g, b):
    mu = jnp.mean(z, axis=-1, keepdims=True)
    zc = z - mu
    var = jnp.mean(zc * zc, axis=-1, keepdims=True)
    return zc * lax.rsqrt(var + LN_EPS) * g + b


def _dot(a, b):
    return jnp.dot(a, b, preferred_element_type=F32)


def _dot_hi(a, b):
    return jnp.dot(a, b, precision=HIGHEST, preferred_element_type=F32)


def _dot_nt(a, b):
    return lax.dot_general(a, b, (((1,), (1,)), ((), ())), preferred_element_type=F32)


def _seg_sum(x, seg):
    hi = x.astype(BF16)
    lo = (x - hi.astype(F32)).astype(BF16)
    return _dot(hi, seg) + _dot(lo, seg)


def _params(*sem):
    return pltpu.CompilerParams(dimension_semantics=sem, vmem_limit_bytes=VMEM_LIMIT_BYTES)


def _const_spec(shape):
    nd = len(shape)
    return pl.BlockSpec(shape, lambda *_: (0,) * nd)


def _mod_spec(d, n_lat_tiles):
    return pl.BlockSpec((1, SUBLANES, d), lambda i: ((i < n_lat_tiles).astype(jnp.int32), 0, 0))


def _ada_kernel(cs_ref, w_ref, b_ref, o_ref):
    s = cs_ref[...]
    s = s * _sigmoid(s)
    o_ref[0] = _dot_hi(s, w_ref[0]) + b_ref[0]


def _ada_modulation(cs, ada_w, ada_b):
    depth, d, n = ada_w.shape
    nb = 1536
    return pl.pallas_call(
        _ada_kernel,
        grid=(depth, n // nb),
        in_specs=[
            pl.BlockSpec((SUBLANES, d), lambda i, j: (0, 0)),
            pl.BlockSpec((1, d, nb), lambda i, j: (i, 0, j)),
            pl.BlockSpec((1, 1, nb), lambda i, j: (i, 0, j)),
        ],
        out_specs=pl.BlockSpec((1, SUBLANES, nb), lambda i, j: (i, 0, j)),
        out_shape=jax.ShapeDtypeStruct((depth, SUBLANES, n), F32),
        compiler_params=_params("arbitrary", "arbitrary"),
        name="ada_modulation",
    )(cs, ada_w, ada_b.reshape(depth, 1, n))


def _modmm_kernel(x_ref, mod_ref, w_ref, o_ref):
    m = mod_ref[0]
    h = x_ref[...] * (1.0 + m[1:2]) + m[0:1]
    o_ref[...] = _dot(h.astype(BF16), w_ref[...]).astype(o_ref.dtype)


def _mod_matmul(x, mod, w, *, n_lat_tiles, out_dtype):
    t, d = x.shape
    n = w.shape[1]
    tm = TILE_ROWS
    return pl.pallas_call(
        _modmm_kernel,
        grid=(t // tm,),
        in_specs=[
            pl.BlockSpec((tm, d), lambda i: (i, 0)),
            _mod_spec(d, n_lat_tiles),
            _const_spec((d, n)),
        ],
        out_specs=pl.BlockSpec((tm, n), lambda i: (i, 0)),
        out_shape=jax.ShapeDtypeStruct((t, n), out_dtype),
        compiler_params=_params("arbitrary"),
        name="mod_matmul",
    )(x, mod, w)


def _modmm_join_kernel(x_ref, ctx_ref, mod_ref, w_ref, o_ref, xa_ref, *, n_lat_tiles):
    xt = jnp.where(pl.program_id(0) < n_lat_tiles, x_ref[...], ctx_ref[...])
    m = mod_ref[0]
    h = xt * (1.0 + m[1:2]) + m[0:1]
    o_ref[...] = _dot(h.astype(BF16), w_ref[...]).astype(o_ref.dtype)
    xa_ref[...] = xt


def _mod_matmul_join(x, ctx, mod, w, *, out_dtype):
    n_lat, d = x.shape
    n_ctx = ctx.shape[0]
    n = w.shape[1]
    tm = TILE_ROWS
    n_lat_tiles = n_lat // tm
    n_tiles = (n_lat + n_ctx) // tm
    return pl.pallas_call(
        functools.partial(_modmm_join_kernel, n_lat_tiles=n_lat_tiles),
        grid=(n_tiles,),
        in_specs=[
            pl.BlockSpec((tm, d), lambda i: (jnp.minimum(i, n_lat_tiles - 1), 0)),
            pl.BlockSpec((tm, d), lambda i: (jnp.maximum(i - n_lat_tiles, 0), 0)),
            _mod_spec(d, n_lat_tiles),
            _const_spec((d, n)),
        ],
        out_specs=[pl.BlockSpec((tm, n), lambda i: (i, 0)), pl.BlockSpec((tm, d), lambda i: (i, 0))],
        out_shape=[jax.ShapeDtypeStruct((n_lat + n_ctx, n), out_dtype),
                   jax.ShapeDtypeStruct((n_lat + n_ctx, d), x.dtype)],
        compiler_params=_params("arbitrary"),
        name="mod_matmul_join",
    )(x, ctx, mod, w)


def _prep_kernel(p_ref, pp_ref, pn_ref, mu_ref, w0_ref, w2_ref, a0_ref, a2_ref, g2_ref, kk_ref, ka_ref,
                 rk_ref, seg_ref,
                 r_o, v_o, kn_o, ld_o, kd_o, bd_o, g_o, bonus_o, pool_o, *, tm, n_lat_tiles, n_tiles):
    i = pl.program_id(0)
    first = jnp.logical_or(i == 0, i == n_lat_tiles)
    last = jnp.logical_or(i == n_lat_tiles - 1, i == n_tiles - 1)
    row = lax.broadcasted_iota(jnp.int32, (tm, 1), 0)

    pm = p_ref[:, :C_RW]
    prev_row = jnp.where(first, 0.0, pp_ref[SUBLANES - 1:SUBLANES, :C_RW])
    next_row = jnp.where(last, 0.0, pn_ref[0:1, :C_RW])
    prev = jnp.where(row == 0, prev_row, pltpu.roll(pm, 1, 0))
    nxt = jnp.where(row == tm - 1, next_row, pltpu.roll(pm, tm - 1, 0))
    pm = pm + (prev - pm) * mu_ref[0:1] + (nxt - pm) * mu_ref[1:2]

    s1, s2, s3 = D_A, 2 * D_A, 3 * D_A
    s4 = s3 + 2 * LORA_W
    s5 = s4 + 2 * LORA_A
    r = pm[:, :s1]
    k = pm[:, s1:s2]
    v = pm[:, s2:s3]
    wd = pm[:, s3:s4]
    ad = pm[:, s4:s5]
    gd = pm[:, s5:]

    lw = _dot(jnp.tanh(wd).astype(BF16), w2_ref[...])
    la = _dot(ad.astype(BF16), a2_ref[...])
    g = _dot(_sigmoid(gd).astype(BF16), g2_ref[...])

    kr = k * kk_ref[...]
    ss = _seg_sum(kr * kr, seg_ref[...])
    kn = kr / jnp.maximum(jnp.sqrt(ss), 1e-12)

    decay_scale = math.exp(-0.5)
    kd_sum = jnp.zeros_like(k)
    for d in range(2):
        sl = slice(d * D_A, (d + 1) * D_A)
        ld_o[d] = -decay_scale * _sigmoid(w0_ref[d:d + 1] + lw[:, sl])
        a = _sigmoid(a0_ref[d:d + 1] + la[:, sl])
        kd = k * (1.0 + (a - 1.0) * ka_ref[...])
        kd_o[d] = kd.astype(kd_o.dtype)
        bd_o[d] = (kn * a).astype(bd_o.dtype)
        kd_sum = kd_sum + kd
    coef = _seg_sum(r * kd_sum * rk_ref[...], seg_ref[...])

    r_o[...] = r.astype(r_o.dtype)
    v_o[...] = v.astype(v_o.dtype)
    kn_o[...] = kn.astype(kn_o.dtype)
    g_o[...] = g.astype(g_o.dtype)
    bonus_o[...] = (coef * v).astype(bonus_o.dtype)

    xp = p_ref[:, C_RW:]
    xprev = jnp.where(first, 0.0, pp_ref[:, C_RW:])
    xnext = jnp.where(last, 0.0, pn_ref[:, C_RW:])
    xe = jnp.concatenate([xprev, xp, xnext], axis=0)
    n_ext = tm + 2 * SUBLANES

    def at_offset(a, s):
        return pltpu.roll(a, (-s) % n_ext, 0)

    sums = {}
    acc = xe + at_offset(xe, -1)
    sums[2] = acc
    half = 1
    for win in (4, 8, 16):
        acc = at_offset(acc, -half) + at_offset(acc, half)
        sums[win] = acc
        half *= 2

    in_ctx = i >= n_lat_tiles
    tile0 = jnp.where(in_ctx, n_lat_tiles, 0)
    seq_len = jnp.where(in_ctx, (n_tiles - n_lat_tiles) * tm, n_lat_tiles * tm)
    t_seq = (i - tile0) * tm + row
    grp = lax.broadcasted_iota(jnp.int32, (1, D_B), 1) // POOL_GROUP
    mean = jnp.zeros((tm, D_B), F32)
    for gi, win in enumerate(POOL_WINDOWS):
        cnt = jnp.minimum(t_seq + win // 2, seq_len) - jnp.maximum(t_seq - win // 2, 0)
        m = sums[win][SUBLANES:SUBLANES + tm] / cnt.astype(F32)
        mean = jnp.where(grp == gi, m, mean)
    pool_o[...] = mean - xp


def _rwkv_prepare(p, mu, w0, w2bd, a0, a2bd, g2, k_k, k_a, r_k, seg, *, n_lat_tiles):
    t, n_in = p.shape
    tm = TILE_ROWS
    n_tiles = t // tm
    halo_per_tile = tm // SUBLANES
    n_halo = t // SUBLANES
    tok = jax.ShapeDtypeStruct((t, D_A), BF16)
    tok2 = jax.ShapeDtypeStruct((2, t, D_A), BF16)
    ld2 = jax.ShapeDtypeStruct((2, t, D_A), F32)
    tok_spec = pl.BlockSpec((tm, D_A), lambda i: (i, 0))
    tok2_spec = pl.BlockSpec((2, tm, D_A), lambda i: (0, i, 0))
    kern = functools.partial(_prep_kernel, tm=tm, n_lat_tiles=n_lat_tiles, n_tiles=n_tiles)
    return pl.pallas_call(
        kern,
        grid=(n_tiles,),
        in_specs=[
            pl.BlockSpec((tm, n_in), lambda i: (i, 0)),
            pl.BlockSpec((SUBLANES, n_in), lambda i: (jnp.maximum(i * halo_per_tile - 1, 0), 0)),
            pl.BlockSpec((SUBLANES, n_in), lambda i: (jnp.minimum((i + 1) * halo_per_tile, n_halo - 1), 0)),
            _const_spec(mu.shape), _const_spec(w0.shape), _const_spec(w2bd.shape), _const_spec(a0.shape),
            _const_spec(a2bd.shape), _const_spec(g2.shape), _const_spec(k_k.shape), _const_spec(k_a.shape),
            _const_spec(r_k.shape), _const_spec(seg.shape),
        ],
        out_specs=[tok_spec, tok_spec, tok_spec, tok2_spec, tok2_spec, tok2_spec, tok_spec, tok_spec,
                   pl.BlockSpec((tm, D_B), lambda i: (i, 0))],
        out_shape=[tok, tok, tok, ld2, tok2, tok2, tok, tok, jax.ShapeDtypeStruct((t, D_B), F32)],
        compiler_params=_params("arbitrary"),
        name="rwkv_prepare",
    )(p, p, p, mu, w0, w2bd, a0, a2bd, g2, k_k, k_a, r_k, seg)


WKV_PAIRS_PER_STEP = 3


def _wkv_kernel(rf_ref, vf_ref, knf_ref, ldf_ref, kdf_ref, bdf_ref, rr_ref, vr_ref, knr_ref, ldr_ref, kdr_ref,
                bdr_ref, yf_ref, yr_ref, h_ref, lhs_s, add_s, pb_s, ends_s, dec_s, v_s, *, chunks_per_block,
                pairs_per_step):
    step = pl.program_id(1)
    cur = lax.rem(step, 2)
    prev = 1 - cur

    @pl.when(step == 0)
    def _():
        h_ref[...] = jnp.zeros_like(h_ref)
        for ref in (lhs_s, add_s, pb_s, ends_s, dec_s, v_s):
            ref[1] = jnp.zeros(ref.shape[1:], ref.dtype)

    c = CHUNK
    ri = lax.broadcasted_iota(jnp.int32, (c, c), 0)
    ci = lax.broadcasted_iota(jnp.int32, (c, c), 1)
    rp = lax.broadcasted_iota(jnp.int32, (c, LANES), 0)
    cp = lax.broadcasted_iota(jnp.int32, (c, LANES), 1) & (c - 1)
    eye_p = jnp.where(rp == cp, 1.0, 0.0)
    r2 = lax.broadcasted_iota(jnp.int32, (LANES, LANES), 0)
    c2 = lax.broadcasted_iota(jnp.int32, (LANES, LANES), 1)
    same_head = (r2 < HEAD) == (c2 < HEAD)
    head0 = lax.broadcasted_iota(jnp.int32, (1, LANES), 1) < HEAD

    def bdiag(x):
        zero = jnp.zeros_like(x)
        return jnp.concatenate([jnp.where(head0, x, zero), jnp.where(head0, zero, x)], axis=0)

    masks = {}
    for rev in (False, True):
        sgn = -1 if rev else 1
        order = (rp - cp) * sgn
        masks[rev] = (jnp.where((ri - ci) * sgn >= 0, 1.0, 0.0).astype(BF16), order > 0, order >= 0)

    chains = []
    for rev, refs, y_ref in ((False, (rf_ref, vf_ref, knf_ref, ldf_ref, kdf_ref, bdf_ref), yf_ref),
                             (True, (rr_ref, vr_ref, knr_ref, ldr_ref, kdr_ref, bdr_ref), yr_ref)):
        for q in range(pairs_per_step):
            chains.append((rev, refs, y_ref, slice(q * LANES, (q + 1) * LANES)))

    insts = []
    for ch, (rev, refs, y_ref, lanes) in enumerate(chains):
        for j in range(chunks_per_block):
            cj = chunks_per_block - 1 - j if rev else j
            insts.append(dict(idx=len(insts), ch=ch, j=j, rev=rev, refs=refs, lanes=lanes,
                              rows=slice(cj * c, (cj + 1) * c)))
    by_key = {(it["ch"], it["j"]): it for it in insts}

    states = [h_ref[ch] for ch in range(len(chains))]
    pending = {}

    def state_group(k):
        j, second = divmod(k, 2)
        for ch, (rev, refs, y_ref, lanes) in enumerate(chains):
            it = by_key[(ch, j)]
            idx = it["idx"]
            if not second:
                pending[ch] = _dot(lhs_s[prev, idx], states[ch].astype(BF16))
            else:
                m1 = pending.pop(ch)
                add = add_s[prev, idx]
                ub = (-m1[:c] - add[:c]).astype(BF16)
                y_ref[it["rows"], lanes] = m1[c:] + add[c:] + _dot(pb_s[prev, idx], bdiag(ub))
                upd = _dot(ends_s[prev, idx], jnp.concatenate([ub, v_s[prev, idx]], axis=0))
                states[ch] = states[ch] * dec_s[prev, idx] + jnp.where(same_head, upd, 0.0)

    n_state_groups = 2 * chunks_per_block
    emitted = [0]

    def interleave_state_group():
        if emitted[0] < n_state_groups:
            state_group(emitted[0])
            emitted[0] += 1

    for it in insts:
        ld = it["refs"][3][0, it["rows"], it["lanes"]]
        tri = masks[it["rev"]][0]
        ld_hi = ld.astype(BF16)
        rem = ld - ld_hi.astype(F32)
        ld_mid = rem.astype(BF16)
        ld_lo = (rem - ld_mid.astype(F32)).astype(BF16)
        part = _dot(tri, jnp.concatenate([ld_hi, ld_mid], axis=1))
        it["cum"] = part[:, :LANES] + part[:, LANES:] + _dot(tri, ld_lo)
    interleave_state_group()

    for it in insts:
        r_ref, v_ref, kn_ref, ld_ref, kd_ref, bd_ref = it["refs"]
        rows, lanes = it["rows"], it["lanes"]
        _, strict, incl = masks[it["rev"]]
        ld = ld_ref[0, rows, lanes]
        kd = kd_ref[0, rows, lanes].astype(F32)
        bd = bd_ref[0, rows, lanes].astype(F32)
        cum = it.pop("cum")
        total = jnp.sum(ld, axis=0, keepdims=True)
        inv = jnp.exp(-cum)
        to_end = jnp.exp(total - cum)
        kq = (kn_ref[rows, lanes].astype(F32) * jnp.exp(cum - ld)).astype(BF16)
        rq = (r_ref[rows, lanes].astype(F32) * jnp.exp(cum)).astype(BF16)
        g = _dot_nt(jnp.concatenate([kq, rq], axis=0),
                    jnp.concatenate([bdiag((bd * inv).astype(BF16)), bdiag((kd * inv).astype(BF16))], axis=0))
        it["q"] = jnp.where(strict, -g[:c, :LANES], 0.0)
        it["bm"] = jnp.where(strict, g[:c, LANES:], 0.0).astype(BF16)
        it["pk"] = jnp.where(incl, g[c:, LANES:], 0.0).astype(BF16)
        it["kq_bd"], it["rq"] = bdiag(kq), rq
        v = v_ref[rows, lanes]
        it["v_bd"] = bdiag(v)
        idx = it["idx"]
        pb_s[cur, idx] = jnp.where(incl, g[c:, :LANES], 0.0).astype(BF16)
        v_s[cur, idx] = v
        ends_s[cur, idx] = jnp.concatenate([bd * to_end, kd * to_end], axis=0).T.astype(BF16)
        dec_s[cur, idx] = jnp.broadcast_to(jnp.exp(jnp.sum(ld.T, axis=1, keepdims=True)), (LANES, LANES))
    interleave_state_group()

    for it in insts:
        q = it.pop("q")
        it["t_inv"] = eye_p + q
        qb = q.astype(BF16)
        it["pw"] = _dot(qb, bdiag(qb)).astype(BF16)
    interleave_state_group()
    n_levels = int(math.log2(c)) - 1
    for level in range(n_levels):
        for it in insts:
            pw = it["pw"]
            t_bd = bdiag(it["t_inv"].astype(BF16))
            if level < n_levels - 1:
                res = _dot(pw, jnp.concatenate([bdiag(pw), t_bd], axis=1))
                it["pw"] = res[:, :LANES].astype(BF16)
                it["t_inv"] = it["t_inv"] + res[:, LANES:]
            else:
                it["t_inv"] = it["t_inv"] + _dot(pw, t_bd)
        interleave_state_group()
    while emitted[0] < n_state_groups:
        interleave_state_group()
    for it in insts:
        it["bmv"] = _dot(it.pop("bm"), it["v_bd"]).astype(BF16)
    for it in insts:
        wu = _dot(it.pop("t_inv").astype(BF16), jnp.concatenate([it.pop("kq_bd"), bdiag(it.pop("bmv"))], axis=1))
        lhs_s[cur, it["idx"]] = jnp.concatenate([wu[:, :LANES].astype(BF16), it.pop("rq")], axis=0)
        it["ut"] = wu[:, LANES:]
    for it in insts:
        pkv = _dot(it.pop("pk"), it["v_bd"])
        add_s[cur, it["idx"]] = jnp.concatenate([it.pop("ut"), pkv], axis=0)
    for ch in range(len(chains)):
        h_ref[ch] = states[ch]


def _wkv_scan(r, v, kn, ld, kd, bd, *, n_lat_blocks):
    t, da = r.shape
    rows = TILE_ROWS
    n_blocks = t // rows
    pps = WKV_PAIRS_PER_STEP
    width = pps * LANES
    n_groups = da // width

    n_ctx_blocks = n_blocks - n_lat_blocks

    def fwd_blk(s):
        return jnp.where(s < n_ctx_blocks, n_lat_blocks + s, s - n_ctx_blocks)

    def rev_blk(s):
        return jnp.where(s < n_ctx_blocks, n_blocks - 1 - s, n_lat_blocks - 1 - (s - n_ctx_blocks))

    def cur(s):
        return jnp.minimum(s, n_blocks - 1)

    def prv(s):
        return jnp.maximum(s - 1, 0)

    fwd = pl.BlockSpec((rows, width), lambda g, s: (fwd_blk(cur(s)), g))
    bwd = pl.BlockSpec((rows, width), lambda g, s: (rev_blk(cur(s)), g))
    fwd_d = pl.BlockSpec((1, rows, width), lambda g, s: (0, fwd_blk(cur(s)), g))
    bwd_d = pl.BlockSpec((1, rows, width), lambda g, s: (1, rev_blk(cur(s)), g))
    y_fwd = pl.BlockSpec((rows, width), lambda g, s: (fwd_blk(prv(s)), g))
    y_bwd = pl.BlockSpec((rows, width), lambda g, s: (rev_blk(prv(s)), g))
    chunks = rows // CHUNK
    n_inst = 2 * pps * chunks
    kern = functools.partial(_wkv_kernel, chunks_per_block=chunks, pairs_per_step=pps)
    out = jax.ShapeDtypeStruct((t, da), F32)
    return pl.pallas_call(
        kern,
        grid=(n_groups, n_blocks + 1),
        in_specs=[fwd, fwd, fwd, fwd_d, fwd_d, fwd_d, bwd, bwd, bwd, bwd_d, bwd_d, bwd_d],
        out_specs=[y_fwd, y_bwd],
        out_shape=[out, out],
        scratch_shapes=[
            pltpu.VMEM((2 * pps, LANES, LANES), F32),
            pltpu.VMEM((2, n_inst, 2 * CHUNK, LANES), BF16),
            pltpu.VMEM((2, n_inst, 2 * CHUNK, LANES), F32),
            pltpu.VMEM((2, n_inst, CHUNK, LANES), BF16),
            pltpu.VMEM((2, n_inst, LANES, LANES), BF16),
            pltpu.VMEM((2, n_inst, LANES, LANES), F32),
            pltpu.VMEM((2, n_inst, CHUNK, LANES), BF16),
        ],
        compiler_params=_params("arbitrary", "arbitrary"),
        name="wkv_scan",
    )(r, v, kn, ld, kd, bd, r, v, kn, ld, kd, bd)


def _readout_kernel(yf_ref, yr_ref, bonus_ref, g_ref, pool_ref, x_ref, mod_ref, seg_ref, lnxg_ref, lnxb_ref,
                    poolw_ref, pools_ref, wout_ref, lng_ref, lnb_ref, o_ref):
    y = yf_ref[...] + yr_ref[...]
    seg = seg_ref[...]
    mu = _seg_sum(y, seg) * (1.0 / HEAD)
    yc = y - mu
    var = _seg_sum(yc * yc, seg) * (1.0 / HEAD)
    yn = yc * lax.rsqrt(var + GN_EPS) * lnxg_ref[...] + lnxb_ref[...]
    a = (yn + bonus_ref[...].astype(F32)) * g_ref[...].astype(F32)
    b = _dot(pool_ref[...].astype(BF16), poolw_ref[...]) * pools_ref[...]
    out = _dot(a.astype(BF16), wout_ref[:D_A, :]) + _dot(b.astype(BF16), wout_ref[D_A:, :])
    m = mod_ref[0]
    o_ref[...] = _layer_norm(ALPHA * x_ref[...] + m[2:3] * out, lng_ref[...], lnb_ref[...])


def _rwkv_readout(y_fwd, y_rev, bonus, g, pooled, x, mod, seg, lnx_g, lnx_b, pool_wbd, pool_scale, w_out, ln_g, ln_b,
                  *, n_lat_tiles):
    t, d = x.shape
    tm = TILE_ROWS
    tok = pl.BlockSpec((tm, D_A), lambda i: (i, 0))
    return pl.pallas_call(
        _readout_kernel,
        grid=(t // tm,),
        in_specs=[
            tok, tok, tok, tok,
            pl.BlockSpec((tm, D_B), lambda i: (i, 0)),
            pl.BlockSpec((tm, d), lambda i: (i, 0)),
            _mod_spec(d, n_lat_tiles),
            _const_spec(seg.shape), _const_spec(lnx_g.shape), _const_spec(lnx_b.shape),
            _const_spec(pool_wbd.shape), _const_spec(pool_scale.shape), _const_spec(w_out.shape),
            _const_spec(ln_g.shape), _const_spec(ln_b.shape),
        ],
        out_specs=pl.BlockSpec((tm, d), lambda i: (i, 0)),
        out_shape=jax.ShapeDtypeStruct((t, d), F32),
        compiler_params=_params("arbitrary"),
        name="rwkv_readout",
    )(y_fwd, y_rev, bonus, g, pooled, x, mod, seg, lnx_g, lnx_b, pool_wbd, pool_scale, w_out, ln_g, ln_b)


def _ffn_kernel(x_ref, mod_ref, w1_ref, w3_ref, w2_ref, lng_ref, lnb_ref, o_ref):
    m = mod_ref[0]
    x = x_ref[...]
    h = (x * (1.0 + m[4:5]) + m[3:4]).astype(BF16)
    a = _dot(h, w1_ref[...])
    b = _dot(h, w3_ref[...])
    act = (a * _sigmoid(a) * b).astype(BF16)
    out = _dot(act, w2_ref[...])
    o_ref[...] = _layer_norm(ALPHA * x + m[5:6] * out, lng_ref[...], lnb_ref[...])


def _ffn(x, mod, w1, w3, w2, ln_g, ln_b, *, tm, n_tiles, n_lat_tiles):
    d = x.shape[1]
    single = pl.Buffered(1)
    return pl.pallas_call(
        _ffn_kernel,
        grid=(n_tiles,),
        in_specs=[
            pl.BlockSpec((tm, d), lambda i: (i, 0)),
            _mod_spec(d, n_lat_tiles),
            pl.BlockSpec(w1.shape, lambda i: (0, 0), pipeline_mode=single),
            pl.BlockSpec(w3.shape, lambda i: (0, 0), pipeline_mode=single),
            pl.BlockSpec(w2.shape, lambda i: (0, 0), pipeline_mode=single),
            _const_spec(ln_g.shape), _const_spec(ln_b.shape),
        ],
        out_specs=pl.BlockSpec((tm, d), lambda i: (i, 0)),
        out_shape=jax.ShapeDtypeStruct((n_tiles * tm, d), F32),
        compiler_params=_params("arbitrary"),
        name="ffn",
    )(x, mod, w1, w3, w2, ln_g, ln_b)


ATTN_ROWS_PER_STEP = 16


def _attn_kernel(q_ref, k_ref, v_ref, bias_ref, o_ref, *, q_rows, n_rows, n_lat, n_ctx, kh):
    rb = pl.program_id(1)
    head0 = lax.broadcasted_iota(jnp.int32, (1, LANES), 1) < HEAD
    kc = k_ref[n_lat:n_lat + n_ctx, :]
    vc = v_ref[n_lat:n_lat + n_ctx, :]
    n_loc = kh * GRID_W
    q_all = q_ref[...] * jnp.asarray(HEAD ** -0.5, q_ref.dtype)
    zero = jnp.zeros((GRID_W, LANES), q_all.dtype)

    rows = []
    for j in range(q_rows):
        rr = rb * q_rows + j
        sr = jnp.clip(rr - kh // 2, 0, n_rows - kh)
        t_var = sr - rr + (NA_KH - 1)
        q = q_all[j * GRID_W:(j + 1) * GRID_W]
        q2 = jnp.concatenate([jnp.where(head0, q, zero), jnp.where(head0, zero, q)], axis=0)
        krows = pl.ds(pl.multiple_of(sr * GRID_W, GRID_W), n_loc)
        bias = jnp.concatenate([bias_ref[0, t_var], bias_ref[1, t_var]], axis=0)
        s_loc = _dot_nt(q2, k_ref[krows, :]) + bias
        s_ctx = _dot_nt(q2, kc)
        rows.append(dict(krows=krows, s_loc=s_loc, s_ctx=s_ctx))
    for it in rows:
        s_loc, s_ctx = it.pop("s_loc"), it.pop("s_ctx")
        m = jnp.maximum(jnp.max(s_loc, axis=-1, keepdims=True), jnp.max(s_ctx, axis=-1, keepdims=True))
        p_loc = jnp.exp(s_loc - m)
        p_ctx = jnp.exp(s_ctx - m)
        it["denom"] = jnp.sum(p_loc, axis=-1, keepdims=True) + jnp.sum(p_ctx, axis=-1, keepdims=True)
        it["p_loc"] = p_loc.astype(BF16)
        it["p_ctx"] = p_ctx.astype(BF16)
    for j, it in enumerate(rows):
        o2 = (_dot(it["p_loc"], v_ref[it["krows"], :]) + _dot(it["p_ctx"], vc)) / it["denom"]
        o_ref[j * GRID_W:(j + 1) * GRID_W, :] = jnp.where(head0, o2[:GRID_W], o2[GRID_W:]).astype(o_ref.dtype)


def _neighbourhood_attention(qkv, bias_tab, *, n_ctx, n_lat):
    t_all = qkv.shape[0]
    d = D_MODEL
    n_rows = n_lat // GRID_W
    kh = min(NA_KH, n_rows)
    q_rows = ATTN_ROWS_PER_STEP
    q_tile = q_rows * GRID_W
    n_pairs = d // LANES
    kern = functools.partial(_attn_kernel, q_rows=q_rows, n_rows=n_rows, n_lat=n_lat, n_ctx=n_ctx, kh=kh)
    n_var = bias_tab.shape[1]
    return pl.pallas_call(
        kern,
        grid=(n_pairs, n_lat // q_tile),
        in_specs=[
            pl.BlockSpec((q_tile, LANES), lambda p, b: (b, p)),
            pl.BlockSpec((t_all, LANES), lambda p, b: (0, n_pairs + p)),
            pl.BlockSpec((t_all, LANES), lambda p, b: (0, 2 * n_pairs + p)),
            pl.BlockSpec((2, n_var, GRID_W, kh * GRID_W), lambda p, b: (p, 0, 0, 0)),
        ],
        out_specs=pl.BlockSpec((q_tile, LANES), lambda p, b: (b, p)),
        out_shape=jax.ShapeDtypeStruct((n_lat, d), BF16),
        compiler_params=_params("arbitrary", "arbitrary"),
        name="neighbourhood_attention",
    )(qkv, qkv, qkv, bias_tab)


def _attention_bias_table(rpb, n_rows):
    kh = min(NA_KH, n_rows)
    cols = np.arange(GRID_W)
    col_start = np.clip(cols - NA_KW // 2, 0, GRID_W - NA_KW)
    in_win = (cols[None, :] >= col_start[:, None]) & (cols[None, :] < col_start[:, None] + NA_KW)
    col_off = cols[None, :] - cols[:, None] + (NA_KW - 1)
    onehot = (col_off[None] == np.arange(2 * NA_KW - 1)[:, None, None]) & in_win[None]
    dense = jnp.einsum('hrj,jck->hcrk', rpb, jnp.asarray(onehot, F32), precision=HIGHEST)
    dense = jnp.where(jnp.asarray(in_win)[None, :, None, :], dense, NEG_BIAS)
    tab = jnp.stack([dense[:, :, t:t + kh].reshape(rpb.shape[0], GRID_W, kh * GRID_W) for t in range(NA_KH)],
                    axis=1)
    return tab.astype(F32)


def _proj_ln_kernel(o_ref, x_ref, mod_ref, w_ref, lng_ref, lnb_ref, out_ref):
    m = mod_ref[0]
    y = _dot(o_ref[...], w_ref[...])
    out_ref[...] = _layer_norm(ALPHA * x_ref[...] + m[2:3] * y, lng_ref[...], lnb_ref[...])


def _proj_ln(o, x, mod, w, ln_g, ln_b, *, tm):
    t, d = o.shape
    return pl.pallas_call(
        _proj_ln_kernel,
        grid=(t // tm,),
        in_specs=[
            pl.BlockSpec((tm, d), lambda i: (i, 0)),
            pl.BlockSpec((tm, d), lambda i: (i, 0)),
            pl.BlockSpec((1, SUBLANES, d), lambda i: (1, 0, 0)),
            _const_spec(w.shape), _const_spec(ln_g.shape), _const_spec(ln_b.shape),
        ],
        out_specs=pl.BlockSpec((tm, d), lambda i: (i, 0)),
        out_shape=jax.ShapeDtypeStruct((t, d), F32),
        compiler_params=_params("arbitrary"),
        name="attn_proj_ln",
    )(o, x, mod, w, ln_g, ln_b)


def _block_diag2(w):
    z = jnp.zeros_like(w[0])
    return jnp.concatenate([jnp.concatenate([w[0], z], axis=1), jnp.concatenate([z, w[1]], axis=1)], axis=0)


def kernel(x, c, ctx, c_ctx, ada_w, ada_b, ln_g, ln_b, ffn_w1, ffn_w3, ffn_w2, ev_w_in, ev_shift_mu, ev_w0,
           ev_w2, ev_a0, ev_a2, ev_g2, ev_k_k, ev_k_a, ev_r_k, ev_lnx_g, ev_lnx_b, ev_pool_w, ev_pool_scale,
           ev_w_out, od_w_in, od_rpb, od_w_out):
    batch, n_lat, d = x.shape
    n_ctx = ctx.shape[1]
    assert batch == 1 and d == D_MODEL
    assert n_ctx % TILE_ROWS == 0 and n_lat % LAT_TILE_ROWS == 0 and n_lat % (ATTN_ROWS_PER_STEP * GRID_W) == 0
    n_lat_tiles = n_lat // TILE_ROWS

    cs = jnp.zeros((SUBLANES, d), F32).at[0].set(c_ctx).at[1].set(c[0])
    mod_all = _ada_modulation(cs, ada_w, ada_b)
    mod_all = mod_all[:, :2].reshape(DEPTH, 2, 6, d)
    mod_all = jnp.pad(mod_all, ((0, 0), (0, 0), (0, SUBLANES - 6), (0, 0)))

    mod = mod_all[0]
    p, xa = _mod_matmul_join(x[0], ctx[0], mod, ev_w_in[0].astype(BF16), out_dtype=F32)
    seg = (jnp.arange(D_A)[:, None] // HEAD == jnp.arange(D_A)[None, :] // HEAD).astype(BF16)
    r, v, kn, ld, kd, bd, g, bonus, pooled = _rwkv_prepare(
        p, ev_shift_mu[0], ev_w0[0], _block_diag2(ev_w2[0]).astype(BF16), ev_a0[0],
        _block_diag2(ev_a2[0]).astype(BF16), ev_g2[0].astype(BF16),
        ev_k_k[0].reshape(1, D_A), ev_k_a[0].reshape(1, D_A), ev_r_k[0].reshape(1, D_A), seg,
        n_lat_tiles=n_lat_tiles)
    y_fwd, y_rev = _wkv_scan(r, v, kn, ld, kd, bd, n_lat_blocks=n_lat_tiles)
    pw = ev_pool_w[0]
    pool_wbd = jnp.zeros((D_B, D_B), F32)
    for gi in range(len(POOL_WINDOWS)):
        sl = slice(gi * POOL_GROUP, (gi + 1) * POOL_GROUP)
        pool_wbd = pool_wbd.at[sl, sl].set(pw[gi])
    xa = _rwkv_readout(y_fwd, y_rev, bonus, g, pooled, xa, mod, seg, ev_lnx_g[0].reshape(1, D_A),
                       ev_lnx_b[0].reshape(1, D_A), pool_wbd.astype(BF16), ev_pool_scale[0].reshape(1, D_B),
                       ev_w_out[0].astype(BF16), ln_g[0, 0].reshape(1, d), ln_b[0, 0].reshape(1, d),
                       n_lat_tiles=n_lat_tiles)
    xa = _ffn(xa, mod, ffn_w1[0].astype(BF16), ffn_w3[0].astype(BF16), ffn_w2[0].astype(BF16),
              ln_g[0, 1].reshape(1, d), ln_b[0, 1].reshape(1, d), tm=TILE_ROWS,
              n_tiles=(n_lat + n_ctx) // TILE_ROWS, n_lat_tiles=n_lat_tiles)

    mod = mod_all[1]
    qkv = _mod_matmul(xa, mod, od_w_in[0].astype(BF16), n_lat_tiles=n_lat_tiles, out_dtype=BF16)
    bias_tab = _attention_bias_table(od_rpb[0], n_lat // GRID_W)
    o = _neighbourhood_attention(qkv, bias_tab, n_ctx=n_ctx, n_lat=n_lat)
    xl = _proj_ln(o, xa, mod, od_w_out[0].astype(BF16), ln_g[1, 0].reshape(1, d), ln_b[1, 0].reshape(1, d),
                  tm=LAT_TILE_ROWS)
    xl = _ffn(xl, mod, ffn_w1[1].astype(BF16), ffn_w3[1].astype(BF16), ffn_w2[1].astype(BF16),
              ln_g[1, 1].reshape(1, d), ln_b[1, 1].reshape(1, d), tm=LAT_TILE_ROWS,
              n_tiles=n_lat // LAT_TILE_ROWS, n_lat_tiles=n_lat // LAT_TILE_ROWS)
    return xl[None]
```

```python
import functools
import math

import jax
import jax.numpy as jnp
import numpy as np
from jax import lax
from jax.experimental import pallas as pl
from jax.experimental.pallas import tpu as pltpu

F32 = jnp.float32
BF16 = jnp.bfloat16
HIGHEST = lax.Precision.HIGHEST

D_MODEL = 1024
DEPTH = 2
GRID_W = 64
ALPHA = (2 * DEPTH) ** 0.25
LN_EPS = 1e-6
HEAD = 64
D_A = 3 * D_MODEL // 4
H_A = D_A // HEAD
LORA_W = 64
LORA_A = 64
LORA_G = 128
GN_EPS = 64e-5
D_B = D_MODEL - D_A
POOL_WINDOWS = (2, 4, 8, 16)
POOL_GROUP = D_B // len(POOL_WINDOWS)
C_RW = 3 * D_A + 2 * LORA_W + 2 * LORA_A + LORA_G
D_IN_EVEN = C_RW + D_B
H_C = D_MODEL // HEAD
NA_KH = 8
NA_KW = 16

LANES = 128
SUBLANES = 8
VMEM_LIMIT_BYTES = 56 * 1024 * 1024

TILE_ROWS = 256
LAT_TILE_ROWS = 512
CHUNK = 64
NEG_BIAS = -1e30


def _sigmoid(x):
    return 1.0 / (1.0 + jnp.exp(-x))


def _layer_norm(z, g, b):
    mu = jnp.mean(z, axis=-1, keepdims=True)
    zc = z - mu
    var = jnp.mean(zc * zc, axis=-1, keepdims=True)
    return zc * lax.rsqrt(var + LN_EPS) * g + b


def _dot(a, b):
    return jnp.dot(a, b, preferred_element_type=F32)


def _dot_hi(a, b):
    return jnp.dot(a, b, precision=HIGHEST, preferred_element_type=F32)


def _dot_nt(a, b):
    return lax.dot_general(a, b, (((1,), (1,)), ((), ())), preferred_element_type=F32)


def _seg_sum(x, seg):
    xb = x.astype(BF16)
    return jnp.concatenate([_dot(xb[:, j:j + LANES], seg) for j in range(0, x.shape[1], LANES)], axis=1)


def _params(*sem):
    return pltpu.CompilerParams(dimension_semantics=sem, vmem_limit_bytes=VMEM_LIMIT_BYTES)


def _const_spec(shape):
    nd = len(shape)
    return pl.BlockSpec(shape, lambda *_: (0,) * nd)


def _mod_spec(d, n_lat_tiles):
    return pl.BlockSpec((1, SUBLANES, d), lambda i: ((i < n_lat_tiles).astype(jnp.int32), 0, 0))


def _ada_kernel(cs_ref, w_ref, b_ref, o_ref):
    s = cs_ref[...]
    s = s * _sigmoid(s)
    o_ref[0] = _dot_hi(s, w_ref[0]) + b_ref[0]


def _ada_modulation(cs, ada_w, ada_b):
    depth, d, n = ada_w.shape
    nb = 1536
    return pl.pallas_call(
        _ada_kernel,
        grid=(depth, n // nb),
        in_specs=[
            pl.BlockSpec((SUBLANES, d), lambda i, j: (0, 0)),
            pl.BlockSpec((1, d, nb), lambda i, j: (i, 0, j)),
            pl.BlockSpec((1, 1, nb), lambda i, j: (i, 0, j)),
        ],
        out_specs=pl.BlockSpec((1, SUBLANES, nb), lambda i, j: (i, 0, j)),
        out_shape=jax.ShapeDtypeStruct((depth, SUBLANES, n), F32),
        compiler_params=_params("arbitrary", "arbitrary"),
        name="ada_modulation",
    )(cs, ada_w, ada_b.reshape(depth, 1, n))


def _modmm_kernel(x_ref, mod_ref, w_ref, o_ref):
    m = mod_ref[0]
    h = x_ref[...] * (1.0 + m[1:2]) + m[0:1]
    o_ref[...] = _dot(h.astype(BF16), w_ref[...]).astype(o_ref.dtype)


def _mod_matmul(x, mod, w, *, n_lat_tiles, out_dtype):
    t, d = x.shape
    n = w.shape[1]
    tm = TILE_ROWS
    return pl.pallas_call(
        _modmm_kernel,
        grid=(t // tm,),
        in_specs=[
            pl.BlockSpec((tm, d), lambda i: (i, 0)),
            _mod_spec(d, n_lat_tiles),
            _const_spec((d, n)),
        ],
        out_specs=pl.BlockSpec((tm, n), lambda i: (i, 0)),
        out_shape=jax.ShapeDtypeStruct((t, n), out_dtype),
        compiler_params=_params("arbitrary"),
        name="mod_matmul",
    )(x, mod, w)


def _prep_kernel(x_ref, ctx_ref, xhp_ref, xhn_ref, chp_ref, chn_ref, mod_ref, w_ref, mu_ref, w0_ref, w2_ref,
                 a0_ref, a2_ref, g2_ref, kk_ref, ka_ref, rk_ref, seg_ref,
                 r_o, v_o, kn_o, ld_o, kd_o, bd_o, g_o, bonus_o, pool_o, xa_o, *, tm, n_lat_tiles, n_tiles):
    i = pl.program_id(0)
    is_lat = i < n_lat_tiles
    first = jnp.logical_or(i == 0, i == n_lat_tiles)
    last = jnp.logical_or(i == n_lat_tiles - 1, i == n_tiles - 1)
    row = lax.broadcasted_iota(jnp.int32, (tm, 1), 0)
    n_ext = tm + 2 * SUBLANES

    xt = jnp.where(is_lat, x_ref[...], ctx_ref[...])
    xa_o[...] = xt
    x_ext = jnp.concatenate([jnp.where(is_lat, xhp_ref[...], chp_ref[...]), xt,
                             jnp.where(is_lat, xhn_ref[...], chn_ref[...])], axis=0)
    m = mod_ref[0]
    h_ext = (x_ext * (1.0 + m[1:2]) + m[0:1]).astype(BF16)
    mu_prev = mu_ref[0:1]
    mu_next = mu_ref[1:2]
    mu_self = 1.0 - mu_prev - mu_next
    inner = slice(SUBLANES, SUBLANES + tm)

    def project(c0, c1):
        p = _dot(h_ext, w_ref[:, c0:c1])
        return jnp.concatenate([jnp.where(first, 0.0, p[:SUBLANES]), p[inner],
                                jnp.where(last, 0.0, p[SUBLANES + tm:])], axis=0)

    def shift_mix(p, c0, c1):
        return (p[inner] * mu_self[:, c0:c1] + pltpu.roll(p, 1, 0)[inner] * mu_prev[:, c0:c1]
                + pltpu.roll(p, n_ext - 1, 0)[inner] * mu_next[:, c0:c1])

    s1, s2, s3 = D_A, 2 * D_A, 3 * D_A
    s4 = s3 + 2 * LORA_W
    s5 = s4 + 2 * LORA_A
    lora = shift_mix(project(s3, C_RW), s3, C_RW)
    k_ext = project(s1, s2)
    lw = _dot(jnp.tanh(lora[:, :s4 - s3]).astype(BF16), w2_ref[...])
    la = _dot(lora[:, s4 - s3:s5 - s3].astype(BF16), a2_ref[...])
    g = _dot(_sigmoid(lora[:, s5 - s3:]).astype(BF16), g2_ref[...])
    g_o[...] = g.astype(g_o.dtype)

    r_ext = project(0, s1)
    k = shift_mix(k_ext, s1, s2)
    kr = k * kk_ref[...]
    ss = _seg_sum(kr * kr, seg_ref[...])
    kn = kr * lax.rsqrt(jnp.maximum(ss, 1e-24))
    kn_o[...] = kn.astype(kn_o.dtype)

    v_ext = project(s2, s3)
    decay_scale = math.exp(-0.5)
    kd_sum = jnp.zeros_like(k)
    for d in range(2):
        sl = slice(d * D_A, (d + 1) * D_A)
        ld_o[d] = -decay_scale * _sigmoid(w0_ref[d:d + 1] + lw[:, sl])
        a = _sigmoid(a0_ref[d:d + 1] + la[:, sl])
        kd = k * (1.0 + (a - 1.0) * ka_ref[...])
        kd_o[d] = kd.astype(kd_o.dtype)
        bd_o[d] = (kn * a).astype(bd_o.dtype)
        kd_sum = kd_sum + kd

    xe = project(C_RW, D_IN_EVEN)
    r = shift_mix(r_ext, 0, s1)
    r_o[...] = r.astype(r_o.dtype)
    coef = _seg_sum(r * kd_sum * rk_ref[...], seg_ref[...])
    v = shift_mix(v_ext, s2, s3)
    v_o[...] = v.astype(v_o.dtype)
    bonus_o[...] = (coef * v).astype(bonus_o.dtype)

    xp = xe[inner]

    def at_offset(a, s):
        return pltpu.roll(a, (-s) % n_ext, 0)

    sums = {}
    acc = xe + at_offset(xe, -1)
    sums[2] = acc
    half = 1
    for win in (4, 8, 16):
        acc = at_offset(acc, -half) + at_offset(acc, half)
        sums[win] = acc
        half *= 2

    in_ctx = i >= n_lat_tiles
    tile0 = jnp.where(in_ctx, n_lat_tiles, 0)
    seq_len = jnp.where(in_ctx, (n_tiles - n_lat_tiles) * tm, n_lat_tiles * tm)
    t_seq = (i - tile0) * tm + row
    grp = lax.broadcasted_iota(jnp.int32, (1, D_B), 1) // POOL_GROUP
    mean = jnp.zeros((tm, D_B), F32)
    for gi, win in enumerate(POOL_WINDOWS):
        cnt = jnp.minimum(t_seq + win // 2, seq_len) - jnp.maximum(t_seq - win // 2, 0)
        m = sums[win][SUBLANES:SUBLANES + tm] / cnt.astype(F32)
        mean = jnp.where(grp == gi, m, mean)
    pool_o[...] = mean - xp


def _rwkv_prepare(x, ctx, mod, w_in, mu, w0, w2bd, a0, a2bd, g2, k_k, k_a, r_k, seg):
    n_lat, d = x.shape
    n_ctx = ctx.shape[0]
    t = n_lat + n_ctx
    tm = TILE_ROWS
    n_tiles = t // tm
    n_lat_tiles = n_lat // tm
    hpt = tm // SUBLANES
    lat_halos = n_lat // SUBLANES
    ctx_halos = n_ctx // SUBLANES

    def lat_tile(i):
        return jnp.minimum(i, n_lat_tiles - 1)

    def ctx_tile(i):
        return jnp.maximum(i - n_lat_tiles, 0)

    halo = (SUBLANES, d)
    tok = jax.ShapeDtypeStruct((t, D_A), BF16)
    tok2 = jax.ShapeDtypeStruct((2, t, D_A), BF16)
    ld2 = jax.ShapeDtypeStruct((2, t, D_A), F32)
    tok_spec = pl.BlockSpec((tm, D_A), lambda i: (i, 0))
    tok2_spec = pl.BlockSpec((2, tm, D_A), lambda i: (0, i, 0))
    kern = functools.partial(_prep_kernel, tm=tm, n_lat_tiles=n_lat_tiles, n_tiles=n_tiles)
    return pl.pallas_call(
        kern,
        grid=(n_tiles,),
        in_specs=[
            pl.BlockSpec((tm, d), lambda i: (lat_tile(i), 0)),
            pl.BlockSpec((tm, d), lambda i: (ctx_tile(i), 0)),
            pl.BlockSpec(halo, lambda i: (jnp.maximum(lat_tile(i) * hpt - 1, 0), 0)),
            pl.BlockSpec(halo, lambda i: (jnp.minimum((lat_tile(i) + 1) * hpt, lat_halos - 1), 0)),
            pl.BlockSpec(halo, lambda i: (jnp.maximum(ctx_tile(i) * hpt - 1, 0), 0)),
            pl.BlockSpec(halo, lambda i: (jnp.minimum((ctx_tile(i) + 1) * hpt, ctx_halos - 1), 0)),
            _mod_spec(d, n_lat_tiles), _const_spec(w_in.shape),
            _const_spec(mu.shape), _const_spec(w0.shape), _const_spec(w2bd.shape), _const_spec(a0.shape),
            _const_spec(a2bd.shape), _const_spec(g2.shape), _const_spec(k_k.shape), _const_spec(k_a.shape),
            _const_spec(r_k.shape), _const_spec(seg.shape),
        ],
        out_specs=[tok_spec, tok_spec, tok_spec, tok2_spec, tok2_spec, tok2_spec, tok_spec, tok_spec,
                   pl.BlockSpec((tm, D_B), lambda i: (i, 0)), pl.BlockSpec((tm, d), lambda i: (i, 0))],
        out_shape=[tok, tok, tok, ld2, tok2, tok2, tok, tok, jax.ShapeDtypeStruct((t, D_B), F32),
                   jax.ShapeDtypeStruct((t, d), x.dtype)],
        compiler_params=_params("arbitrary"),
        name="rwkv_prepare",
    )(x, ctx, x, x, ctx, ctx, mod, w_in, mu, w0, w2bd, a0, a2bd, g2, k_k, k_a, r_k, seg)


WKV_PAIRS_PER_STEP = 3


def _wkv_kernel(rf_ref, vf_ref, knf_ref, ldf_ref, kdf_ref, bdf_ref, rr_ref, vr_ref, knr_ref, ldr_ref, kdr_ref,
                bdr_ref, yf_ref, yr_ref, h_ref, lhs_s, add_s, pb_s, ends_s, dec_s, v_s, *, chunks_per_block,
                pairs_per_step):
    step = pl.program_id(1)
    cur = lax.rem(step, 2)
    prev = 1 - cur

    @pl.when(step == 0)
    def _():
        h_ref[...] = jnp.zeros_like(h_ref)
        for ref in (lhs_s, add_s, pb_s, ends_s, dec_s, v_s):
            ref[1] = jnp.zeros(ref.shape[1:], ref.dtype)

    c = CHUNK
    ri = lax.broadcasted_iota(jnp.int32, (c, c), 0)
    ci = lax.broadcasted_iota(jnp.int32, (c, c), 1)
    rp = lax.broadcasted_iota(jnp.int32, (c, LANES), 0)
    cp = lax.broadcasted_iota(jnp.int32, (c, LANES), 1) & (c - 1)
    eye_p = jnp.where(rp == cp, 1.0, 0.0)
    r2 = lax.broadcasted_iota(jnp.int32, (LANES, LANES), 0)
    c2 = lax.broadcasted_iota(jnp.int32, (LANES, LANES), 1)
    same_head = (r2 < HEAD) == (c2 < HEAD)
    head0 = lax.broadcasted_iota(jnp.int32, (1, LANES), 1) < HEAD

    def bdiag(x):
        zero = jnp.zeros_like(x)
        return jnp.concatenate([jnp.where(head0, x, zero), jnp.where(head0, zero, x)], axis=0)

    masks = {}
    for rev in (False, True):
        sgn = -1 if rev else 1
        order = (rp - cp) * sgn
        masks[rev] = (jnp.where((ri - ci) * sgn >= 0, 1.0, 0.0).astype(BF16), order > 0, order >= 0)

    chains = []
    for rev, refs, y_ref in ((False, (rf_ref, vf_ref, knf_ref, ldf_ref, kdf_ref, bdf_ref), yf_ref),
                             (True, (rr_ref, vr_ref, knr_ref, ldr_ref, kdr_ref, bdr_ref), yr_ref)):
        for q in range(pairs_per_step):
            chains.append((rev, refs, y_ref, slice(q * LANES, (q + 1) * LANES)))

    insts = []
    for ch, (rev, refs, y_ref, lanes) in enumerate(chains):
        for j in range(chunks_per_block):
            cj = chunks_per_block - 1 - j if rev else j
            insts.append(dict(idx=len(insts), ch=ch, j=j, rev=rev, refs=refs, lanes=lanes,
                              rows=slice(cj * c, (cj + 1) * c)))
    by_key = {(it["ch"], it["j"]): it for it in insts}

    states = [h_ref[ch] for ch in range(len(chains))]
    pending = {}

    def state_group(k):
        j, second = divmod(k, 2)
        for ch, (rev, refs, y_ref, lanes) in enumerate(chains):
            it = by_key[(ch, j)]
            idx = it["idx"]
            if not second:
                pending[ch] = _dot(lhs_s[prev, idx], states[ch].astype(BF16))
            else:
                m1 = pending.pop(ch)
                add = add_s[prev, idx]
                ub = (-m1[:c] - add[:c]).astype(BF16)
                y_ref[it["rows"], lanes] = m1[c:] + add[c:] + _dot(pb_s[prev, idx], bdiag(ub))
                upd = _dot(ends_s[prev, idx], jnp.concatenate([ub, v_s[prev, idx]], axis=0))
                states[ch] = states[ch] * dec_s[prev, idx] + jnp.where(same_head, upd, 0.0)

    n_state_groups = 2 * chunks_per_block
    emitted = [0]

    def interleave_state_group():
        if emitted[0] < n_state_groups:
            state_group(emitted[0])
            emitted[0] += 1

    for it in insts:
        ld = it["refs"][3][0, it["rows"], it["lanes"]]
        tri = masks[it["rev"]][0]
        ld_hi = ld.astype(BF16)
        rem = ld - ld_hi.astype(F32)
        ld_mid = rem.astype(BF16)
        ld_lo = (rem - ld_mid.astype(F32)).astype(BF16)
        part = _dot(tri, jnp.concatenate([ld_hi, ld_mid], axis=1))
        it["cum"] = part[:, :LANES] + part[:, LANES:] + _dot(tri, ld_lo)
    interleave_state_group()

    for it in insts:
        r_ref, v_ref, kn_ref, ld_ref, kd_ref, bd_ref = it["refs"]
        rows, lanes = it["rows"], it["lanes"]
        _, strict, incl = masks[it["rev"]]
        ld = ld_ref[0, rows, lanes]
        kd = kd_ref[0, rows, lanes].astype(F32)
        bd = bd_ref[0, rows, lanes].astype(F32)
        cum = it.pop("cum")
        total = jnp.sum(ld, axis=0, keepdims=True)
        inv = jnp.exp(-cum)
        to_end = jnp.exp(total - cum)
        kq = (kn_ref[rows, lanes].astype(F32) * jnp.exp(cum - ld)).astype(BF16)
        rq = (r_ref[rows, lanes].astype(F32) * jnp.exp(cum)).astype(BF16)
        g = _dot_nt(jnp.concatenate([kq, rq], axis=0),
                    jnp.concatenate([bdiag((bd * inv).astype(BF16)), bdiag((kd * inv).astype(BF16))], axis=0))
        it["q"] = jnp.where(strict, -g[:c, :LANES], 0.0)
        it["bm"] = jnp.where(strict, g[:c, LANES:], 0.0).astype(BF16)
        it["pk"] = jnp.where(incl, g[c:, LANES:], 0.0).astype(BF16)
        it["kq_bd"], it["rq"] = bdiag(kq), rq
        v = v_ref[rows, lanes]
        it["v_bd"] = bdiag(v)
        idx = it["idx"]
        pb_s[cur, idx] = jnp.where(incl, g[c:, :LANES], 0.0).astype(BF16)
        v_s[cur, idx] = v
        ends_s[cur, idx] = jnp.concatenate([bd * to_end, kd * to_end], axis=0).T.astype(BF16)
        dec_s[cur, idx] = jnp.broadcast_to(jnp.exp(jnp.sum(ld.T, axis=1, keepdims=True)), (LANES, LANES))
    interleave_state_group()

    for it in insts:
        q = it.pop("q")
        it["t_inv"] = eye_p + q
        qb = q.astype(BF16)
        it["pw"] = _dot(qb, bdiag(qb)).astype(BF16)
    interleave_state_group()
    n_levels = int(math.log2(c)) - 1
    for level in range(n_levels):
        for it in insts:
            pw = it["pw"]
            t_bd = bdiag(it["t_inv"].astype(BF16))
            if level < n_levels - 1:
                res = _dot(pw, jnp.concatenate([bdiag(pw), t_bd], axis=1))
                it["pw"] = res[:, :LANES].astype(BF16)
                it["t_inv"] = it["t_inv"] + res[:, LANES:]
            else:
                it["t_inv"] = it["t_inv"] + _dot(pw, t_bd)
        interleave_state_group()
    while emitted[0] < n_state_groups:
        interleave_state_group()
    for it in insts:
        it["bmv"] = _dot(it.pop("bm"), it["v_bd"]).astype(BF16)
    for it in insts:
        wu = _dot(it.pop("t_inv").astype(BF16), jnp.concatenate([it.pop("kq_bd"), bdiag(it.pop("bmv"))], axis=1))
        lhs_s[cur, it["idx"]] = jnp.concatenate([wu[:, :LANES].astype(BF16), it.pop("rq")], axis=0)
        it["ut"] = wu[:, LANES:]
    for it in insts:
        pkv = _dot(it.pop("pk"), it["v_bd"])
        add_s[cur, it["idx"]] = jnp.concatenate([it.pop("ut"), pkv], axis=0)
    for ch in range(len(chains)):
        h_ref[ch] = states[ch]


def _wkv_scan(r, v, kn, ld, kd, bd, *, n_lat_blocks):
    t, da = r.shape
    rows = TILE_ROWS
    n_blocks = t // rows
    pps = WKV_PAIRS_PER_STEP
    width = pps * LANES
    n_groups = da // width

    n_ctx_blocks = n_blocks - n_lat_blocks

    def fwd_blk(s):
        return jnp.where(s < n_ctx_blocks, n_lat_blocks + s, s - n_ctx_blocks)

    def rev_blk(s):
        return jnp.where(s < n_ctx_blocks, n_blocks - 1 - s, n_lat_blocks - 1 - (s - n_ctx_blocks))

    def cur(s):
        return jnp.minimum(s, n_blocks - 1)

    def prv(s):
        return jnp.maximum(s - 1, 0)

    fwd = pl.BlockSpec((rows, width), lambda g, s: (fwd_blk(cur(s)), g))
    bwd = pl.BlockSpec((rows, width), lambda g, s: (rev_blk(cur(s)), g))
    fwd_d = pl.BlockSpec((1, rows, width), lambda g, s: (0, fwd_blk(cur(s)), g))
    bwd_d = pl.BlockSpec((1, rows, width), lambda g, s: (1, rev_blk(cur(s)), g))
    y_fwd = pl.BlockSpec((rows, width), lambda g, s: (fwd_blk(prv(s)), g))
    y_bwd = pl.BlockSpec((rows, width), lambda g, s: (rev_blk(prv(s)), g))
    chunks = rows // CHUNK
    n_inst = 2 * pps * chunks
    kern = functools.partial(_wkv_kernel, chunks_per_block=chunks, pairs_per_step=pps)
    out = jax.ShapeDtypeStruct((t, da), F32)
    return pl.pallas_call(
        kern,
        grid=(n_groups, n_blocks + 1),
        in_specs=[fwd, fwd, fwd, fwd_d, fwd_d, fwd_d, bwd, bwd, bwd, bwd_d, bwd_d, bwd_d],
        out_specs=[y_fwd, y_bwd],
        out_shape=[out, out],
        scratch_shapes=[
            pltpu.VMEM((2 * pps, LANES, LANES), F32),
            pltpu.VMEM((2, n_inst, 2 * CHUNK, LANES), BF16),
            pltpu.VMEM((2, n_inst, 2 * CHUNK, LANES), F32),
            pltpu.VMEM((2, n_inst, CHUNK, LANES), BF16),
            pltpu.VMEM((2, n_inst, LANES, LANES), BF16),
            pltpu.VMEM((2, n_inst, LANES, LANES), F32),
            pltpu.VMEM((2, n_inst, CHUNK, LANES), BF16),
        ],
        compiler_params=_params("arbitrary", "arbitrary"),
        name="wkv_scan",
    )(r, v, kn, ld, kd, bd, r, v, kn, ld, kd, bd)


def _readout_kernel(yf_ref, yr_ref, bonus_ref, g_ref, pool_ref, x_ref, mod_ref, seg_ref, lnxg_ref, lnxb_ref,
                    poolw_ref, pools_ref, wout_ref, lng_ref, lnb_ref, o_ref):
    y = yf_ref[...] + yr_ref[...]
    seg = seg_ref[...]
    mu = _seg_sum(y, seg) * (1.0 / HEAD)
    yc = y - mu
    var = _seg_sum(yc * yc, seg) * (1.0 / HEAD)
    yn = yc * lax.rsqrt(var + GN_EPS) * lnxg_ref[...] + lnxb_ref[...]
    a = (yn + bonus_ref[...].astype(F32)) * g_ref[...].astype(F32)
    b = _dot(pool_ref[...].astype(BF16), poolw_ref[...]) * pools_ref[...]
    out = _dot(a.astype(BF16), wout_ref[:D_A, :]) + _dot(b.astype(BF16), wout_ref[D_A:, :])
    m = mod_ref[0]
    o_ref[...] = _layer_norm(ALPHA * x_ref[...] + m[2:3] * out, lng_ref[...], lnb_ref[...])


def _rwkv_readout(y_fwd, y_rev, bonus, g, pooled, x, mod, seg, lnx_g, lnx_b, pool_wbd, pool_scale, w_out, ln_g, ln_b,
                  *, n_lat_tiles):
    t, d = x.shape
    tm = TILE_ROWS
    tok = pl.BlockSpec((tm, D_A), lambda i: (i, 0))
    return pl.pallas_call(
        _readout_kernel,
        grid=(t // tm,),
        in_specs=[
            tok, tok, tok, tok,
            pl.BlockSpec((tm, D_B), lambda i: (i, 0)),
            pl.BlockSpec((tm, d), lambda i: (i, 0)),
            _mod_spec(d, n_lat_tiles),
            _const_spec(seg.shape), _const_spec(lnx_g.shape), _const_spec(lnx_b.shape),
            _const_spec(pool_wbd.shape), _const_spec(pool_scale.shape), _const_spec(w_out.shape),
            _const_spec(ln_g.shape), _const_spec(ln_b.shape),
        ],
        out_specs=pl.BlockSpec((tm, d), lambda i: (i, 0)),
        out_shape=jax.ShapeDtypeStruct((t, d), F32),
        compiler_params=_params("arbitrary"),
        name="rwkv_readout",
    )(y_fwd, y_rev, bonus, g, pooled, x, mod, seg, lnx_g, lnx_b, pool_wbd, pool_scale, w_out, ln_g, ln_b)


def _ffn_kernel(x_ref, mod_ref, w1_ref, w3_ref, w2_ref, lng_ref, lnb_ref, o_ref):
    m = mod_ref[0]
    x = x_ref[...]
    h = (x * (1.0 + m[4:5]) + m[3:4]).astype(BF16)
    a = _dot(h, w1_ref[...])
    b = _dot(h, w3_ref[...])
    act = (a * _sigmoid(a) * b).astype(BF16)
    out = _dot(act, w2_ref[...])
    o_ref[...] = _layer_norm(ALPHA * x + m[5:6] * out, lng_ref[...], lnb_ref[...])


def _ffn(x, mod, w1, w3, w2, ln_g, ln_b, *, tm, n_tiles, n_lat_tiles):
    d = x.shape[1]
    single = pl.Buffered(1)
    return pl.pallas_call(
        _ffn_kernel,
        grid=(n_tiles,),
        in_specs=[
            pl.BlockSpec((tm, d), lambda i: (i, 0)),
            _mod_spec(d, n_lat_tiles),
            pl.BlockSpec(w1.shape, lambda i: (0, 0), pipeline_mode=single),
            pl.BlockSpec(w3.shape, lambda i: (0, 0), pipeline_mode=single),
            pl.BlockSpec(w2.shape, lambda i: (0, 0), pipeline_mode=single),
            _const_spec(ln_g.shape), _const_spec(ln_b.shape),
        ],
        out_specs=pl.BlockSpec((tm, d), lambda i: (i, 0)),
        out_shape=jax.ShapeDtypeStruct((n_tiles * tm, d), F32),
        compiler_params=_params("arbitrary"),
        name="ffn",
    )(x, mod, w1, w3, w2, ln_g, ln_b)


ATTN_ROWS_PER_STEP = 16


def _attn_kernel(q_ref, k_ref, v_ref, bias_ref, o_ref, *, q_rows, n_rows, n_lat, n_ctx, kh):
    rb = pl.program_id(1)
    head0 = lax.broadcasted_iota(jnp.int32, (1, LANES), 1) < HEAD
    kc = k_ref[n_lat:n_lat + n_ctx, :]
    vc = v_ref[n_lat:n_lat + n_ctx, :]
    n_loc = kh * GRID_W
    q_all = q_ref[...] * jnp.asarray(HEAD ** -0.5, q_ref.dtype)
    zero = jnp.zeros((GRID_W, LANES), q_all.dtype)

    rows = []
    for j in range(q_rows):
        rr = rb * q_rows + j
        sr = jnp.clip(rr - kh // 2, 0, n_rows - kh)
        t_var = sr - rr + (NA_KH - 1)
        q = q_all[j * GRID_W:(j + 1) * GRID_W]
        q2 = jnp.concatenate([jnp.where(head0, q, zero), jnp.where(head0, zero, q)], axis=0)
        krows = pl.ds(pl.multiple_of(sr * GRID_W, GRID_W), n_loc)
        bias = jnp.concatenate([bias_ref[0, t_var], bias_ref[1, t_var]], axis=0)
        s_loc = _dot_nt(q2, k_ref[krows, :]) + bias
        s_ctx = _dot_nt(q2, kc)
        rows.append(dict(krows=krows, s_loc=s_loc, s_ctx=s_ctx))
    for it in rows:
        s_loc, s_ctx = it.pop("s_loc"), it.pop("s_ctx")
        m = jnp.maximum(jnp.max(s_loc, axis=-1, keepdims=True), jnp.max(s_ctx, axis=-1, keepdims=True))
        p_loc = jnp.exp(s_loc - m)
        p_ctx = jnp.exp(s_ctx - m)
        it["denom"] = jnp.sum(p_loc, axis=-1, keepdims=True) + jnp.sum(p_ctx, axis=-1, keepdims=True)
        it["p_loc"] = p_loc.astype(BF16)
        it["p_ctx"] = p_ctx.astype(BF16)
    for j, it in enumerate(rows):
        o2 = (_dot(it["p_loc"], v_ref[it["krows"], :]) + _dot(it["p_ctx"], vc)) / it["denom"]
        o_ref[j * GRID_W:(j + 1) * GRID_W, :] = jnp.where(head0, o2[:GRID_W], o2[GRID_W:]).astype(o_ref.dtype)


def _neighbourhood_attention(qkv, bias_tab, *, n_ctx, n_lat):
    t_all = qkv.shape[0]
    d = D_MODEL
    n_rows = n_lat // GRID_W
    kh = min(NA_KH, n_rows)
    q_rows = ATTN_ROWS_PER_STEP
    q_tile = q_rows * GRID_W
    n_pairs = d // LANES
    kern = functools.partial(_attn_kernel, q_rows=q_rows, n_rows=n_rows, n_lat=n_lat, n_ctx=n_ctx, kh=kh)
    n_var = bias_tab.shape[1]
    return pl.pallas_call(
        kern,
        grid=(n_pairs, n_lat // q_tile),
        in_specs=[
            pl.BlockSpec((q_tile, LANES), lambda p, b: (b, p)),
            pl.BlockSpec((t_all, LANES), lambda p, b: (0, n_pairs + p)),
            pl.BlockSpec((t_all, LANES), lambda p, b: (0, 2 * n_pairs + p)),
            pl.BlockSpec((2, n_var, GRID_W, kh * GRID_W), lambda p, b: (p, 0, 0, 0)),
        ],
        out_specs=pl.BlockSpec((q_tile, LANES), lambda p, b: (b, p)),
        out_shape=jax.ShapeDtypeStruct((n_lat, d), BF16),
        compiler_params=_params("arbitrary", "arbitrary"),
        name="neighbourhood_attention",
    )(qkv, qkv, qkv, bias_tab)


def _attention_bias_table(rpb, n_rows):
    kh = min(NA_KH, n_rows)
    cols = np.arange(GRID_W)
    col_start = np.clip(cols - NA_KW // 2, 0, GRID_W - NA_KW)
    in_win = (cols[None, :] >= col_start[:, None]) & (cols[None, :] < col_start[:, None] + NA_KW)
    col_off = cols[None, :] - cols[:, None] + (NA_KW - 1)
    onehot = (col_off[None] == np.arange(2 * NA_KW - 1)[:, None, None]) & in_win[None]
    dense = jnp.einsum('hrj,jck->hcrk', rpb, jnp.asarray(onehot, F32), precision=HIGHEST)
    dense = jnp.where(jnp.asarray(in_win)[None, :, None, :], dense, NEG_BIAS)
    tab = jnp.stack([dense[:, :, t:t + kh].reshape(rpb.shape[0], GRID_W, kh * GRID_W) for t in range(NA_KH)],
                    axis=1)
    return tab.astype(F32)


def _proj_ln_kernel(o_ref, x_ref, mod_ref, w_ref, lng_ref, lnb_ref, out_ref):
    m = mod_ref[0]
    y = _dot(o_ref[...], w_ref[...])
    out_ref[...] = _layer_norm(ALPHA * x_ref[...] + m[2:3] * y, lng_ref[...], lnb_ref[...])


def _proj_ln(o, x, mod, w, ln_g, ln_b, *, tm):
    t, d = o.shape
    return pl.pallas_call(
        _proj_ln_kernel,
        grid=(t // tm,),
        in_specs=[
            pl.BlockSpec((tm, d), lambda i: (i, 0)),
            pl.BlockSpec((tm, d), lambda i: (i, 0)),
            pl.BlockSpec((1, SUBLANES, d), lambda i: (1, 0, 0)),
            _const_spec(w.shape), _const_spec(ln_g.shape), _const_spec(ln_b.shape),
        ],
        out_specs=pl.BlockSpec((tm, d), lambda i: (i, 0)),
        out_shape=jax.ShapeDtypeStruct((t, d), F32),
        compiler_params=_params("arbitrary"),
        name="attn_proj_ln",
    )(o, x, mod, w, ln_g, ln_b)


def _block_diag2(w):
    z = jnp.zeros_like(w[0])
    return jnp.concatenate([jnp.concatenate([w[0], z], axis=1), jnp.concatenate([z, w[1]], axis=1)], axis=0)


def kernel(x, c, ctx, c_ctx, ada_w, ada_b, ln_g, ln_b, ffn_w1, ffn_w3, ffn_w2, ev_w_in, ev_shift_mu, ev_w0,
           ev_w2, ev_a0, ev_a2, ev_g2, ev_k_k, ev_k_a, ev_r_k, ev_lnx_g, ev_lnx_b, ev_pool_w, ev_pool_scale,
           ev_w_out, od_w_in, od_rpb, od_w_out):
    batch, n_lat, d = x.shape
    n_ctx = ctx.shape[1]
    assert batch == 1 and d == D_MODEL
    assert n_ctx % TILE_ROWS == 0 and n_lat % LAT_TILE_ROWS == 0 and n_lat % (ATTN_ROWS_PER_STEP * GRID_W) == 0
    n_lat_tiles = n_lat // TILE_ROWS

    cs = jnp.zeros((SUBLANES, d), F32).at[0].set(c_ctx).at[1].set(c[0])
    mod_all = _ada_modulation(cs, ada_w, ada_b)
    mod_all = mod_all[:, :2].reshape(DEPTH, 2, 6, d)
    mod_all = jnp.pad(mod_all, ((0, 0), (0, 0), (0, SUBLANES - 6), (0, 0)))

    mod = mod_all[0]
    seg = (jnp.arange(LANES)[:, None] // HEAD == jnp.arange(LANES)[None, :] // HEAD).astype(BF16)
    r, v, kn, ld, kd, bd, g, bonus, pooled, xa = _rwkv_prepare(
        x[0], ctx[0], mod, ev_w_in[0].astype(BF16), ev_shift_mu[0], ev_w0[0],
        _block_diag2(ev_w2[0]).astype(BF16), ev_a0[0], _block_diag2(ev_a2[0]).astype(BF16),
        ev_g2[0].astype(BF16), ev_k_k[0].reshape(1, D_A), ev_k_a[0].reshape(1, D_A),
        ev_r_k[0].reshape(1, D_A), seg)
    y_fwd, y_rev = _wkv_scan(r, v, kn, ld, kd, bd, n_lat_blocks=n_lat_tiles)
    pw = ev_pool_w[0]
    pool_wbd = jnp.zeros((D_B, D_B), F32)
    for gi in range(len(POOL_WINDOWS)):
        sl = slice(gi * POOL_GROUP, (gi + 1) * POOL_GROUP)
        pool_wbd = pool_wbd.at[sl, sl].set(pw[gi])
    xa = _rwkv_readout(y_fwd, y_rev, bonus, g, pooled, xa, mod, seg, ev_lnx_g[0].reshape(1, D_A),
                       ev_lnx_b[0].reshape(1, D_A), pool_wbd.astype(BF16), ev_pool_scale[0].reshape(1, D_B),
                       ev_w_out[0].astype(BF16), ln_g[0, 0].reshape(1, d), ln_b[0, 0].reshape(1, d),
                       n_lat_tiles=n_lat_tiles)
    xa = _ffn(xa, mod, ffn_w1[0].astype(BF16), ffn_w3[0].astype(BF16), ffn_w2[0].astype(BF16),
              ln_g[0, 1].reshape(1, d), ln_b[0, 1].reshape(1, d), tm=TILE_ROWS,
              n_tiles=(n_lat + n_ctx) // TILE_ROWS, n_lat_tiles=n_lat_tiles)

    mod = mod_all[1]
    qkv = _mod_matmul(xa, mod, od_w_in[0].astype(BF16), n_lat_tiles=n_lat_tiles, out_dtype=BF16)
    bias_tab = _attention_bias_table(od_rpb[0], n_lat // GRID_W)
    o = _neighbourhood_attention(qkv, bias_tab, n_ctx=n_ctx, n_lat=n_lat)
    xl = _proj_ln(o, xa, mod, od_w_out[0].astype(BF16), ln_g[1, 0].reshape(1, d), ln_b[1, 0].reshape(1, d),
                  tm=LAT_TILE_ROWS)
    xl = _ffn(xl, mod, ffn_w1[1].astype(BF16), ffn_w3[1].astype(BF16), ffn_w2[1].astype(BF16),
              ln_g[1, 1].reshape(1, d), ln_b[1, 1].reshape(1, d), tm=LAT_TILE_ROWS,
              n_tiles=n_lat // LAT_TILE_ROWS, n_lat_tiles=n_lat // LAT_TILE_ROWS)
    return xl[None]
```

```python
import functools
import math

import jax
import jax.numpy as jnp
import numpy as np
from jax import lax
from jax.experimental import pallas as pl
from jax.experimental.pallas import tpu as pltpu

F32 = jnp.float32
BF16 = jnp.bfloat16
HIGHEST = lax.Precision.HIGHEST

D_MODEL = 1024
DEPTH = 2
GRID_W = 64
ALPHA = (2 * DEPTH) ** 0.25
LN_EPS = 1e-6
HEAD = 64
D_A = 3 * D_MODEL // 4
H_A = D_A // HEAD
LORA_W = 64
LORA_A = 64
LORA_G = 128
GN_EPS = 64e-5
D_B = D_MODEL - D_A
POOL_WINDOWS = (2, 4, 8, 16)
POOL_GROUP = D_B // len(POOL_WINDOWS)
C_RW = 3 * D_A + 2 * LORA_W + 2 * LORA_A + LORA_G
D_IN_EVEN = C_RW + D_B
H_C = D_MODEL // HEAD
NA_KH = 8
NA_KW = 16

LANES = 128
SUBLANES = 8
VMEM_LIMIT_BYTES = 56 * 1024 * 1024

TILE_ROWS = 256
LAT_TILE_ROWS = 512
CHUNK = 64
NEG_BIAS = -1e30


def _sigmoid(x):
    return 1.0 / (1.0 + jnp.exp(-x))


def _layer_norm(z, g, b):
    mu = jnp.mean(z, axis=-1, keepdims=True)
    zc = z - mu
    var = jnp.mean(zc * zc, axis=-1, keepdims=True)
    return zc * lax.rsqrt(var + LN_EPS) * g + b


def _dot(a, b):
    return jnp.dot(a, b, preferred_element_type=F32)


def _dot_hi(a, b):
    return jnp.dot(a, b, precision=HIGHEST, preferred_element_type=F32)


def _dot_nt(a, b):
    return lax.dot_general(a, b, (((1,), (1,)), ((), ())), preferred_element_type=F32)


def _seg_sum(x, seg):
    xb = x.astype(BF16)
    return jnp.concatenate([_dot(xb[:, j:j + LANES], seg) for j in range(0, x.shape[1], LANES)], axis=1)


def _params(*sem):
    return pltpu.CompilerParams(dimension_semantics=sem, vmem_limit_bytes=VMEM_LIMIT_BYTES)


def _const_spec(shape):
    nd = len(shape)
    return pl.BlockSpec(shape, lambda *_: (0,) * nd)


def _mod_spec(d, n_lat_tiles):
    return pl.BlockSpec((1, SUBLANES, d), lambda i: ((i < n_lat_tiles).astype(jnp.int32), 0, 0))


def _ada_kernel(cs_ref, w_ref, b_ref, o_ref):
    s = cs_ref[...]
    s = s * _sigmoid(s)
    o_ref[0] = _dot_hi(s, w_ref[0]) + b_ref[0]


def _ada_modulation(cs, ada_w, ada_b):
    depth, d, n = ada_w.shape
    nb = 1536
    return pl.pallas_call(
        _ada_kernel,
        grid=(depth, n // nb),
        in_specs=[
            pl.BlockSpec((SUBLANES, d), lambda i, j: (0, 0)),
            pl.BlockSpec((1, d, nb), lambda i, j: (i, 0, j)),
            pl.BlockSpec((1, 1, nb), lambda i, j: (i, 0, j)),
        ],
        out_specs=pl.BlockSpec((1, SUBLANES, nb), lambda i, j: (i, 0, j)),
        out_shape=jax.ShapeDtypeStruct((depth, SUBLANES, n), F32),
        compiler_params=_params("arbitrary", "arbitrary"),
        name="ada_modulation",
    )(cs, ada_w, ada_b.reshape(depth, 1, n))


def _prep_kernel(x_ref, ctx_ref, xhp_ref, xhn_ref, chp_ref, chn_ref, mod_ref, w_ref, mu_ref, w0_ref, w2_ref,
                 a0_ref, a2_ref, g2_ref, kk_ref, ka_ref, rk_ref, seg_ref,
                 r_o, v_o, kn_o, ld_o, kd_o, bd_o, g_o, bonus_o, pool_o, xa_o, *, tm, n_lat_tiles, n_tiles):
    i = pl.program_id(0)
    is_lat = i < n_lat_tiles
    first = jnp.logical_or(i == 0, i == n_lat_tiles)
    last = jnp.logical_or(i == n_lat_tiles - 1, i == n_tiles - 1)
    row = lax.broadcasted_iota(jnp.int32, (tm, 1), 0)
    n_ext = tm + 2 * SUBLANES

    xt = jnp.where(is_lat, x_ref[...], ctx_ref[...])
    xa_o[...] = xt
    x_ext = jnp.concatenate([jnp.where(is_lat, xhp_ref[...], chp_ref[...]), xt,
                             jnp.where(is_lat, xhn_ref[...], chn_ref[...])], axis=0)
    m = mod_ref[0]
    h_ext = (x_ext * (1.0 + m[1:2]) + m[0:1]).astype(BF16)
    mu_prev = mu_ref[0:1]
    mu_next = mu_ref[1:2]
    mu_self = 1.0 - mu_prev - mu_next
    inner = slice(SUBLANES, SUBLANES + tm)

    def project(c0, c1):
        p = _dot(h_ext, w_ref[:, c0:c1])
        return jnp.concatenate([jnp.where(first, 0.0, p[:SUBLANES]), p[inner],
                                jnp.where(last, 0.0, p[SUBLANES + tm:])], axis=0)

    def shift_mix(p, c0, c1):
        return (p[inner] * mu_self[:, c0:c1] + pltpu.roll(p, 1, 0)[inner] * mu_prev[:, c0:c1]
                + pltpu.roll(p, n_ext - 1, 0)[inner] * mu_next[:, c0:c1])

    s1, s2, s3 = D_A, 2 * D_A, 3 * D_A
    s4 = s3 + 2 * LORA_W
    s5 = s4 + 2 * LORA_A
    lora = shift_mix(project(s3, C_RW), s3, C_RW)
    k_ext = project(s1, s2)
    lw = _dot(jnp.tanh(lora[:, :s4 - s3]).astype(BF16), w2_ref[...])
    la = _dot(lora[:, s4 - s3:s5 - s3].astype(BF16), a2_ref[...])
    g = _dot(_sigmoid(lora[:, s5 - s3:]).astype(BF16), g2_ref[...])
    g_o[...] = g.astype(g_o.dtype)

    r_ext = project(0, s1)
    k = shift_mix(k_ext, s1, s2)
    kr = k * kk_ref[...]
    ss = _seg_sum(kr * kr, seg_ref[...])
    kn = kr * lax.rsqrt(jnp.maximum(ss, 1e-24))
    kn_o[...] = kn.astype(kn_o.dtype)

    v_ext = project(s2, s3)
    decay_scale = math.exp(-0.5)
    kd_sum = jnp.zeros_like(k)
    for d in range(2):
        sl = slice(d * D_A, (d + 1) * D_A)
        ld_o[d] = -decay_scale * _sigmoid(w0_ref[d:d + 1] + lw[:, sl])
        a = _sigmoid(a0_ref[d:d + 1] + la[:, sl])
        kd = k * (1.0 + (a - 1.0) * ka_ref[...])
        kd_o[d] = kd.astype(kd_o.dtype)
        bd_o[d] = (kn * a).astype(bd_o.dtype)
        kd_sum = kd_sum + kd

    xe = project(C_RW, D_IN_EVEN)
    r = shift_mix(r_ext, 0, s1)
    r_o[...] = r.astype(r_o.dtype)
    coef = _seg_sum(r * kd_sum * rk_ref[...], seg_ref[...])
    v = shift_mix(v_ext, s2, s3)
    v_o[...] = v.astype(v_o.dtype)
    bonus_o[...] = (coef * v).astype(bonus_o.dtype)

    xp = xe[inner]

    def at_offset(a, s):
        return pltpu.roll(a, (-s) % n_ext, 0)

    sums = {}
    acc = xe + at_offset(xe, -1)
    sums[2] = acc
    half = 1
    for win in (4, 8, 16):
        acc = at_offset(acc, -half) + at_offset(acc, half)
        sums[win] = acc
        half *= 2

    in_ctx = i >= n_lat_tiles
    tile0 = jnp.where(in_ctx, n_lat_tiles, 0)
    seq_len = jnp.where(in_ctx, (n_tiles - n_lat_tiles) * tm, n_lat_tiles * tm)
    t_seq = (i - tile0) * tm + row
    grp = lax.broadcasted_iota(jnp.int32, (1, D_B), 1) // POOL_GROUP
    mean = jnp.zeros((tm, D_B), F32)
    for gi, win in enumerate(POOL_WINDOWS):
        cnt = jnp.minimum(t_seq + win // 2, seq_len) - jnp.maximum(t_seq - win // 2, 0)
        m = sums[win][SUBLANES:SUBLANES + tm] / cnt.astype(F32)
        mean = jnp.where(grp == gi, m, mean)
    pool_o[...] = mean - xp


def _rwkv_prepare(x, ctx, mod, w_in, mu, w0, w2bd, a0, a2bd, g2, k_k, k_a, r_k, seg):
    n_lat, d = x.shape
    n_ctx = ctx.shape[0]
    t = n_lat + n_ctx
    tm = TILE_ROWS
    n_tiles = t // tm
    n_lat_tiles = n_lat // tm
    hpt = tm // SUBLANES
    lat_halos = n_lat // SUBLANES
    ctx_halos = n_ctx // SUBLANES

    def lat_tile(i):
        return jnp.minimum(i, n_lat_tiles - 1)

    def ctx_tile(i):
        return jnp.maximum(i - n_lat_tiles, 0)

    halo = (SUBLANES, d)
    tok = jax.ShapeDtypeStruct((t, D_A), BF16)
    tok2 = jax.ShapeDtypeStruct((2, t, D_A), BF16)
    ld2 = jax.ShapeDtypeStruct((2, t, D_A), F32)
    tok_spec = pl.BlockSpec((tm, D_A), lambda i: (i, 0))
    tok2_spec = pl.BlockSpec((2, tm, D_A), lambda i: (0, i, 0))
    kern = functools.partial(_prep_kernel, tm=tm, n_lat_tiles=n_lat_tiles, n_tiles=n_tiles)
    return pl.pallas_call(
        kern,
        grid=(n_tiles,),
        in_specs=[
            pl.BlockSpec((tm, d), lambda i: (lat_tile(i), 0)),
            pl.BlockSpec((tm, d), lambda i: (ctx_tile(i), 0)),
            pl.BlockSpec(halo, lambda i: (jnp.maximum(lat_tile(i) * hpt - 1, 0), 0)),
            pl.BlockSpec(halo, lambda i: (jnp.minimum((lat_tile(i) + 1) * hpt, lat_halos - 1), 0)),
            pl.BlockSpec(halo, lambda i: (jnp.maximum(ctx_tile(i) * hpt - 1, 0), 0)),
            pl.BlockSpec(halo, lambda i: (jnp.minimum((ctx_tile(i) + 1) * hpt, ctx_halos - 1), 0)),
            _mod_spec(d, n_lat_tiles), _const_spec(w_in.shape),
            _const_spec(mu.shape), _const_spec(w0.shape), _const_spec(w2bd.shape), _const_spec(a0.shape),
            _const_spec(a2bd.shape), _const_spec(g2.shape), _const_spec(k_k.shape), _const_spec(k_a.shape),
            _const_spec(r_k.shape), _const_spec(seg.shape),
        ],
        out_specs=[tok_spec, tok_spec, tok_spec, tok2_spec, tok2_spec, tok2_spec, tok_spec, tok_spec,
                   pl.BlockSpec((tm, D_B), lambda i: (i, 0)), pl.BlockSpec((tm, d), lambda i: (i, 0))],
        out_shape=[tok, tok, tok, ld2, tok2, tok2, tok, tok, jax.ShapeDtypeStruct((t, D_B), F32),
                   jax.ShapeDtypeStruct((t, d), x.dtype)],
        compiler_params=_params("arbitrary"),
        name="rwkv_prepare",
    )(x, ctx, x, x, ctx, ctx, mod, w_in, mu, w0, w2bd, a0, a2bd, g2, k_k, k_a, r_k, seg)


WKV_PAIRS_PER_STEP = 3


def _wkv_kernel(rf_ref, vf_ref, knf_ref, ldf_ref, kdf_ref, bdf_ref, rr_ref, vr_ref, knr_ref, ldr_ref, kdr_ref,
                bdr_ref, yf_ref, yr_ref, h_ref, lhs_s, add_s, pb_s, ends_s, dec_s, v_s, *, chunks_per_block,
                pairs_per_step):
    step = pl.program_id(1)
    cur = lax.rem(step, 2)
    prev = 1 - cur

    @pl.when(step == 0)
    def _():
        h_ref[...] = jnp.zeros_like(h_ref)
        for ref in (lhs_s, add_s, pb_s, ends_s, dec_s, v_s):
            ref[1] = jnp.zeros(ref.shape[1:], ref.dtype)

    c = CHUNK
    ri = lax.broadcasted_iota(jnp.int32, (c, c), 0)
    ci = lax.broadcasted_iota(jnp.int32, (c, c), 1)
    rp = lax.broadcasted_iota(jnp.int32, (c, LANES), 0)
    cp = lax.broadcasted_iota(jnp.int32, (c, LANES), 1) & (c - 1)
    eye_p = jnp.where(rp == cp, 1.0, 0.0)
    r2 = lax.broadcasted_iota(jnp.int32, (LANES, LANES), 0)
    c2 = lax.broadcasted_iota(jnp.int32, (LANES, LANES), 1)
    same_head = (r2 < HEAD) == (c2 < HEAD)
    head0 = lax.broadcasted_iota(jnp.int32, (1, LANES), 1) < HEAD

    def bdiag(x):
        zero = jnp.zeros_like(x)
        return jnp.concatenate([jnp.where(head0, x, zero), jnp.where(head0, zero, x)], axis=0)

    masks = {}
    for rev in (False, True):
        sgn = -1 if rev else 1
        order = (rp - cp) * sgn
        masks[rev] = (jnp.where((ri - ci) * sgn >= 0, 1.0, 0.0).astype(BF16), order > 0, order >= 0)

    chains = []
    for rev, refs, y_ref in ((False, (rf_ref, vf_ref, knf_ref, ldf_ref, kdf_ref, bdf_ref), yf_ref),
                             (True, (rr_ref, vr_ref, knr_ref, ldr_ref, kdr_ref, bdr_ref), yr_ref)):
        for q in range(pairs_per_step):
            chains.append((rev, refs, y_ref, slice(q * LANES, (q + 1) * LANES)))

    insts = []
    for ch, (rev, refs, y_ref, lanes) in enumerate(chains):
        for j in range(chunks_per_block):
            cj = chunks_per_block - 1 - j if rev else j
            insts.append(dict(idx=len(insts), ch=ch, j=j, rev=rev, refs=refs, lanes=lanes,
                              rows=slice(cj * c, (cj + 1) * c)))
    by_key = {(it["ch"], it["j"]): it for it in insts}

    states = [h_ref[ch] for ch in range(len(chains))]
    pending = {}

    def state_group(k):
        j, second = divmod(k, 2)
        for ch, (rev, refs, y_ref, lanes) in enumerate(chains):
            it = by_key[(ch, j)]
            idx = it["idx"]
            if not second:
                pending[ch] = _dot(lhs_s[prev, idx], states[ch].astype(BF16))
            else:
                m1 = pending.pop(ch)
                add = add_s[prev, idx]
                ub = (-m1[:c] - add[:c]).astype(BF16)
                y_ref[it["rows"], lanes] = m1[c:] + add[c:] + _dot(pb_s[prev, idx], bdiag(ub))
                upd = _dot(ends_s[prev, idx], jnp.concatenate([ub, v_s[prev, idx]], axis=0))
                states[ch] = states[ch] * dec_s[prev, idx] + jnp.where(same_head, upd, 0.0)

    n_state_groups = 2 * chunks_per_block
    emitted = [0]

    def interleave_state_group():
        if emitted[0] < n_state_groups:
            state_group(emitted[0])
            emitted[0] += 1

    for it in insts:
        ld = it["refs"][3][0, it["rows"], it["lanes"]]
        tri = masks[it["rev"]][0]
        ld_hi = ld.astype(BF16)
        rem = ld - ld_hi.astype(F32)
        ld_mid = rem.astype(BF16)
        ld_lo = (rem - ld_mid.astype(F32)).astype(BF16)
        part = _dot(tri, jnp.concatenate([ld_hi, ld_mid], axis=1))
        it["cum"] = part[:, :LANES] + part[:, LANES:] + _dot(tri, ld_lo)
    interleave_state_group()

    for it in insts:
        r_ref, v_ref, kn_ref, ld_ref, kd_ref, bd_ref = it["refs"]
        rows, lanes = it["rows"], it["lanes"]
        _, strict, incl = masks[it["rev"]]
        ld = ld_ref[0, rows, lanes]
        kd = kd_ref[0, rows, lanes].astype(F32)
        bd = bd_ref[0, rows, lanes].astype(F32)
        cum = it.pop("cum")
        total = jnp.sum(ld, axis=0, keepdims=True)
        inv = jnp.exp(-cum)
        to_end = inv * jnp.exp(total)
        kq = (kn_ref[rows, lanes].astype(F32) * jnp.exp(cum - ld)).astype(BF16)
        rq = (r_ref[rows, lanes].astype(F32) * jnp.exp(cum)).astype(BF16)
        g = _dot_nt(jnp.concatenate([kq, rq], axis=0),
                    jnp.concatenate([bdiag((bd * inv).astype(BF16)), bdiag((kd * inv).astype(BF16))], axis=0))
        it["q"] = jnp.where(strict, -g[:c, :LANES], 0.0)
        it["bm"] = jnp.where(strict, g[:c, LANES:], 0.0).astype(BF16)
        it["pk"] = jnp.where(incl, g[c:, LANES:], 0.0).astype(BF16)
        it["kq_bd"], it["rq"] = bdiag(kq), rq
        v = v_ref[rows, lanes]
        it["v_bd"] = bdiag(v)
        idx = it["idx"]
        pb_s[cur, idx] = jnp.where(incl, g[c:, :LANES], 0.0).astype(BF16)
        v_s[cur, idx] = v
        ends_s[cur, idx] = jnp.concatenate([bd * to_end, kd * to_end], axis=0).T.astype(BF16)
        dec_s[cur, idx] = jnp.broadcast_to(jnp.exp(jnp.sum(ld.T, axis=1, keepdims=True)), (LANES, LANES))
    interleave_state_group()

    for it in insts:
        q = it.pop("q")
        it["t_inv"] = eye_p + q
        qb = q.astype(BF16)
        it["pw"] = _dot(qb, bdiag(qb)).astype(BF16)
    interleave_state_group()
    n_levels = int(math.log2(c)) - 1
    for level in range(n_levels):
        for it in insts:
            pw = it["pw"]
            t_bd = bdiag(it["t_inv"].astype(BF16))
            if level < n_levels - 1:
                res = _dot(pw, jnp.concatenate([bdiag(pw), t_bd], axis=1))
                it["pw"] = res[:, :LANES].astype(BF16)
                it["t_inv"] = it["t_inv"] + res[:, LANES:]
            else:
                it["t_inv"] = it["t_inv"] + _dot(pw, t_bd)
        interleave_state_group()
    while emitted[0] < n_state_groups:
        interleave_state_group()
    for it in insts:
        it["bmv"] = _dot(it.pop("bm"), it["v_bd"]).astype(BF16)
    for it in insts:
        wu = _dot(it.pop("t_inv").astype(BF16), jnp.concatenate([it.pop("kq_bd"), bdiag(it.pop("bmv"))], axis=1))
        lhs_s[cur, it["idx"]] = jnp.concatenate([wu[:, :LANES].astype(BF16), it.pop("rq")], axis=0)
        it["ut"] = wu[:, LANES:]
    for it in insts:
        pkv = _dot(it.pop("pk"), it["v_bd"])
        add_s[cur, it["idx"]] = jnp.concatenate([it.pop("ut"), pkv], axis=0)
    for ch in range(len(chains)):
        h_ref[ch] = states[ch]


def _wkv_scan(r, v, kn, ld, kd, bd, *, n_lat_blocks):
    t, da = r.shape
    rows = TILE_ROWS
    n_blocks = t // rows
    pps = WKV_PAIRS_PER_STEP
    width = pps * LANES
    n_groups = da // width

    n_ctx_blocks = n_blocks - n_lat_blocks

    def fwd_blk(s):
        return jnp.where(s < n_ctx_blocks, n_lat_blocks + s, s - n_ctx_blocks)

    def rev_blk(s):
        return jnp.where(s < n_ctx_blocks, n_blocks - 1 - s, n_lat_blocks - 1 - (s - n_ctx_blocks))

    def cur(s):
        return jnp.minimum(s, n_blocks - 1)

    def prv(s):
        return jnp.maximum(s - 1, 0)

    fwd = pl.BlockSpec((rows, width), lambda g, s: (fwd_blk(cur(s)), g))
    bwd = pl.BlockSpec((rows, width), lambda g, s: (rev_blk(cur(s)), g))
    fwd_d = pl.BlockSpec((1, rows, width), lambda g, s: (0, fwd_blk(cur(s)), g))
    bwd_d = pl.BlockSpec((1, rows, width), lambda g, s: (1, rev_blk(cur(s)), g))
    y_fwd = pl.BlockSpec((rows, width), lambda g, s: (fwd_blk(prv(s)), g))
    y_bwd = pl.BlockSpec((rows, width), lambda g, s: (rev_blk(prv(s)), g))
    chunks = rows // CHUNK
    n_inst = 2 * pps * chunks
    kern = functools.partial(_wkv_kernel, chunks_per_block=chunks, pairs_per_step=pps)
    out = jax.ShapeDtypeStruct((t, da), F32)
    return pl.pallas_call(
        kern,
        grid=(n_groups, n_blocks + 1),
        in_specs=[fwd, fwd, fwd, fwd_d, fwd_d, fwd_d, bwd, bwd, bwd, bwd_d, bwd_d, bwd_d],
        out_specs=[y_fwd, y_bwd],
        out_shape=[out, out],
        scratch_shapes=[
            pltpu.VMEM((2 * pps, LANES, LANES), F32),
            pltpu.VMEM((2, n_inst, 2 * CHUNK, LANES), BF16),
            pltpu.VMEM((2, n_inst, 2 * CHUNK, LANES), F32),
            pltpu.VMEM((2, n_inst, CHUNK, LANES), BF16),
            pltpu.VMEM((2, n_inst, LANES, LANES), BF16),
            pltpu.VMEM((2, n_inst, LANES, LANES), F32),
            pltpu.VMEM((2, n_inst, CHUNK, LANES), BF16),
        ],
        compiler_params=_params("arbitrary", "arbitrary"),
        name="wkv_scan",
    )(r, v, kn, ld, kd, bd, r, v, kn, ld, kd, bd)


def _ffn_sublayer(x, m, w1_ref, w3_ref, w2_ref, lng_ref, lnb_ref):
    h = (x * (1.0 + m[4:5]) + m[3:4]).astype(BF16)
    a = _dot(h, w1_ref[...])
    b = _dot(h, w3_ref[...])
    act = (a * _sigmoid(a) * b).astype(BF16)
    out = _dot(act, w2_ref[...])
    return _layer_norm(ALPHA * x + m[5:6] * out, lng_ref[...], lnb_ref[...])


def _readout_kernel(yf_ref, yr_ref, bonus_ref, g_ref, pool_ref, x_ref, mod_ref, seg_ref, lnxg_ref, lnxb_ref,
                    poolw_ref, pools_ref, wout_ref, lng_ref, lnb_ref, w1_ref, w3_ref, w2_ref, lng2_ref, lnb2_ref,
                    mod_next_ref, w_next_ref, o_ref, proj_ref):
    y = yf_ref[...] + yr_ref[...]
    seg = seg_ref[...]
    mu = _seg_sum(y, seg) * (1.0 / HEAD)
    yc = y - mu
    var = _seg_sum(yc * yc, seg) * (1.0 / HEAD)
    yn = yc * lax.rsqrt(var + GN_EPS) * lnxg_ref[...] + lnxb_ref[...]
    a = (yn + bonus_ref[...].astype(F32)) * g_ref[...].astype(F32)
    b = _dot(pool_ref[...].astype(BF16), poolw_ref[...]) * pools_ref[...]
    out = _dot(a.astype(BF16), wout_ref[:D_A, :]) + _dot(b.astype(BF16), wout_ref[D_A:, :])
    m = mod_ref[0]
    x1 = _layer_norm(ALPHA * x_ref[...] + m[2:3] * out, lng_ref[...], lnb_ref[...])
    x2 = _ffn_sublayer(x1, m, w1_ref, w3_ref, w2_ref, lng2_ref, lnb2_ref)
    o_ref[...] = x2
    mn = mod_next_ref[0]
    proj_ref[...] = _dot((x2 * (1.0 + mn[1:2]) + mn[0:1]).astype(BF16), w_next_ref[...]).astype(proj_ref.dtype)


def _single_spec(shape):
    nd = len(shape)
    return pl.BlockSpec(shape, lambda *_: (0,) * nd, pipeline_mode=pl.Buffered(1))


def _rwkv_readout_ffn(y_fwd, y_rev, bonus, g, pooled, x, mod, seg, lnx_g, lnx_b, pool_wbd, pool_scale, w_out, ln_g,
                      ln_b, w1, w3, w2, ln_g2, ln_b2, mod_next, w_next, *, n_lat_tiles):
    t, d = x.shape
    tm = TILE_ROWS
    tok = pl.BlockSpec((tm, D_A), lambda i: (i, 0))
    return pl.pallas_call(
        _readout_kernel,
        grid=(t // tm,),
        in_specs=[
            tok, tok, tok, tok,
            pl.BlockSpec((tm, D_B), lambda i: (i, 0)),
            pl.BlockSpec((tm, d), lambda i: (i, 0)),
            _mod_spec(d, n_lat_tiles),
            _const_spec(seg.shape), _const_spec(lnx_g.shape), _const_spec(lnx_b.shape),
            _const_spec(pool_wbd.shape), _const_spec(pool_scale.shape), _const_spec(w_out.shape),
            _const_spec(ln_g.shape), _const_spec(ln_b.shape),
            _single_spec(w1.shape), _single_spec(w3.shape), _single_spec(w2.shape),
            _const_spec(ln_g2.shape), _const_spec(ln_b2.shape),
            _mod_spec(d, n_lat_tiles), _single_spec(w_next.shape),
        ],
        out_specs=[pl.BlockSpec((tm, d), lambda i: (i, 0)), pl.BlockSpec((tm, w_next.shape[1]), lambda i: (i, 0))],
        out_shape=[jax.ShapeDtypeStruct((t, d), F32), jax.ShapeDtypeStruct((t, w_next.shape[1]), BF16)],
        compiler_params=_params("arbitrary"),
        name="rwkv_readout_ffn",
    )(y_fwd, y_rev, bonus, g, pooled, x, mod, seg, lnx_g, lnx_b, pool_wbd, pool_scale, w_out, ln_g, ln_b,
      w1, w3, w2, ln_g2, ln_b2, mod_next, w_next)


ATTN_ROWS_PER_STEP = 16


def _attn_kernel(q_ref, k_ref, v_ref, bias_ref, o_ref, *, q_rows, n_rows, n_lat, n_ctx, kh):
    rb = pl.program_id(1)
    head0 = lax.broadcasted_iota(jnp.int32, (1, LANES), 1) < HEAD
    kc = k_ref[n_lat:n_lat + n_ctx, :]
    vc = v_ref[n_lat:n_lat + n_ctx, :]
    n_loc = kh * GRID_W
    q_all = q_ref[...] * jnp.asarray(HEAD ** -0.5, q_ref.dtype)
    zero = jnp.zeros((GRID_W, LANES), q_all.dtype)

    rows = []
    for j in range(q_rows):
        rr = rb * q_rows + j
        sr = jnp.clip(rr - kh // 2, 0, n_rows - kh)
        t_var = sr - rr + (NA_KH - 1)
        q = q_all[j * GRID_W:(j + 1) * GRID_W]
        q2 = jnp.concatenate([jnp.where(head0, q, zero), jnp.where(head0, zero, q)], axis=0)
        krows = pl.ds(pl.multiple_of(sr * GRID_W, GRID_W), n_loc)
        bias = jnp.concatenate([bias_ref[0, t_var], bias_ref[1, t_var]], axis=0)
        s_loc = _dot_nt(q2, k_ref[krows, :]) + bias
        s_ctx = _dot_nt(q2, kc)
        rows.append(dict(krows=krows, s_loc=s_loc, s_ctx=s_ctx))
    for it in rows:
        s_loc, s_ctx = it.pop("s_loc"), it.pop("s_ctx")
        m = jnp.maximum(jnp.max(s_loc, axis=-1, keepdims=True), jnp.max(s_ctx, axis=-1, keepdims=True))
        p_loc = jnp.exp(s_loc - m)
        p_ctx = jnp.exp(s_ctx - m)
        it["denom"] = jnp.sum(p_loc, axis=-1, keepdims=True) + jnp.sum(p_ctx, axis=-1, keepdims=True)
        it["p_loc"] = p_loc.astype(BF16)
        it["p_ctx"] = p_ctx.astype(BF16)
    for j, it in enumerate(rows):
        o2 = (_dot(it["p_loc"], v_ref[it["krows"], :]) + _dot(it["p_ctx"], vc)) / it["denom"]
        o_ref[j * GRID_W:(j + 1) * GRID_W, :] = jnp.where(head0, o2[:GRID_W], o2[GRID_W:]).astype(o_ref.dtype)


def _neighbourhood_attention(qkv, bias_tab, *, n_ctx, n_lat):
    t_all = qkv.shape[0]
    d = D_MODEL
    n_rows = n_lat // GRID_W
    kh = min(NA_KH, n_rows)
    q_rows = ATTN_ROWS_PER_STEP
    q_tile = q_rows * GRID_W
    n_pairs = d // LANES
    kern = functools.partial(_attn_kernel, q_rows=q_rows, n_rows=n_rows, n_lat=n_lat, n_ctx=n_ctx, kh=kh)
    n_var = bias_tab.shape[1]
    return pl.pallas_call(
        kern,
        grid=(n_pairs, n_lat // q_tile),
        in_specs=[
            pl.BlockSpec((q_tile, LANES), lambda p, b: (b, p)),
            pl.BlockSpec((t_all, LANES), lambda p, b: (0, n_pairs + p)),
            pl.BlockSpec((t_all, LANES), lambda p, b: (0, 2 * n_pairs + p)),
            pl.BlockSpec((2, n_var, GRID_W, kh * GRID_W), lambda p, b: (p, 0, 0, 0)),
        ],
        out_specs=pl.BlockSpec((q_tile, LANES), lambda p, b: (b, p)),
        out_shape=jax.ShapeDtypeStruct((n_lat, d), BF16),
        compiler_params=_params("arbitrary", "arbitrary"),
        name="neighbourhood_attention",
    )(qkv, qkv, qkv, bias_tab)


def _attention_bias_table(rpb, n_rows):
    kh = min(NA_KH, n_rows)
    cols = np.arange(GRID_W)
    col_start = np.clip(cols - NA_KW // 2, 0, GRID_W - NA_KW)
    in_win = (cols[None, :] >= col_start[:, None]) & (cols[None, :] < col_start[:, None] + NA_KW)
    col_off = cols[None, :] - cols[:, None] + (NA_KW - 1)
    onehot = (col_off[None] == np.arange(2 * NA_KW - 1)[:, None, None]) & in_win[None]
    dense = jnp.einsum('hrj,jck->hcrk', rpb, jnp.asarray(onehot, F32), precision=HIGHEST)
    dense = jnp.where(jnp.asarray(in_win)[None, :, None, :], dense, NEG_BIAS)
    tab = jnp.stack([dense[:, :, t:t + kh].reshape(rpb.shape[0], GRID_W, kh * GRID_W) for t in range(NA_KH)],
                    axis=1)
    return tab.astype(F32)


def _proj_ffn_kernel(o_ref, x_ref, mod_ref, w_ref, lng_ref, lnb_ref, w1_ref, w3_ref, w2_ref, lng2_ref, lnb2_ref,
                     out_ref):
    m = mod_ref[0]
    y = _dot(o_ref[...], w_ref[...])
    x1 = _layer_norm(ALPHA * x_ref[...] + m[2:3] * y, lng_ref[...], lnb_ref[...])
    out_ref[...] = _ffn_sublayer(x1, m, w1_ref, w3_ref, w2_ref, lng2_ref, lnb2_ref)


def _attn_proj_ffn(o, x, mod, w, ln_g, ln_b, w1, w3, w2, ln_g2, ln_b2, *, tm):
    t, d = o.shape
    return pl.pallas_call(
        _proj_ffn_kernel,
        grid=(t // tm,),
        in_specs=[
            pl.BlockSpec((tm, d), lambda i: (i, 0)),
            pl.BlockSpec((tm, d), lambda i: (i, 0)),
            pl.BlockSpec((1, SUBLANES, d), lambda i: (1, 0, 0)),
            _single_spec(w.shape), _const_spec(ln_g.shape), _const_spec(ln_b.shape),
            _single_spec(w1.shape), _single_spec(w3.shape), _single_spec(w2.shape),
            _const_spec(ln_g2.shape), _const_spec(ln_b2.shape),
        ],
        out_specs=pl.BlockSpec((tm, d), lambda i: (i, 0)),
        out_shape=jax.ShapeDtypeStruct((t, d), F32),
        compiler_params=_params("arbitrary"),
        name="attn_proj_ffn",
    )(o, x, mod, w, ln_g, ln_b, w1, w3, w2, ln_g2, ln_b2)


def _block_diag2(w):
    z = jnp.zeros_like(w[0])
    return jnp.concatenate([jnp.concatenate([w[0], z], axis=1), jnp.concatenate([z, w[1]], axis=1)], axis=0)


def kernel(x, c, ctx, c_ctx, ada_w, ada_b, ln_g, ln_b, ffn_w1, ffn_w3, ffn_w2, ev_w_in, ev_shift_mu, ev_w0,
           ev_w2, ev_a0, ev_a2, ev_g2, ev_k_k, ev_k_a, ev_r_k, ev_lnx_g, ev_lnx_b, ev_pool_w, ev_pool_scale,
           ev_w_out, od_w_in, od_rpb, od_w_out):
    batch, n_lat, d = x.shape
    n_ctx = ctx.shape[1]
    assert batch == 1 and d == D_MODEL
    assert n_ctx % TILE_ROWS == 0 and n_lat % LAT_TILE_ROWS == 0 and n_lat % (ATTN_ROWS_PER_STEP * GRID_W) == 0
    n_lat_tiles = n_lat // TILE_ROWS

    cs = jnp.zeros((SUBLANES, d), F32).at[0].set(c_ctx).at[1].set(c[0])
    mod_all = _ada_modulation(cs, ada_w, ada_b)
    mod_all = mod_all[:, :2].reshape(DEPTH, 2, 6, d)
    mod_all = jnp.pad(mod_all, ((0, 0), (0, 0), (0, SUBLANES - 6), (0, 0)))

    mod = mod_all[0]
    seg = (jnp.arange(LANES)[:, None] // HEAD == jnp.arange(LANES)[None, :] // HEAD).astype(BF16)
    r, v, kn, ld, kd, bd, g, bonus, pooled, xa = _rwkv_prepare(
        x[0], ctx[0], mod, ev_w_in[0].astype(BF16), ev_shift_mu[0], ev_w0[0],
        _block_diag2(ev_w2[0]).astype(BF16), ev_a0[0], _block_diag2(ev_a2[0]).astype(BF16),
        ev_g2[0].astype(BF16), ev_k_k[0].reshape(1, D_A), ev_k_a[0].reshape(1, D_A),
        ev_r_k[0].reshape(1, D_A), seg)
    y_fwd, y_rev = _wkv_scan(r, v, kn, ld, kd, bd, n_lat_blocks=n_lat_tiles)
    pw = ev_pool_w[0]
    pool_wbd = jnp.zeros((D_B, D_B), F32)
    for gi in range(len(POOL_WINDOWS)):
        sl = slice(gi * POOL_GROUP, (gi + 1) * POOL_GROUP)
        pool_wbd = pool_wbd.at[sl, sl].set(pw[gi])
    xa, qkv = _rwkv_readout_ffn(
        y_fwd, y_rev, bonus, g, pooled, xa, mod, seg, ev_lnx_g[0].reshape(1, D_A), ev_lnx_b[0].reshape(1, D_A),
        pool_wbd.astype(BF16), ev_pool_scale[0].reshape(1, D_B), ev_w_out[0].astype(BF16),
        ln_g[0, 0].reshape(1, d), ln_b[0, 0].reshape(1, d),
        ffn_w1[0].astype(BF16), ffn_w3[0].astype(BF16), ffn_w2[0].astype(BF16),
        ln_g[0, 1].reshape(1, d), ln_b[0, 1].reshape(1, d), mod_all[1], od_w_in[0].astype(BF16),
        n_lat_tiles=n_lat_tiles)

    mod = mod_all[1]
    bias_tab = _attention_bias_table(od_rpb[0], n_lat // GRID_W)
    o = _neighbourhood_attention(qkv, bias_tab, n_ctx=n_ctx, n_lat=n_lat)
    xl = _attn_proj_ffn(
        o, xa, mod, od_w_out[0].astype(BF16), ln_g[1, 0].reshape(1, d), ln_b[1, 0].reshape(1, d),
        ffn_w1[1].astype(BF16), ffn_w3[1].astype(BF16), ffn_w2[1].astype(BF16),
        ln_g[1, 1].reshape(1, d), ln_b[1, 1].reshape(1, d), tm=LAT_TILE_ROWS)
    return xl[None]
```

```python
import functools
import math

import jax
import jax.numpy as jnp
import numpy as np
from jax import lax
from jax.experimental import pallas as pl
from jax.experimental.pallas import tpu as pltpu

F32 = jnp.float32
BF16 = jnp.bfloat16
HIGHEST = lax.Precision.HIGHEST

D_MODEL = 1024
DEPTH = 2
GRID_W = 64
ALPHA = (2 * DEPTH) ** 0.25
LN_EPS = 1e-6
HEAD = 64
D_A = 3 * D_MODEL // 4
H_A = D_A // HEAD
LORA_W = 64
LORA_A = 64
LORA_G = 128
GN_EPS = 64e-5
D_B = D_MODEL - D_A
POOL_WINDOWS = (2, 4, 8, 16)
POOL_GROUP = D_B // len(POOL_WINDOWS)
C_RW = 3 * D_A + 2 * LORA_W + 2 * LORA_A + LORA_G
D_IN_EVEN = C_RW + D_B
H_C = D_MODEL // HEAD
NA_KH = 8
NA_KW = 16

LANES = 128
SUBLANES = 8
VMEM_LIMIT_BYTES = 56 * 1024 * 1024

TILE_ROWS = 256
LAT_TILE_ROWS = 512
CHUNK = 64
NEG_BIAS = -1e30


def _sigmoid(x):
    return 1.0 / (1.0 + jnp.exp(-x))


def _layer_norm(z, g, b):
    mu = jnp.mean(z, axis=-1, keepdims=True)
    zc = z - mu
    var = jnp.mean(zc * zc, axis=-1, keepdims=True)
    return zc * lax.rsqrt(var + LN_EPS) * g + b


def _dot(a, b):
    return jnp.dot(a, b, preferred_element_type=F32)


def _dot_hi(a, b):
    return jnp.dot(a, b, precision=HIGHEST, preferred_element_type=F32)


def _dot_nt(a, b):
    return lax.dot_general(a, b, (((1,), (1,)), ((), ())), preferred_element_type=F32)


def _seg_sum(x, seg):
    xb = x.astype(BF16)
    return jnp.concatenate([_dot(xb[:, j:j + LANES], seg) for j in range(0, x.shape[1], LANES)], axis=1)


def _params(*sem):
    return pltpu.CompilerParams(dimension_semantics=sem, vmem_limit_bytes=VMEM_LIMIT_BYTES)


def _const_spec(shape):
    nd = len(shape)
    return pl.BlockSpec(shape, lambda *_: (0,) * nd)


def _mod_spec(d, n_lat_tiles):
    return pl.BlockSpec((1, SUBLANES, d), lambda i: ((i < n_lat_tiles).astype(jnp.int32), 0, 0))


def _ada_kernel(cs_ref, w_ref, b_ref, o_ref):
    s = cs_ref[...]
    s = s * _sigmoid(s)
    o_ref[0] = _dot_hi(s, w_ref[0]) + b_ref[0]


def _ada_modulation(cs, ada_w, ada_b):
    depth, d, n = ada_w.shape
    nb = 1536
    return pl.pallas_call(
        _ada_kernel,
        grid=(depth, n // nb),
        in_specs=[
            pl.BlockSpec((SUBLANES, d), lambda i, j: (0, 0)),
            pl.BlockSpec((1, d, nb), lambda i, j: (i, 0, j)),
            pl.BlockSpec((1, 1, nb), lambda i, j: (i, 0, j)),
        ],
        out_specs=pl.BlockSpec((1, SUBLANES, nb), lambda i, j: (i, 0, j)),
        out_shape=jax.ShapeDtypeStruct((depth, SUBLANES, n), F32),
        compiler_params=_params("arbitrary", "arbitrary"),
        name="ada_modulation",
    )(cs, ada_w, ada_b.reshape(depth, 1, n))


def _prep_kernel(x_ref, ctx_ref, xhp_ref, xhn_ref, chp_ref, chn_ref, mod_ref, w_ref, mu_ref, w0_ref, w2_ref,
                 a0_ref, a2_ref, g2_ref, kk_ref, ka_ref, rk_ref, seg_ref,
                 r_o, v_o, kn_o, ld_o, kd_o, bd_o, g_o, bonus_o, pool_o, xa_o, *, tm, n_lat_tiles, n_tiles):
    i = pl.program_id(0)
    is_lat = i < n_lat_tiles
    first = jnp.logical_or(i == 0, i == n_lat_tiles)
    last = jnp.logical_or(i == n_lat_tiles - 1, i == n_tiles - 1)
    row = lax.broadcasted_iota(jnp.int32, (tm, 1), 0)
    n_ext = tm + 2 * SUBLANES

    xt = jnp.where(is_lat, x_ref[...], ctx_ref[...])
    xa_o[...] = xt
    x_ext = jnp.concatenate([jnp.where(is_lat, xhp_ref[...], chp_ref[...]), xt,
                             jnp.where(is_lat, xhn_ref[...], chn_ref[...])], axis=0)
    m = mod_ref[0]
    h_ext = (x_ext * (1.0 + m[1:2]) + m[0:1]).astype(BF16)
    mu_prev = mu_ref[0:1]
    mu_next = mu_ref[1:2]
    mu_self = 1.0 - mu_prev - mu_next
    inner = slice(SUBLANES, SUBLANES + tm)

    def project(c0, c1):
        p = _dot(h_ext, w_ref[:, c0:c1])
        return jnp.concatenate([jnp.where(first, 0.0, p[:SUBLANES]), p[inner],
                                jnp.where(last, 0.0, p[SUBLANES + tm:])], axis=0)

    def shift_mix(p, c0, c1):
        return (p[inner] * mu_self[:, c0:c1] + pltpu.roll(p, 1, 0)[inner] * mu_prev[:, c0:c1]
                + pltpu.roll(p, n_ext - 1, 0)[inner] * mu_next[:, c0:c1])

    s1, s2, s3 = D_A, 2 * D_A, 3 * D_A
    s4 = s3 + 2 * LORA_W
    s5 = s4 + 2 * LORA_A
    lora = shift_mix(project(s3, C_RW), s3, C_RW)
    k_ext = project(s1, s2)
    lw = _dot(jnp.tanh(lora[:, :s4 - s3]).astype(BF16), w2_ref[...])
    la = _dot(lora[:, s4 - s3:s5 - s3].astype(BF16), a2_ref[...])
    g = _dot(_sigmoid(lora[:, s5 - s3:]).astype(BF16), g2_ref[...])
    g_o[...] = g.astype(g_o.dtype)

    r_ext = project(0, s1)
    k = shift_mix(k_ext, s1, s2)
    kr = k * kk_ref[...]
    ss = _seg_sum(kr * kr, seg_ref[...])
    kn = kr * lax.rsqrt(jnp.maximum(ss, 1e-24))
    kn_o[...] = kn.astype(kn_o.dtype)

    v_ext = project(s2, s3)
    decay_scale = math.exp(-0.5)
    kd_sum = jnp.zeros_like(k)
    for d in range(2):
        sl = slice(d * D_A, (d + 1) * D_A)
        ld_o[d] = -decay_scale * _sigmoid(w0_ref[d:d + 1] + lw[:, sl])
        a = _sigmoid(a0_ref[d:d + 1] + la[:, sl])
        kd = k * (1.0 + (a - 1.0) * ka_ref[...])
        kd_o[d] = kd.astype(kd_o.dtype)
        bd_o[d] = (kn * a).astype(bd_o.dtype)
        kd_sum = kd_sum + kd

    xe = project(C_RW, D_IN_EVEN)
    r = shift_mix(r_ext, 0, s1)
    r_o[...] = r.astype(r_o.dtype)
    coef = _seg_sum(r * kd_sum * rk_ref[...], seg_ref[...])
    v = shift_mix(v_ext, s2, s3)
    v_o[...] = v.astype(v_o.dtype)
    bonus_o[...] = (coef * v).astype(bonus_o.dtype)

    xp = xe[inner]

    def at_offset(a, s):
        return pltpu.roll(a, (-s) % n_ext, 0)

    sums = {}
    acc = xe + at_offset(xe, -1)
    sums[2] = acc
    half = 1
    for win in (4, 8, 16):
        acc = at_offset(acc, -half) + at_offset(acc, half)
        sums[win] = acc
        half *= 2

    in_ctx = i >= n_lat_tiles
    tile0 = jnp.where(in_ctx, n_lat_tiles, 0)
    seq_len = jnp.where(in_ctx, (n_tiles - n_lat_tiles) * tm, n_lat_tiles * tm)
    t_seq = (i - tile0) * tm + row
    grp = lax.broadcasted_iota(jnp.int32, (1, D_B), 1) // POOL_GROUP
    mean = jnp.zeros((tm, D_B), F32)
    for gi, win in enumerate(POOL_WINDOWS):
        cnt = jnp.minimum(t_seq + win // 2, seq_len) - jnp.maximum(t_seq - win // 2, 0)
        m = sums[win][SUBLANES:SUBLANES + tm] / cnt.astype(F32)
        mean = jnp.where(grp == gi, m, mean)
    pool_o[...] = mean - xp


def _rwkv_prepare(x, ctx, mod, w_in, mu, w0, w2bd, a0, a2bd, g2, k_k, k_a, r_k, seg):
    n_lat, d = x.shape
    n_ctx = ctx.shape[0]
    t = n_lat + n_ctx
    tm = TILE_ROWS
    n_tiles = t // tm
    n_lat_tiles = n_lat // tm
    hpt = tm // SUBLANES
    lat_halos = n_lat // SUBLANES
    ctx_halos = n_ctx // SUBLANES

    def lat_tile(i):
        return jnp.minimum(i, n_lat_tiles - 1)

    def ctx_tile(i):
        return jnp.maximum(i - n_lat_tiles, 0)

    halo = (SUBLANES, d)
    tok = jax.ShapeDtypeStruct((t, D_A), BF16)
    tok2 = jax.ShapeDtypeStruct((2, t, D_A), BF16)
    ld2 = jax.ShapeDtypeStruct((2, t, D_A), F32)
    tok_spec = pl.BlockSpec((tm, D_A), lambda i: (i, 0))
    tok2_spec = pl.BlockSpec((2, tm, D_A), lambda i: (0, i, 0))
    kern = functools.partial(_prep_kernel, tm=tm, n_lat_tiles=n_lat_tiles, n_tiles=n_tiles)
    return pl.pallas_call(
        kern,
        grid=(n_tiles,),
        in_specs=[
            pl.BlockSpec((tm, d), lambda i: (lat_tile(i), 0)),
            pl.BlockSpec((tm, d), lambda i: (ctx_tile(i), 0)),
            pl.BlockSpec(halo, lambda i: (jnp.maximum(lat_tile(i) * hpt - 1, 0), 0)),
            pl.BlockSpec(halo, lambda i: (jnp.minimum((lat_tile(i) + 1) * hpt, lat_halos - 1), 0)),
            pl.BlockSpec(halo, lambda i: (jnp.maximum(ctx_tile(i) * hpt - 1, 0), 0)),
            pl.BlockSpec(halo, lambda i: (jnp.minimum((ctx_tile(i) + 1) * hpt, ctx_halos - 1), 0)),
            _mod_spec(d, n_lat_tiles), _const_spec(w_in.shape),
            _const_spec(mu.shape), _const_spec(w0.shape), _const_spec(w2bd.shape), _const_spec(a0.shape),
            _const_spec(a2bd.shape), _const_spec(g2.shape), _const_spec(k_k.shape), _const_spec(k_a.shape),
            _const_spec(r_k.shape), _const_spec(seg.shape),
        ],
        out_specs=[tok_spec, tok_spec, tok_spec, tok2_spec, tok2_spec, tok2_spec, tok_spec, tok_spec,
                   pl.BlockSpec((tm, D_B), lambda i: (i, 0)), pl.BlockSpec((tm, d), lambda i: (i, 0))],
        out_shape=[tok, tok, tok, ld2, tok2, tok2, tok, tok, jax.ShapeDtypeStruct((t, D_B), F32),
                   jax.ShapeDtypeStruct((t, d), x.dtype)],
        compiler_params=_params("arbitrary"),
        name="rwkv_prepare",
    )(x, ctx, x, x, ctx, ctx, mod, w_in, mu, w0, w2bd, a0, a2bd, g2, k_k, k_a, r_k, seg)


WKV_PAIRS_PER_STEP = 3
WKV_STAGE_STAGGER = 1


def _wkv_kernel(rf_ref, vf_ref, knf_ref, ldf_ref, kdf_ref, bdf_ref, rr_ref, vr_ref, knr_ref, ldr_ref, kdr_ref,
                bdr_ref, yf_ref, yr_ref, h_ref, lhs_s, add_s, pb_s, ends_s, dec_s, v_s, *, chunks_per_block,
                pairs_per_step):
    step = pl.program_id(1)
    cur = lax.rem(step, 2)
    prev = 1 - cur

    @pl.when(step == 0)
    def _():
        h_ref[...] = jnp.zeros_like(h_ref)
        for ref in (lhs_s, add_s, pb_s, ends_s, dec_s, v_s):
            ref[1] = jnp.zeros(ref.shape[1:], ref.dtype)

    c = CHUNK
    ri = lax.broadcasted_iota(jnp.int32, (c, c), 0)
    ci = lax.broadcasted_iota(jnp.int32, (c, c), 1)
    rp = lax.broadcasted_iota(jnp.int32, (c, LANES), 0)
    cp = lax.broadcasted_iota(jnp.int32, (c, LANES), 1) & (c - 1)
    eye_p = jnp.where(rp == cp, 1.0, 0.0)
    r2 = lax.broadcasted_iota(jnp.int32, (LANES, LANES), 0)
    c2 = lax.broadcasted_iota(jnp.int32, (LANES, LANES), 1)
    same_head = (r2 < HEAD) == (c2 < HEAD)
    head0 = lax.broadcasted_iota(jnp.int32, (1, LANES), 1) < HEAD

    def bdiag(x):
        zero = jnp.zeros_like(x)
        return jnp.concatenate([jnp.where(head0, x, zero), jnp.where(head0, zero, x)], axis=0)

    masks = {}
    for rev in (False, True):
        sgn = -1 if rev else 1
        order = (rp - cp) * sgn
        masks[rev] = (jnp.where((ri - ci) * sgn >= 0, 1.0, 0.0).astype(BF16), order > 0, order >= 0)

    chains = []
    for rev, refs, y_ref in ((False, (rf_ref, vf_ref, knf_ref, ldf_ref, kdf_ref, bdf_ref), yf_ref),
                             (True, (rr_ref, vr_ref, knr_ref, ldr_ref, kdr_ref, bdr_ref), yr_ref)):
        for q in range(pairs_per_step):
            chains.append((rev, refs, y_ref, slice(q * LANES, (q + 1) * LANES)))

    insts = []
    for ch, (rev, refs, y_ref, lanes) in enumerate(chains):
        for j in range(chunks_per_block):
            cj = chunks_per_block - 1 - j if rev else j
            insts.append(dict(idx=len(insts), ch=ch, j=j, rev=rev, refs=refs, lanes=lanes,
                              rows=slice(cj * c, (cj + 1) * c)))
    by_key = {(it["ch"], it["j"]): it for it in insts}

    states = [h_ref[ch] for ch in range(len(chains))]
    pending = {}

    def state_group(k):
        j, second = divmod(k, 2)
        for ch, (rev, refs, y_ref, lanes) in enumerate(chains):
            it = by_key[(ch, j)]
            idx = it["idx"]
            if not second:
                pending[ch] = _dot(lhs_s[prev, idx], states[ch].astype(BF16))
            else:
                m1 = pending.pop(ch)
                add = add_s[prev, idx]
                ub = (-m1[:c] - add[:c]).astype(BF16)
                y_ref[it["rows"], lanes] = m1[c:] + add[c:] + _dot(pb_s[prev, idx], bdiag(ub))
                upd = _dot(ends_s[prev, idx], jnp.concatenate([ub, v_s[prev, idx]], axis=0))
                states[ch] = states[ch] * dec_s[prev, idx] + jnp.where(same_head, upd, 0.0)

    n_state_groups = 2 * chunks_per_block
    emitted = [0]

    def interleave_state_group():
        if emitted[0] < n_state_groups:
            state_group(emitted[0])
            emitted[0] += 1

    def stage_cumsum(group):
        for it in group:
            ld = it["refs"][3][0, it["rows"], it["lanes"]]
            tri = masks[it["rev"]][0]
            ld_hi = ld.astype(BF16)
            rem = ld - ld_hi.astype(F32)
            ld_mid = rem.astype(BF16)
            ld_lo = (rem - ld_mid.astype(F32)).astype(BF16)
            part = _dot(tri, jnp.concatenate([ld_hi, ld_mid], axis=1))
            it["cum"] = part[:, :LANES] + part[:, LANES:] + _dot(tri, ld_lo)

    def stage_gram(group):
        for it in group:
            r_ref, v_ref, kn_ref, ld_ref, kd_ref, bd_ref = it["refs"]
            rows, lanes = it["rows"], it["lanes"]
            _, strict, incl = masks[it["rev"]]
            ld = ld_ref[0, rows, lanes]
            kd = kd_ref[0, rows, lanes].astype(F32)
            bd = bd_ref[0, rows, lanes].astype(F32)
            cum = it.pop("cum")
            total = jnp.sum(ld, axis=0, keepdims=True)
            inv = jnp.exp(-cum)
            to_end = inv * jnp.exp(total)
            kq = (kn_ref[rows, lanes].astype(F32) * jnp.exp(cum - ld)).astype(BF16)
            rq = (r_ref[rows, lanes].astype(F32) * jnp.exp(cum)).astype(BF16)
            g = _dot_nt(jnp.concatenate([kq, rq], axis=0),
                        jnp.concatenate([bdiag((bd * inv).astype(BF16)), bdiag((kd * inv).astype(BF16))], axis=0))
            it["q"] = jnp.where(strict, -g[:c, :LANES], 0.0)
            it["bm"] = jnp.where(strict, g[:c, LANES:], 0.0).astype(BF16)
            it["pk"] = jnp.where(incl, g[c:, LANES:], 0.0).astype(BF16)
            it["kq_bd"], it["rq"] = bdiag(kq), rq
            v = v_ref[rows, lanes]
            it["v_bd"] = bdiag(v)
            idx = it["idx"]
            pb_s[cur, idx] = jnp.where(incl, g[c:, :LANES], 0.0).astype(BF16)
            v_s[cur, idx] = v
            ends_s[cur, idx] = jnp.concatenate([bd * to_end, kd * to_end], axis=0).T.astype(BF16)
            dec_s[cur, idx] = jnp.broadcast_to(jnp.exp(jnp.sum(ld.T, axis=1, keepdims=True)), (LANES, LANES))

    def stage_square(group):
        for it in group:
            q = it.pop("q")
            it["t_inv"] = eye_p + q
            qb = q.astype(BF16)
            it["pw"] = _dot(qb, bdiag(qb)).astype(BF16)

    n_levels = int(math.log2(c)) - 1

    def stage_level(level):
        def run(group):
            for it in group:
                pw = it["pw"]
                t_bd = bdiag(it["t_inv"].astype(BF16))
                if level < n_levels - 1:
                    res = _dot(pw, jnp.concatenate([bdiag(pw), t_bd], axis=1))
                    it["pw"] = res[:, :LANES].astype(BF16)
                    it["t_inv"] = it["t_inv"] + res[:, LANES:]
                else:
                    it["t_inv"] = it["t_inv"] + _dot(pw, t_bd)
        return run

    def stage_bmv(group):
        for it in group:
            it["bmv"] = _dot(it.pop("bm"), it["v_bd"]).astype(BF16)

    def stage_apply(group):
        for it in group:
            wu = _dot(it.pop("t_inv").astype(BF16),
                      jnp.concatenate([it.pop("kq_bd"), bdiag(it.pop("bmv"))], axis=1))
            lhs_s[cur, it["idx"]] = jnp.concatenate([wu[:, :LANES].astype(BF16), it.pop("rq")], axis=0)
            it["ut"] = wu[:, LANES:]

    def stage_pkv(group):
        for it in group:
            pkv = _dot(it.pop("pk"), it["v_bd"])
            add_s[cur, it["idx"]] = jnp.concatenate([it.pop("ut"), pkv], axis=0)

    stages = ([stage_cumsum, stage_gram, stage_square] + [stage_level(lv) for lv in range(n_levels)]
              + [stage_bmv, stage_apply, stage_pkv])
    half = len(insts) // 2
    groups = (insts[:half], insts[half:])
    for step_i in range(len(stages) + WKV_STAGE_STAGGER):
        if step_i < len(stages):
            stages[step_i](groups[0])
        if 0 <= step_i - WKV_STAGE_STAGGER < len(stages):
            stages[step_i - WKV_STAGE_STAGGER](groups[1])
        interleave_state_group()
    while emitted[0] < n_state_groups:
        interleave_state_group()
    for ch in range(len(chains)):
        h_ref[ch] = states[ch]


def _wkv_scan(r, v, kn, ld, kd, bd, *, n_lat_blocks):
    t, da = r.shape
    rows = TILE_ROWS
    n_blocks = t // rows
    pps = WKV_PAIRS_PER_STEP
    width = pps * LANES
    n_groups = da // width

    n_ctx_blocks = n_blocks - n_lat_blocks

    def fwd_blk(s):
        return jnp.where(s < n_ctx_blocks, n_lat_blocks + s, s - n_ctx_blocks)

    def rev_blk(s):
        return jnp.where(s < n_ctx_blocks, n_blocks - 1 - s, n_lat_blocks - 1 - (s - n_ctx_blocks))

    def cur(s):
        return jnp.minimum(s, n_blocks - 1)

    def prv(s):
        return jnp.maximum(s - 1, 0)

    fwd = pl.BlockSpec((rows, width), lambda g, s: (fwd_blk(cur(s)), g))
    bwd = pl.BlockSpec((rows, width), lambda g, s: (rev_blk(cur(s)), g))
    fwd_d = pl.BlockSpec((1, rows, width), lambda g, s: (0, fwd_blk(cur(s)), g))
    bwd_d = pl.BlockSpec((1, rows, width), lambda g, s: (1, rev_blk(cur(s)), g))
    y_fwd = pl.BlockSpec((rows, width), lambda g, s: (fwd_blk(prv(s)), g))
    y_bwd = pl.BlockSpec((rows, width), lambda g, s: (rev_blk(prv(s)), g))
    chunks = rows // CHUNK
    n_inst = 2 * pps * chunks
    kern = functools.partial(_wkv_kernel, chunks_per_block=chunks, pairs_per_step=pps)
    out = jax.ShapeDtypeStruct((t, da), F32)
    return pl.pallas_call(
        kern,
        grid=(n_groups, n_blocks + 1),
        in_specs=[fwd, fwd, fwd, fwd_d, fwd_d, fwd_d, bwd, bwd, bwd, bwd_d, bwd_d, bwd_d],
        out_specs=[y_fwd, y_bwd],
        out_shape=[out, out],
        scratch_shapes=[
            pltpu.VMEM((2 * pps, LANES, LANES), F32),
            pltpu.VMEM((2, n_inst, 2 * CHUNK, LANES), BF16),
            pltpu.VMEM((2, n_inst, 2 * CHUNK, LANES), F32),
            pltpu.VMEM((2, n_inst, CHUNK, LANES), BF16),
            pltpu.VMEM((2, n_inst, LANES, LANES), BF16),
            pltpu.VMEM((2, n_inst, LANES, LANES), F32),
            pltpu.VMEM((2, n_inst, CHUNK, LANES), BF16),
        ],
        compiler_params=_params("arbitrary", "arbitrary"),
        name="wkv_scan",
    )(r, v, kn, ld, kd, bd, r, v, kn, ld, kd, bd)


def _ffn_sublayer(x, m, w1_ref, w3_ref, w2_ref, lng_ref, lnb_ref):
    h = (x * (1.0 + m[4:5]) + m[3:4]).astype(BF16)
    a = _dot(h, w1_ref[...])
    b = _dot(h, w3_ref[...])
    act = (a * _sigmoid(a) * b).astype(BF16)
    out = _dot(act, w2_ref[...])
    return _layer_norm(ALPHA * x + m[5:6] * out, lng_ref[...], lnb_ref[...])


def _readout_kernel(yf_ref, yr_ref, bonus_ref, g_ref, pool_ref, x_ref, mod_ref, seg_ref, lnxg_ref, lnxb_ref,
                    poolw_ref, pools_ref, wout_ref, lng_ref, lnb_ref, w1_ref, w3_ref, w2_ref, lng2_ref, lnb2_ref,
                    mod_next_ref, w_next_ref, o_ref, proj_ref):
    y = yf_ref[...] + yr_ref[...]
    seg = seg_ref[...]
    mu = _seg_sum(y, seg) * (1.0 / HEAD)
    yc = y - mu
    var = _seg_sum(yc * yc, seg) * (1.0 / HEAD)
    yn = yc * lax.rsqrt(var + GN_EPS) * lnxg_ref[...] + lnxb_ref[...]
    a = (yn + bonus_ref[...].astype(F32)) * g_ref[...].astype(F32)
    b = _dot(pool_ref[...].astype(BF16), poolw_ref[...]) * pools_ref[...]
    out = _dot(a.astype(BF16), wout_ref[:D_A, :]) + _dot(b.astype(BF16), wout_ref[D_A:, :])
    m = mod_ref[0]
    x1 = _layer_norm(ALPHA * x_ref[...] + m[2:3] * out, lng_ref[...], lnb_ref[...])
    x2 = _ffn_sublayer(x1, m, w1_ref, w3_ref, w2_ref, lng2_ref, lnb2_ref)
    o_ref[...] = x2
    mn = mod_next_ref[0]
    proj_ref[...] = _dot((x2 * (1.0 + mn[1:2]) + mn[0:1]).astype(BF16), w_next_ref[...]).astype(proj_ref.dtype)


def _single_spec(shape):
    nd = len(shape)
    return pl.BlockSpec(shape, lambda *_: (0,) * nd, pipeline_mode=pl.Buffered(1))


def _rwkv_readout_ffn(y_fwd, y_rev, bonus, g, pooled, x, mod, seg, lnx_g, lnx_b, pool_wbd, pool_scale, w_out, ln_g,
                      ln_b, w1, w3, w2, ln_g2, ln_b2, mod_next, w_next, *, n_lat_tiles):
    t, d = x.shape
    tm = TILE_ROWS
    tok = pl.BlockSpec((tm, D_A), lambda i: (i, 0))
    return pl.pallas_call(
        _readout_kernel,
        grid=(t // tm,),
        in_specs=[
            tok, tok, tok, tok,
            pl.BlockSpec((tm, D_B), lambda i: (i, 0)),
            pl.BlockSpec((tm, d), lambda i: (i, 0)),
            _mod_spec(d, n_lat_tiles),
            _const_spec(seg.shape), _const_spec(lnx_g.shape), _const_spec(lnx_b.shape),
            _const_spec(pool_wbd.shape), _const_spec(pool_scale.shape), _const_spec(w_out.shape),
            _const_spec(ln_g.shape), _const_spec(ln_b.shape),
            _single_spec(w1.shape), _single_spec(w3.shape), _single_spec(w2.shape),
            _const_spec(ln_g2.shape), _const_spec(ln_b2.shape),
            _mod_spec(d, n_lat_tiles), _single_spec(w_next.shape),
        ],
        out_specs=[pl.BlockSpec((tm, d), lambda i: (i, 0)), pl.BlockSpec((tm, w_next.shape[1]), lambda i: (i, 0))],
        out_shape=[jax.ShapeDtypeStruct((t, d), F32), jax.ShapeDtypeStruct((t, w_next.shape[1]), BF16)],
        compiler_params=_params("arbitrary"),
        name="rwkv_readout_ffn",
    )(y_fwd, y_rev, bonus, g, pooled, x, mod, seg, lnx_g, lnx_b, pool_wbd, pool_scale, w_out, ln_g, ln_b,
      w1, w3, w2, ln_g2, ln_b2, mod_next, w_next)


ATTN_PV_LAG = 4
ATTN_ROWS_PER_STEP = 32


def _attn_kernel(q_ref, k_ref, v_ref, bias_ref, o_ref, *, q_rows, n_rows, n_lat, n_ctx, kh):
    rb = pl.program_id(1)
    head0 = lax.broadcasted_iota(jnp.int32, (1, LANES), 1) < HEAD
    kc = k_ref[n_lat:n_lat + n_ctx, :]
    vc = v_ref[n_lat:n_lat + n_ctx, :]
    n_loc = kh * GRID_W
    q_all = q_ref[...] * jnp.asarray(HEAD ** -0.5, q_ref.dtype)
    zero = jnp.zeros((GRID_W, LANES), q_all.dtype)

    rows = [dict() for _ in range(q_rows)]

    def scores(j):
        it = rows[j]
        rr = rb * q_rows + j
        sr = jnp.clip(rr - kh // 2, 0, n_rows - kh)
        t_var = sr - rr + (NA_KH - 1)
        q = q_all[j * GRID_W:(j + 1) * GRID_W]
        q2 = jnp.concatenate([jnp.where(head0, q, zero), jnp.where(head0, zero, q)], axis=0)
        it["krows"] = pl.ds(pl.multiple_of(sr * GRID_W, GRID_W), n_loc)
        bias = jnp.concatenate([bias_ref[0, t_var], bias_ref[1, t_var]], axis=0)
        it["s_loc"] = _dot_nt(q2, k_ref[it["krows"], :]) + bias
        it["s_ctx"] = _dot_nt(q2, kc)

    def softmax(j):
        it = rows[j]
        s_loc, s_ctx = it.pop("s_loc"), it.pop("s_ctx")
        m = jnp.maximum(jnp.max(s_loc, axis=-1, keepdims=True), jnp.max(s_ctx, axis=-1, keepdims=True))
        p_loc = jnp.exp(s_loc - m)
        p_ctx = jnp.exp(s_ctx - m)
        it["denom"] = jnp.sum(p_loc, axis=-1, keepdims=True) + jnp.sum(p_ctx, axis=-1, keepdims=True)
        it["p_loc"] = p_loc.astype(BF16)
        it["p_ctx"] = p_ctx.astype(BF16)

    def values(j):
        it = rows[j]
        o2 = (_dot(it.pop("p_loc"), v_ref[it["krows"], :]) + _dot(it.pop("p_ctx"), vc)) / it["denom"]
        o_ref[j * GRID_W:(j + 1) * GRID_W, :] = jnp.where(head0, o2[:GRID_W], o2[GRID_W:]).astype(o_ref.dtype)

    for j in range(q_rows + ATTN_PV_LAG):
        if j < q_rows:
            scores(j)
        if 0 <= j - 1 < q_rows:
            softmax(j - 1)
        if 0 <= j - ATTN_PV_LAG < q_rows:
            values(j - ATTN_PV_LAG)


def _neighbourhood_attention(qkv, bias_tab, *, n_ctx, n_lat):
    t_all = qkv.shape[0]
    d = D_MODEL
    n_rows = n_lat // GRID_W
    kh = min(NA_KH, n_rows)
    q_rows = ATTN_ROWS_PER_STEP
    q_tile = q_rows * GRID_W
    n_pairs = d // LANES
    kern = functools.partial(_attn_kernel, q_rows=q_rows, n_rows=n_rows, n_lat=n_lat, n_ctx=n_ctx, kh=kh)
    n_var = bias_tab.shape[1]
    return pl.pallas_call(
        kern,
        grid=(n_pairs, n_lat // q_tile),
        in_specs=[
            pl.BlockSpec((q_tile, LANES), lambda p, b: (b, p)),
            pl.BlockSpec((t_all, LANES), lambda p, b: (0, n_pairs + p)),
            pl.BlockSpec((t_all, LANES), lambda p, b: (0, 2 * n_pairs + p)),
            pl.BlockSpec((2, n_var, GRID_W, kh * GRID_W), lambda p, b: (p, 0, 0, 0)),
        ],
        out_specs=pl.BlockSpec((q_tile, LANES), lambda p, b: (b, p)),
        out_shape=jax.ShapeDtypeStruct((n_lat, d), BF16),
        compiler_params=_params("arbitrary", "arbitrary"),
        name="neighbourhood_attention",
    )(qkv, qkv, qkv, bias_tab)


def _attention_bias_table(rpb, n_rows):
    kh = min(NA_KH, n_rows)
    cols = np.arange(GRID_W)
    col_start = np.clip(cols - NA_KW // 2, 0, GRID_W - NA_KW)
    in_win = (cols[None, :] >= col_start[:, None]) & (cols[None, :] < col_start[:, None] + NA_KW)
    col_off = cols[None, :] - cols[:, None] + (NA_KW - 1)
    onehot = (col_off[None] == np.arange(2 * NA_KW - 1)[:, None, None]) & in_win[None]
    dense = jnp.einsum('hrj,jck->hcrk', rpb, jnp.asarray(onehot, F32), precision=HIGHEST)
    dense = jnp.where(jnp.asarray(in_win)[None, :, None, :], dense, NEG_BIAS)
    tab = jnp.stack([dense[:, :, t:t + kh].reshape(rpb.shape[0], GRID_W, kh * GRID_W) for t in range(NA_KH)],
                    axis=1)
    return tab.astype(F32)


def _proj_ffn_kernel(o_ref, x_ref, mod_ref, w_ref, lng_ref, lnb_ref, w1_ref, w3_ref, w2_ref, lng2_ref, lnb2_ref,
                     out_ref):
    m = mod_ref[0]
    y = _dot(o_ref[...], w_ref[...])
    x1 = _layer_norm(ALPHA * x_ref[...] + m[2:3] * y, lng_ref[...], lnb_ref[...])
    out_ref[...] = _ffn_sublayer(x1, m, w1_ref, w3_ref, w2_ref, lng2_ref, lnb2_ref)


def _attn_proj_ffn(o, x, mod, w, ln_g, ln_b, w1, w3, w2, ln_g2, ln_b2, *, tm):
    t, d = o.shape
    return pl.pallas_call(
        _proj_ffn_kernel,
        grid=(t // tm,),
        in_specs=[
            pl.BlockSpec((tm, d), lambda i: (i, 0)),
            pl.BlockSpec((tm, d), lambda i: (i, 0)),
            pl.BlockSpec((1, SUBLANES, d), lambda i: (1, 0, 0)),
            _single_spec(w.shape), _const_spec(ln_g.shape), _const_spec(ln_b.shape),
            _single_spec(w1.shape), _single_spec(w3.shape), _single_spec(w2.shape),
            _const_spec(ln_g2.shape), _const_spec(ln_b2.shape),
        ],
        out_specs=pl.BlockSpec((tm, d), lambda i: (i, 0)),
        out_shape=jax.ShapeDtypeStruct((t, d), F32),
        compiler_params=_params("arbitrary"),
        name="attn_proj_ffn",
    )(o, x, mod, w, ln_g, ln_b, w1, w3, w2, ln_g2, ln_b2)


def _block_diag2(w):
    z = jnp.zeros_like(w[0])
    return jnp.concatenate([jnp.concatenate([w[0], z], axis=1), jnp.concatenate([z, w[1]], axis=1)], axis=0)


def kernel(x, c, ctx, c_ctx, ada_w, ada_b, ln_g, ln_b, ffn_w1, ffn_w3, ffn_w2, ev_w_in, ev_shift_mu, ev_w0,
           ev_w2, ev_a0, ev_a2, ev_g2, ev_k_k, ev_k_a, ev_r_k, ev_lnx_g, ev_lnx_b, ev_pool_w, ev_pool_scale,
           ev_w_out, od_w_in, od_rpb, od_w_out):
    batch, n_lat, d = x.shape
    n_ctx = ctx.shape[1]
    assert batch == 1 and d == D_MODEL
    assert n_ctx % TILE_ROWS == 0 and n_lat % LAT_TILE_ROWS == 0 and n_lat % (ATTN_ROWS_PER_STEP * GRID_W) == 0
    n_lat_tiles = n_lat // TILE_ROWS

    cs = jnp.zeros((SUBLANES, d), F32).at[0].set(c_ctx).at[1].set(c[0])
    mod_all = _ada_modulation(cs, ada_w, ada_b)
    mod_all = mod_all[:, :2].reshape(DEPTH, 2, 6, d)
    mod_all = jnp.pad(mod_all, ((0, 0), (0, 0), (0, SUBLANES - 6), (0, 0)))

    mod = mod_all[0]
    seg = (jnp.arange(LANES)[:, None] // HEAD == jnp.arange(LANES)[None, :] // HEAD).astype(BF16)
    r, v, kn, ld, kd, bd, g, bonus, pooled, xa = _rwkv_prepare(
        x[0], ctx[0], mod, ev_w_in[0].astype(BF16), ev_shift_mu[0], ev_w0[0],
        _block_diag2(ev_w2[0]).astype(BF16), ev_a0[0], _block_diag2(ev_a2[0]).astype(BF16),
        ev_g2[0].astype(BF16), ev_k_k[0].reshape(1, D_A), ev_k_a[0].reshape(1, D_A),
        ev_r_k[0].reshape(1, D_A), seg)
    y_fwd, y_rev = _wkv_scan(r, v, kn, ld, kd, bd, n_lat_blocks=n_lat_tiles)
    pw = ev_pool_w[0]
    pool_wbd = jnp.zeros((D_B, D_B), F32)
    for gi in range(len(POOL_WINDOWS)):
        sl = slice(gi * POOL_GROUP, (gi + 1) * POOL_GROUP)
        pool_wbd = pool_wbd.at[sl, sl].set(pw[gi])
    xa, qkv = _rwkv_readout_ffn(
        y_fwd, y_rev, bonus, g, pooled, xa, mod, seg, ev_lnx_g[0].reshape(1, D_A), ev_lnx_b[0].reshape(1, D_A),
        pool_wbd.astype(BF16), ev_pool_scale[0].reshape(1, D_B), ev_w_out[0].astype(BF16),
        ln_g[0, 0].reshape(1, d), ln_b[0, 0].reshape(1, d),
        ffn_w1[0].astype(BF16), ffn_w3[0].astype(BF16), ffn_w2[0].astype(BF16),
        ln_g[0, 1].reshape(1, d), ln_b[0, 1].reshape(1, d), mod_all[1], od_w_in[0].astype(BF16),
        n_lat_tiles=n_lat_tiles)

    mod = mod_all[1]
    bias_tab = _attention_bias_table(od_rpb[0], n_lat // GRID_W)
    o = _neighbourhood_attention(qkv, bias_tab, n_ctx=n_ctx, n_lat=n_lat)
    xl = _attn_proj_ffn(
        o, xa, mod, od_w_out[0].astype(BF16), ln_g[1, 0].reshape(1, d), ln_b[1, 0].reshape(1, d),
        ffn_w1[1].astype(BF16), ffn_w3[1].astype(BF16), ffn_w2[1].astype(BF16),
        ln_g[1, 1].reshape(1, d), ln_b[1, 1].reshape(1, d), tm=LAT_TILE_ROWS)
    return xl[None]
```

```python
import functools
import math

import jax
import jax.numpy as jnp
import numpy as np
from jax import lax
from jax.experimental import pallas as pl
from jax.experimental.pallas import tpu as pltpu

F32 = jnp.float32
BF16 = jnp.bfloat16
HIGHEST = lax.Precision.HIGHEST

D_MODEL = 1024
DEPTH = 2
GRID_W = 64
ALPHA = (2 * DEPTH) ** 0.25
LN_EPS = 1e-6
HEAD = 64
D_A = 3 * D_MODEL // 4
H_A = D_A // HEAD
LORA_W = 64
LORA_A = 64
LORA_G = 128
GN_EPS = 64e-5
D_B = D_MODEL - D_A
POOL_WINDOWS = (2, 4, 8, 16)
POOL_GROUP = D_B // len(POOL_WINDOWS)
C_RW = 3 * D_A + 2 * LORA_W + 2 * LORA_A + LORA_G
D_IN_EVEN = C_RW + D_B
H_C = D_MODEL // HEAD
NA_KH = 8
NA_KW = 16

LANES = 128
SUBLANES = 8
VMEM_LIMIT_BYTES = 56 * 1024 * 1024

TILE_ROWS = 256
LAT_TILE_ROWS = 512
CHUNK = 64
NEG_BIAS = -1e30


def _sigmoid(x):
    return 1.0 / (1.0 + jnp.exp(-x))


def _layer_norm(z, g, b):
    mu = jnp.mean(z, axis=-1, keepdims=True)
    zc = z - mu
    var = jnp.mean(zc * zc, axis=-1, keepdims=True)
    return zc * lax.rsqrt(var + LN_EPS) * g + b


def _dot(a, b):
    return jnp.dot(a, b, preferred_element_type=F32)


def _dot_hi(a, b):
    return jnp.dot(a, b, precision=HIGHEST, preferred_element_type=F32)


def _dot_nt(a, b):
    return lax.dot_general(a, b, (((1,), (1,)), ((), ())), preferred_element_type=F32)


def _seg_sum(x, seg):
    xb = x.astype(BF16)
    return jnp.concatenate([_dot(xb[:, j:j + LANES], seg) for j in range(0, x.shape[1], LANES)], axis=1)


def _params(*sem):
    return pltpu.CompilerParams(dimension_semantics=sem, vmem_limit_bytes=VMEM_LIMIT_BYTES)


def _const_spec(shape):
    nd = len(shape)
    return pl.BlockSpec(shape, lambda *_: (0,) * nd)


def _mod_spec(d, n_lat_tiles):
    return pl.BlockSpec((1, SUBLANES, d), lambda i: ((i < n_lat_tiles).astype(jnp.int32), 0, 0))


def _ada_kernel(cs_ref, w_ref, b_ref, o_ref):
    s = cs_ref[...]
    s = s * _sigmoid(s)
    o_ref[0] = _dot_hi(s, w_ref[0]) + b_ref[0]


def _ada_modulation(cs, ada_w, ada_b):
    depth, d, n = ada_w.shape
    nb = 1536
    return pl.pallas_call(
        _ada_kernel,
        grid=(depth, n // nb),
        in_specs=[
            pl.BlockSpec((SUBLANES, d), lambda i, j: (0, 0)),
            pl.BlockSpec((1, d, nb), lambda i, j: (i, 0, j)),
            pl.BlockSpec((1, 1, nb), lambda i, j: (i, 0, j)),
        ],
        out_specs=pl.BlockSpec((1, SUBLANES, nb), lambda i, j: (i, 0, j)),
        out_shape=jax.ShapeDtypeStruct((depth, SUBLANES, n), F32),
        compiler_params=_params("arbitrary", "arbitrary"),
        name="ada_modulation",
    )(cs, ada_w, ada_b.reshape(depth, 1, n))


def _prep_kernel(x_ref, ctx_ref, xhp_ref, xhn_ref, chp_ref, chn_ref, mod_ref, w_ref, mu_ref, w0_ref, w2_ref,
                 a0_ref, a2_ref, g2_ref, kk_ref, ka_ref, rk_ref, seg_ref,
                 r_o, v_o, kn_o, ld_o, kd_o, bd_o, g_o, bonus_o, pool_o, xa_o, *, tm, n_lat_tiles, n_tiles):
    i = pl.program_id(0)
    is_lat = i < n_lat_tiles
    first = jnp.logical_or(i == 0, i == n_lat_tiles)
    last = jnp.logical_or(i == n_lat_tiles - 1, i == n_tiles - 1)
    row = lax.broadcasted_iota(jnp.int32, (tm, 1), 0)
    n_ext = tm + 2 * SUBLANES

    xt = jnp.where(is_lat, x_ref[...], ctx_ref[...])
    xa_o[...] = xt
    x_ext = jnp.concatenate([jnp.where(is_lat, xhp_ref[...], chp_ref[...]), xt,
                             jnp.where(is_lat, xhn_ref[...], chn_ref[...])], axis=0)
    m = mod_ref[0]
    h_ext = (x_ext * (1.0 + m[1:2]) + m[0:1]).astype(BF16)
    mu_prev = mu_ref[0:1]
    mu_next = mu_ref[1:2]
    mu_self = 1.0 - mu_prev - mu_next
    inner = slice(SUBLANES, SUBLANES + tm)

    def project(c0, c1):
        p = _dot(h_ext, w_ref[:, c0:c1])
        return jnp.concatenate([jnp.where(first, 0.0, p[:SUBLANES]), p[inner],
                                jnp.where(last, 0.0, p[SUBLANES + tm:])], axis=0)

    def shift_mix(p, c0, c1):
        return (p[inner] * mu_self[:, c0:c1] + pltpu.roll(p, 1, 0)[inner] * mu_prev[:, c0:c1]
                + pltpu.roll(p, n_ext - 1, 0)[inner] * mu_next[:, c0:c1])

    s1, s2, s3 = D_A, 2 * D_A, 3 * D_A
    s4 = s3 + 2 * LORA_W
    s5 = s4 + 2 * LORA_A
    lora = shift_mix(project(s3, C_RW), s3, C_RW)
    k_ext = project(s1, s2)
    lw = _dot(jnp.tanh(lora[:, :s4 - s3]).astype(BF16), w2_ref[...])
    la = _dot(lora[:, s4 - s3:s5 - s3].astype(BF16), a2_ref[...])
    g = _dot(_sigmoid(lora[:, s5 - s3:]).astype(BF16), g2_ref[...])
    g_o[...] = g.astype(g_o.dtype)

    r_ext = project(0, s1)
    k = shift_mix(k_ext, s1, s2)
    kr = k * kk_ref[...]
    ss = _seg_sum(kr * kr, seg_ref[...])
    kn = kr * lax.rsqrt(jnp.maximum(ss, 1e-24))
    kn_o[...] = kn.astype(kn_o.dtype)

    v_ext = project(s2, s3)
    decay_scale = math.exp(-0.5)
    kd_sum = jnp.zeros_like(k)
    for d in range(2):
        sl = slice(d * D_A, (d + 1) * D_A)
        ld_o[d] = -decay_scale * _sigmoid(w0_ref[d:d + 1] + lw[:, sl])
        a = _sigmoid(a0_ref[d:d + 1] + la[:, sl])
        kd = k * (1.0 + (a - 1.0) * ka_ref[...])
        kd_o[d] = kd.astype(kd_o.dtype)
        bd_o[d] = (kn * a).astype(bd_o.dtype)
        kd_sum = kd_sum + kd

    xe = project(C_RW, D_IN_EVEN)
    r = shift_mix(r_ext, 0, s1)
    r_o[...] = r.astype(r_o.dtype)
    coef = _seg_sum(r * kd_sum * rk_ref[...], seg_ref[...])
    v = shift_mix(v_ext, s2, s3)
    v_o[...] = v.astype(v_o.dtype)
    bonus_o[...] = (coef * v).astype(bonus_o.dtype)

    xp = xe[inner]

    def at_offset(a, s):
        return pltpu.roll(a, (-s) % n_ext, 0)

    sums = {}
    acc = xe + at_offset(xe, -1)
    sums[2] = acc
    half = 1
    for win in (4, 8, 16):
        acc = at_offset(acc, -half) + at_offset(acc, half)
        sums[win] = acc
        half *= 2

    in_ctx = i >= n_lat_tiles
    tile0 = jnp.where(in_ctx, n_lat_tiles, 0)
    seq_len = jnp.where(in_ctx, (n_tiles - n_lat_tiles) * tm, n_lat_tiles * tm)
    t_seq = (i - tile0) * tm + row
    grp = lax.broadcasted_iota(jnp.int32, (1, D_B), 1) // POOL_GROUP
    mean = jnp.zeros((tm, D_B), F32)
    for gi, win in enumerate(POOL_WINDOWS):
        cnt = jnp.minimum(t_seq + win // 2, seq_len) - jnp.maximum(t_seq - win // 2, 0)
        m = sums[win][SUBLANES:SUBLANES + tm] / cnt.astype(F32)
        mean = jnp.where(grp == gi, m, mean)
    pool_o[...] = mean - xp


def _rwkv_prepare(x, ctx, mod, w_in, mu, w0, w2bd, a0, a2bd, g2, k_k, k_a, r_k, seg):
    n_lat, d = x.shape
    n_ctx = ctx.shape[0]
    t = n_lat + n_ctx
    tm = TILE_ROWS
    n_tiles = t // tm
    n_lat_tiles = n_lat // tm
    hpt = tm // SUBLANES
    lat_halos = n_lat // SUBLANES
    ctx_halos = n_ctx // SUBLANES

    def lat_tile(i):
        return jnp.minimum(i, n_lat_tiles - 1)

    def ctx_tile(i):
        return jnp.maximum(i - n_lat_tiles, 0)

    halo = (SUBLANES, d)
    tok = jax.ShapeDtypeStruct((t, D_A), BF16)
    tok2 = jax.ShapeDtypeStruct((2, t, D_A), BF16)
    ld2 = jax.ShapeDtypeStruct((2, t, D_A), F32)
    tok_spec = pl.BlockSpec((tm, D_A), lambda i: (i, 0))
    tok2_spec = pl.BlockSpec((2, tm, D_A), lambda i: (0, i, 0))
    kern = functools.partial(_prep_kernel, tm=tm, n_lat_tiles=n_lat_tiles, n_tiles=n_tiles)
    return pl.pallas_call(
        kern,
        grid=(n_tiles,),
        in_specs=[
            pl.BlockSpec((tm, d), lambda i: (lat_tile(i), 0)),
            pl.BlockSpec((tm, d), lambda i: (ctx_tile(i), 0)),
            pl.BlockSpec(halo, lambda i: (jnp.maximum(lat_tile(i) * hpt - 1, 0), 0)),
            pl.BlockSpec(halo, lambda i: (jnp.minimum((lat_tile(i) + 1) * hpt, lat_halos - 1), 0)),
            pl.BlockSpec(halo, lambda i: (jnp.maximum(ctx_tile(i) * hpt - 1, 0), 0)),
            pl.BlockSpec(halo, lambda i: (jnp.minimum((ctx_tile(i) + 1) * hpt, ctx_halos - 1), 0)),
            _mod_spec(d, n_lat_tiles), _const_spec(w_in.shape),
            _const_spec(mu.shape), _const_spec(w0.shape), _const_spec(w2bd.shape), _const_spec(a0.shape),
            _const_spec(a2bd.shape), _const_spec(g2.shape), _const_spec(k_k.shape), _const_spec(k_a.shape),
            _const_spec(r_k.shape), _const_spec(seg.shape),
        ],
        out_specs=[tok_spec, tok_spec, tok_spec, tok2_spec, tok2_spec, tok2_spec, tok_spec, tok_spec,
                   pl.BlockSpec((tm, D_B), lambda i: (i, 0)), pl.BlockSpec((tm, d), lambda i: (i, 0))],
        out_shape=[tok, tok, tok, ld2, tok2, tok2, tok, tok, jax.ShapeDtypeStruct((t, D_B), F32),
                   jax.ShapeDtypeStruct((t, d), x.dtype)],
        compiler_params=_params("arbitrary"),
        name="rwkv_prepare",
    )(x, ctx, x, x, ctx, ctx, mod, w_in, mu, w0, w2bd, a0, a2bd, g2, k_k, k_a, r_k, seg)


WKV_PAIRS_PER_STEP = 3
WKV_STAGE_STAGGER = 1


def _wkv_kernel(rf_ref, vf_ref, knf_ref, ldf_ref, kdf_ref, bdf_ref, rr_ref, vr_ref, knr_ref, ldr_ref, kdr_ref,
                bdr_ref, yf_ref, yr_ref, h_ref, lhs_s, add_s, pb_s, ends_s, dec_s, v_s, *, chunks_per_block,
                pairs_per_step):
    step = pl.program_id(1)
    cur = lax.rem(step, 2)
    prev = 1 - cur

    @pl.when(step == 0)
    def _():
        h_ref[...] = jnp.zeros_like(h_ref)
        for ref in (lhs_s, add_s, pb_s, ends_s, dec_s, v_s):
            ref[1] = jnp.zeros(ref.shape[1:], ref.dtype)

    c = CHUNK
    ri = lax.broadcasted_iota(jnp.int32, (c, c), 0)
    ci = lax.broadcasted_iota(jnp.int32, (c, c), 1)
    rp = lax.broadcasted_iota(jnp.int32, (c, LANES), 0)
    cp = lax.broadcasted_iota(jnp.int32, (c, LANES), 1) & (c - 1)
    eye_p = jnp.where(rp == cp, 1.0, 0.0)
    r2 = lax.broadcasted_iota(jnp.int32, (LANES, LANES), 0)
    c2 = lax.broadcasted_iota(jnp.int32, (LANES, LANES), 1)
    same_head = (r2 < HEAD) == (c2 < HEAD)
    head0 = lax.broadcasted_iota(jnp.int32, (1, LANES), 1) < HEAD

    def bdiag(x):
        zero = jnp.zeros_like(x)
        return jnp.concatenate([jnp.where(head0, x, zero), jnp.where(head0, zero, x)], axis=0)

    masks = {}
    for rev in (False, True):
        sgn = -1 if rev else 1
        order = (rp - cp) * sgn
        masks[rev] = (jnp.where((ri - ci) * sgn >= 0, 1.0, 0.0).astype(BF16), order > 0, order >= 0)

    chains = []
    for rev, refs, y_ref in ((False, (rf_ref, vf_ref, knf_ref, ldf_ref, kdf_ref, bdf_ref), yf_ref),
                             (True, (rr_ref, vr_ref, knr_ref, ldr_ref, kdr_ref, bdr_ref), yr_ref)):
        for q in range(pairs_per_step):
            chains.append((rev, refs, y_ref, slice(q * LANES, (q + 1) * LANES)))

    insts = []
    for ch, (rev, refs, y_ref, lanes) in enumerate(chains):
        for j in range(chunks_per_block):
            cj = chunks_per_block - 1 - j if rev else j
            insts.append(dict(idx=len(insts), ch=ch, j=j, rev=rev, refs=refs, lanes=lanes,
                              rows=slice(cj * c, (cj + 1) * c)))
    by_key = {(it["ch"], it["j"]): it for it in insts}

    states = [h_ref[ch] for ch in range(len(chains))]
    pending = {}

    def state_group(k):
        j, second = divmod(k, 2)
        for ch, (rev, refs, y_ref, lanes) in enumerate(chains):
            it = by_key[(ch, j)]
            idx = it["idx"]
            if not second:
                pending[ch] = _dot(lhs_s[prev, idx], states[ch].astype(BF16))
            else:
                m1 = pending.pop(ch)
                add = add_s[prev, idx]
                ub = (-m1[:c] - add[:c]).astype(BF16)
                y_ref[it["rows"], lanes] = m1[c:] + add[c:] + _dot(pb_s[prev, idx], bdiag(ub))
                upd = _dot(ends_s[prev, idx], jnp.concatenate([ub, v_s[prev, idx]], axis=0))
                states[ch] = states[ch] * dec_s[prev, idx] + jnp.where(same_head, upd, 0.0)

    n_state_groups = 2 * chunks_per_block
    emitted = [0]

    def interleave_state_group():
        if emitted[0] < n_state_groups:
            state_group(emitted[0])
            emitted[0] += 1

    def stage_cumsum(group):
        for it in group:
            ld = it["refs"][3][0, it["rows"], it["lanes"]]
            tri = masks[it["rev"]][0]
            ld_hi = ld.astype(BF16)
            rem = ld - ld_hi.astype(F32)
            ld_mid = rem.astype(BF16)
            ld_lo = (rem - ld_mid.astype(F32)).astype(BF16)
            part = _dot(tri, jnp.concatenate([ld_hi, ld_mid], axis=1))
            it["cum"] = part[:, :LANES] + part[:, LANES:] + _dot(tri, ld_lo)

    def stage_gram(group):
        for it in group:
            r_ref, v_ref, kn_ref, ld_ref, kd_ref, bd_ref = it["refs"]
            rows, lanes = it["rows"], it["lanes"]
            _, strict, incl = masks[it["rev"]]
            ld = ld_ref[0, rows, lanes]
            kd = kd_ref[0, rows, lanes].astype(F32)
            bd = bd_ref[0, rows, lanes].astype(F32)
            cum = it.pop("cum")
            total = jnp.sum(ld, axis=0, keepdims=True)
            inv = jnp.exp(-cum)
            to_end = inv * jnp.exp(total)
            kq = (kn_ref[rows, lanes].astype(F32) * jnp.exp(cum - ld)).astype(BF16)
            rq = (r_ref[rows, lanes].astype(F32) * jnp.exp(cum)).astype(BF16)
            g = _dot_nt(jnp.concatenate([kq, rq], axis=0),
                        jnp.concatenate([bdiag((bd * inv).astype(BF16)), bdiag((kd * inv).astype(BF16))], axis=0))
            it["q"] = jnp.where(strict, -g[:c, :LANES], 0.0)
            it["bm"] = jnp.where(strict, g[:c, LANES:], 0.0).astype(BF16)
            it["pk"] = jnp.where(incl, g[c:, LANES:], 0.0).astype(BF16)
            it["kq_bd"], it["rq"] = bdiag(kq), rq
            v = v_ref[rows, lanes]
            it["v_bd"] = bdiag(v)
            idx = it["idx"]
            pb_s[cur, idx] = jnp.where(incl, g[c:, :LANES], 0.0).astype(BF16)
            v_s[cur, idx] = v
            ends_s[cur, idx] = jnp.concatenate([bd * to_end, kd * to_end], axis=0).T.astype(BF16)
            dec_s[cur, idx] = jnp.broadcast_to(jnp.exp(jnp.sum(ld.T, axis=1, keepdims=True)), (LANES, LANES))

    def stage_square(group):
        for it in group:
            q = it.pop("q")
            it["t_inv"] = eye_p + q
            qb = q.astype(BF16)
            it["pw"] = _dot(qb, bdiag(qb)).astype(BF16)

    n_levels = int(math.log2(c)) - 1

    def stage_level(level):
        def run(group):
            for it in group:
                pw = it["pw"]
                t_bd = bdiag(it["t_inv"].astype(BF16))
                if level < n_levels - 1:
                    res = _dot(pw, jnp.concatenate([bdiag(pw), t_bd], axis=1))
                    it["pw"] = res[:, :LANES].astype(BF16)
                    it["t_inv"] = it["t_inv"] + res[:, LANES:]
                else:
                    it["t_inv"] = it["t_inv"] + _dot(pw, t_bd)
        return run

    def stage_bmv(group):
        for it in group:
            it["bmv"] = _dot(it.pop("bm"), it["v_bd"]).astype(BF16)

    def stage_apply(group):
        for it in group:
            wu = _dot(it.pop("t_inv").astype(BF16),
                      jnp.concatenate([it.pop("kq_bd"), bdiag(it.pop("bmv"))], axis=1))
            lhs_s[cur, it["idx"]] = jnp.concatenate([wu[:, :LANES].astype(BF16), it.pop("rq")], axis=0)
            it["ut"] = wu[:, LANES:]

    def stage_pkv(group):
        for it in group:
            pkv = _dot(it.pop("pk"), it["v_bd"])
            add_s[cur, it["idx"]] = jnp.concatenate([it.pop("ut"), pkv], axis=0)

    stages = ([stage_cumsum, stage_gram, stage_square] + [stage_level(lv) for lv in range(n_levels)]
              + [stage_bmv, stage_apply, stage_pkv])
    half = len(insts) // 2
    groups = (insts[:half], insts[half:])
    for step_i in range(len(stages) + WKV_STAGE_STAGGER):
        if step_i < len(stages):
            stages[step_i](groups[0])
        if 0 <= step_i - WKV_STAGE_STAGGER < len(stages):
            stages[step_i - WKV_STAGE_STAGGER](groups[1])
        interleave_state_group()
    while emitted[0] < n_state_groups:
        interleave_state_group()
    for ch in range(len(chains)):
        h_ref[ch] = states[ch]


def _wkv_scan(r, v, kn, ld, kd, bd, *, n_lat_blocks):
    t, da = r.shape
    rows = TILE_ROWS
    n_blocks = t // rows
    pps = WKV_PAIRS_PER_STEP
    width = pps * LANES
    n_groups = da // width

    n_ctx_blocks = n_blocks - n_lat_blocks

    def fwd_blk(s):
        return jnp.where(s < n_ctx_blocks, n_lat_blocks + s, s - n_ctx_blocks)

    def rev_blk(s):
        return jnp.where(s < n_ctx_blocks, n_blocks - 1 - s, n_lat_blocks - 1 - (s - n_ctx_blocks))

    def cur(s):
        return jnp.minimum(s, n_blocks - 1)

    def prv(s):
        return jnp.maximum(s - 1, 0)

    fwd = pl.BlockSpec((rows, width), lambda g, s: (fwd_blk(cur(s)), g))
    bwd = pl.BlockSpec((rows, width), lambda g, s: (rev_blk(cur(s)), g))
    fwd_d = pl.BlockSpec((1, rows, width), lambda g, s: (0, fwd_blk(cur(s)), g))
    bwd_d = pl.BlockSpec((1, rows, width), lambda g, s: (1, rev_blk(cur(s)), g))
    y_fwd = pl.BlockSpec((rows, width), lambda g, s: (fwd_blk(prv(s)), g))
    y_bwd = pl.BlockSpec((rows, width), lambda g, s: (rev_blk(prv(s)), g))
    chunks = rows // CHUNK
    n_inst = 2 * pps * chunks
    kern = functools.partial(_wkv_kernel, chunks_per_block=chunks, pairs_per_step=pps)
    out = jax.ShapeDtypeStruct((t, da), F32)
    return pl.pallas_call(
        kern,
        grid=(n_groups, n_blocks + 1),
        in_specs=[fwd, fwd, fwd, fwd_d, fwd_d, fwd_d, bwd, bwd, bwd, bwd_d, bwd_d, bwd_d],
        out_specs=[y_fwd, y_bwd],
        out_shape=[out, out],
        scratch_shapes=[
            pltpu.VMEM((2 * pps, LANES, LANES), F32),
            pltpu.VMEM((2, n_inst, 2 * CHUNK, LANES), BF16),
            pltpu.VMEM((2, n_inst, 2 * CHUNK, LANES), F32),
            pltpu.VMEM((2, n_inst, CHUNK, LANES), BF16),
            pltpu.VMEM((2, n_inst, LANES, LANES), BF16),
            pltpu.VMEM((2, n_inst, LANES, LANES), F32),
            pltpu.VMEM((2, n_inst, CHUNK, LANES), BF16),
        ],
        compiler_params=_params("arbitrary", "arbitrary"),
        name="wkv_scan",
    )(r, v, kn, ld, kd, bd, r, v, kn, ld, kd, bd)


def _ffn_sublayer(x, m, w1_ref, w3_ref, w2_ref, lng_ref, lnb_ref):
    h = (x * (1.0 + m[4:5]) + m[3:4]).astype(BF16)
    a = _dot(h, w1_ref[...])
    b = _dot(h, w3_ref[...])
    act = (a * _sigmoid(a) * b).astype(BF16)
    out = _dot(act, w2_ref[...])
    return _layer_norm(ALPHA * x + m[5:6] * out, lng_ref[...], lnb_ref[...])


def _readout_kernel(yf_ref, yr_ref, bonus_ref, g_ref, pool_ref, x_ref, mod_ref, seg_ref, lnxg_ref, lnxb_ref,
                    poolw_ref, pools_ref, wout_ref, lng_ref, lnb_ref, w1_ref, w3_ref, w2_ref, lng2_ref, lnb2_ref,
                    mod_next_ref, w_next_ref, o_ref, proj_ref):
    y = yf_ref[...] + yr_ref[...]
    seg = seg_ref[...]
    mu = _seg_sum(y, seg) * (1.0 / HEAD)
    yc = y - mu
    var = _seg_sum(yc * yc, seg) * (1.0 / HEAD)
    yn = yc * lax.rsqrt(var + GN_EPS) * lnxg_ref[...] + lnxb_ref[...]
    a = (yn + bonus_ref[...].astype(F32)) * g_ref[...].astype(F32)
    b = _dot(pool_ref[...].astype(BF16), poolw_ref[...]) * pools_ref[...]
    out = _dot(a.astype(BF16), wout_ref[:D_A, :]) + _dot(b.astype(BF16), wout_ref[D_A:, :])
    m = mod_ref[0]
    x1 = _layer_norm(ALPHA * x_ref[...] + m[2:3] * out, lng_ref[...], lnb_ref[...])
    x2 = _ffn_sublayer(x1, m, w1_ref, w3_ref, w2_ref, lng2_ref, lnb2_ref)
    o_ref[...] = x2
    mn = mod_next_ref[0]
    proj_ref[...] = _dot((x2 * (1.0 + mn[1:2]) + mn[0:1]).astype(BF16), w_next_ref[...]).astype(proj_ref.dtype)


def _single_spec(shape):
    nd = len(shape)
    return pl.BlockSpec(shape, lambda *_: (0,) * nd, pipeline_mode=pl.Buffered(1))


def _rwkv_readout_ffn(y_fwd, y_rev, bonus, g, pooled, x, mod, seg, lnx_g, lnx_b, pool_wbd, pool_scale, w_out, ln_g,
                      ln_b, w1, w3, w2, ln_g2, ln_b2, mod_next, w_next, *, n_lat_tiles):
    t, d = x.shape
    tm = TILE_ROWS
    tok = pl.BlockSpec((tm, D_A), lambda i: (i, 0))
    return pl.pallas_call(
        _readout_kernel,
        grid=(t // tm,),
        in_specs=[
            tok, tok, tok, tok,
            pl.BlockSpec((tm, D_B), lambda i: (i, 0)),
            pl.BlockSpec((tm, d), lambda i: (i, 0)),
            _mod_spec(d, n_lat_tiles),
            _const_spec(seg.shape), _const_spec(lnx_g.shape), _const_spec(lnx_b.shape),
            _const_spec(pool_wbd.shape), _const_spec(pool_scale.shape), _const_spec(w_out.shape),
            _const_spec(ln_g.shape), _const_spec(ln_b.shape),
            _single_spec(w1.shape), _single_spec(w3.shape), _single_spec(w2.shape),
            _const_spec(ln_g2.shape), _const_spec(ln_b2.shape),
            _mod_spec(d, n_lat_tiles), _single_spec(w_next.shape),
        ],
        out_specs=[pl.BlockSpec((tm, d), lambda i: (i, 0)), pl.BlockSpec((tm, w_next.shape[1]), lambda i: (i, 0))],
        out_shape=[jax.ShapeDtypeStruct((t, d), F32), jax.ShapeDtypeStruct((t, w_next.shape[1]), BF16)],
        compiler_params=_params("arbitrary"),
        name="rwkv_readout_ffn",
    )(y_fwd, y_rev, bonus, g, pooled, x, mod, seg, lnx_g, lnx_b, pool_wbd, pool_scale, w_out, ln_g, ln_b,
      w1, w3, w2, ln_g2, ln_b2, mod_next, w_next)


ATTN_PV_LAG = 4
ATTN_ROWS_PER_STEP = 32


def _attn_kernel(q_ref, k_ref, v_ref, bias_ref, o_ref, *, q_rows, n_rows, n_lat, n_ctx, kh):
    rb = pl.program_id(1)
    head0 = lax.broadcasted_iota(jnp.int32, (1, LANES), 1) < HEAD
    kc = k_ref[n_lat:n_lat + n_ctx, :]
    vc = v_ref[n_lat:n_lat + n_ctx, :]
    n_loc = kh * GRID_W
    q_all = q_ref[...] * jnp.asarray(HEAD ** -0.5, q_ref.dtype)
    zero = jnp.zeros((GRID_W, LANES), q_all.dtype)

    rows = [dict() for _ in range(q_rows)]

    def scores(j):
        it = rows[j]
        rr = rb * q_rows + j
        sr = jnp.clip(rr - kh // 2, 0, n_rows - kh)
        t_var = sr - rr + (NA_KH - 1)
        q = q_all[j * GRID_W:(j + 1) * GRID_W]
        q2 = jnp.concatenate([jnp.where(head0, q, zero), jnp.where(head0, zero, q)], axis=0)
        it["krows"] = pl.ds(pl.multiple_of(sr * GRID_W, GRID_W), n_loc)
        bias = jnp.concatenate(
            [jnp.concatenate([bias_ref[hh, t_var + 2 * a] for a in range(kh // 2)], axis=1) for hh in range(2)],
            axis=0)
        it["s_loc"] = _dot_nt(q2, k_ref[it["krows"], :]) + bias
        it["s_ctx"] = _dot_nt(q2, kc)

    def softmax(j):
        it = rows[j]
        s_loc, s_ctx = it.pop("s_loc"), it.pop("s_ctx")
        m = jnp.maximum(jnp.max(s_loc, axis=-1, keepdims=True), jnp.max(s_ctx, axis=-1, keepdims=True))
        p_loc = jnp.exp(s_loc - m)
        p_ctx = jnp.exp(s_ctx - m)
        it["denom"] = jnp.sum(p_loc, axis=-1, keepdims=True) + jnp.sum(p_ctx, axis=-1, keepdims=True)
        it["p_loc"] = p_loc.astype(BF16)
        it["p_ctx"] = p_ctx.astype(BF16)

    def values(j):
        it = rows[j]
        o2 = (_dot(it.pop("p_loc"), v_ref[it["krows"], :]) + _dot(it.pop("p_ctx"), vc)) / it["denom"]
        o_ref[j * GRID_W:(j + 1) * GRID_W, :] = jnp.where(head0, o2[:GRID_W], o2[GRID_W:]).astype(o_ref.dtype)

    for j in range(q_rows + ATTN_PV_LAG):
        if j < q_rows:
            scores(j)
        if 0 <= j - 1 < q_rows:
            softmax(j - 1)
        if 0 <= j - ATTN_PV_LAG < q_rows:
            values(j - ATTN_PV_LAG)


def _neighbourhood_attention(qkv, bias_tab, *, n_ctx, n_lat):
    t_all = qkv.shape[0]
    d = D_MODEL
    n_rows = n_lat // GRID_W
    kh = min(NA_KH, n_rows)
    q_rows = ATTN_ROWS_PER_STEP
    q_tile = q_rows * GRID_W
    n_pairs = d // LANES
    kern = functools.partial(_attn_kernel, q_rows=q_rows, n_rows=n_rows, n_lat=n_lat, n_ctx=n_ctx, kh=kh)
    assert kh % 2 == 0
    return pl.pallas_call(
        kern,
        grid=(n_pairs, n_lat // q_tile),
        in_specs=[
            pl.BlockSpec((q_tile, LANES), lambda p, b: (b, p)),
            pl.BlockSpec((t_all, LANES), lambda p, b: (0, n_pairs + p)),
            pl.BlockSpec((t_all, LANES), lambda p, b: (0, 2 * n_pairs + p)),
            pl.BlockSpec((2,) + bias_tab.shape[1:], lambda p, b: (p, 0, 0, 0)),
        ],
        out_specs=pl.BlockSpec((q_tile, LANES), lambda p, b: (b, p)),
        out_shape=jax.ShapeDtypeStruct((n_lat, d), BF16),
        compiler_params=_params("arbitrary", "arbitrary"),
        name="neighbourhood_attention",
    )(qkv, qkv, qkv, bias_tab)


def _attention_bias_table(rpb):
    cols = np.arange(GRID_W)
    col_start = np.clip(cols - NA_KW // 2, 0, GRID_W - NA_KW)
    in_win = (cols[None, :] >= col_start[:, None]) & (cols[None, :] < col_start[:, None] + NA_KW)
    col_off = cols[None, :] - cols[:, None] + (NA_KW - 1)
    onehot = (col_off[None] == np.arange(2 * NA_KW - 1)[:, None, None]) & in_win[None]
    dense = jnp.einsum('hrj,jck->hrck', rpb, jnp.asarray(onehot, F32), precision=HIGHEST)
    dense = jnp.where(jnp.asarray(in_win)[None, None], dense, NEG_BIAS)
    return jnp.concatenate([dense[:, :-1], dense[:, 1:]], axis=-1).astype(F32)


def _proj_ffn_kernel(o_ref, x_ref, mod_ref, w_ref, lng_ref, lnb_ref, w1_ref, w3_ref, w2_ref, lng2_ref, lnb2_ref,
                     out_ref):
    m = mod_ref[0]
    y = _dot(o_ref[...], w_ref[...])
    x1 = _layer_norm(ALPHA * x_ref[...] + m[2:3] * y, lng_ref[...], lnb_ref[...])
    out_ref[...] = _ffn_sublayer(x1, m, w1_ref, w3_ref, w2_ref, lng2_ref, lnb2_ref)


def _attn_proj_ffn(o, x, mod, w, ln_g, ln_b, w1, w3, w2, ln_g2, ln_b2, *, tm):
    t, d = o.shape
    return pl.pallas_call(
        _proj_ffn_kernel,
        grid=(t // tm,),
        in_specs=[
            pl.BlockSpec((tm, d), lambda i: (i, 0)),
            pl.BlockSpec((tm, d), lambda i: (i, 0)),
            pl.BlockSpec((1, SUBLANES, d), lambda i: (1, 0, 0)),
            _single_spec(w.shape), _const_spec(ln_g.shape), _const_spec(ln_b.shape),
            _single_spec(w1.shape), _single_spec(w3.shape), _single_spec(w2.shape),
            _const_spec(ln_g2.shape), _const_spec(ln_b2.shape),
        ],
        out_specs=pl.BlockSpec((tm, d), lambda i: (i, 0)),
        out_shape=jax.ShapeDtypeStruct((t, d), F32),
        compiler_params=_params("arbitrary"),
        name="attn_proj_ffn",
    )(o, x, mod, w, ln_g, ln_b, w1, w3, w2, ln_g2, ln_b2)


def _block_diag(w):
    n, r, c = w.shape
    on_diag = (np.arange(n * r)[:, None] // r) == (np.arange(n * c)[None, :] // c)
    return jnp.where(jnp.asarray(on_diag), jnp.tile(w.reshape(n * r, c), (1, n)), 0.0)


def kernel(x, c, ctx, c_ctx, ada_w, ada_b, ln_g, ln_b, ffn_w1, ffn_w3, ffn_w2, ev_w_in, ev_shift_mu, ev_w0,
           ev_w2, ev_a0, ev_a2, ev_g2, ev_k_k, ev_k_a, ev_r_k, ev_lnx_g, ev_lnx_b, ev_pool_w, ev_pool_scale,
           ev_w_out, od_w_in, od_rpb, od_w_out):
    batch, n_lat, d = x.shape
    n_ctx = ctx.shape[1]
    assert batch == 1 and d == D_MODEL
    assert n_ctx % TILE_ROWS == 0 and n_lat % LAT_TILE_ROWS == 0 and n_lat % (ATTN_ROWS_PER_STEP * GRID_W) == 0
    n_lat_tiles = n_lat // TILE_ROWS

    cs = jnp.concatenate([c_ctx[None], c, jnp.zeros((SUBLANES - 2, d), F32)], axis=0)
    mod_all = _ada_modulation(cs, ada_w, ada_b)
    mod_all = mod_all[:, :2].reshape(DEPTH, 2, 6, d)
    mod_all = jnp.pad(mod_all, ((0, 0), (0, 0), (0, SUBLANES - 6), (0, 0)))

    mod = mod_all[0]
    seg = jnp.asarray(np.arange(LANES)[:, None] // HEAD == np.arange(LANES)[None, :] // HEAD, BF16)
    r, v, kn, ld, kd, bd, g, bonus, pooled, xa = _rwkv_prepare(
        x[0], ctx[0], mod, ev_w_in[0].astype(BF16), ev_shift_mu[0], ev_w0[0],
        _block_diag(ev_w2[0]).astype(BF16), ev_a0[0], _block_diag(ev_a2[0]).astype(BF16),
        ev_g2[0].astype(BF16), ev_k_k[0].reshape(1, D_A), ev_k_a[0].reshape(1, D_A),
        ev_r_k[0].reshape(1, D_A), seg)
    y_fwd, y_rev = _wkv_scan(r, v, kn, ld, kd, bd, n_lat_blocks=n_lat_tiles)
    pool_wbd = _block_diag(ev_pool_w[0])
    xa, qkv = _rwkv_readout_ffn(
        y_fwd, y_rev, bonus, g, pooled, xa, mod, seg, ev_lnx_g[0].reshape(1, D_A), ev_lnx_b[0].reshape(1, D_A),
        pool_wbd.astype(BF16), ev_pool_scale[0].reshape(1, D_B), ev_w_out[0].astype(BF16),
        ln_g[0, 0].reshape(1, d), ln_b[0, 0].reshape(1, d),
        ffn_w1[0].astype(BF16), ffn_w3[0].astype(BF16), ffn_w2[0].astype(BF16),
        ln_g[0, 1].reshape(1, d), ln_b[0, 1].reshape(1, d), mod_all[1], od_w_in[0].astype(BF16),
        n_lat_tiles=n_lat_tiles)

    mod = mod_all[1]
    bias_tab = _attention_bias_table(od_rpb[0])
    o = _neighbourhood_attention(qkv, bias_tab, n_ctx=n_ctx, n_lat=n_lat)
    xl = _attn_proj_ffn(
        o, xa, mod, od_w_out[0].astype(BF16), ln_g[1, 0].reshape(1, d), ln_b[1, 0].reshape(1, d),
        ffn_w1[1].astype(BF16), ffn_w3[1].astype(BF16), ffn_w2[1].astype(BF16),
        ln_g[1, 1].reshape(1, d), ln_b[1, 1].reshape(1, d), tm=LAT_TILE_ROWS)
    return xl[None]
```

```python
import functools
import math

import jax
import jax.numpy as jnp
import numpy as np
from jax import lax
from jax.experimental import pallas as pl
from jax.experimental.pallas import tpu as pltpu

F32 = jnp.float32
BF16 = jnp.bfloat16
HIGHEST = lax.Precision.HIGHEST

D_MODEL = 1024
DEPTH = 2
GRID_W = 64
ALPHA = (2 * DEPTH) ** 0.25
LN_EPS = 1e-6
HEAD = 64
D_A = 3 * D_MODEL // 4
H_A = D_A // HEAD
LORA_W = 64
LORA_A = 64
LORA_G = 128
GN_EPS = 64e-5
D_B = D_MODEL - D_A
POOL_WINDOWS = (2, 4, 8, 16)
POOL_GROUP = D_B // len(POOL_WINDOWS)
C_RW = 3 * D_A + 2 * LORA_W + 2 * LORA_A + LORA_G
D_IN_EVEN = C_RW + D_B
H_C = D_MODEL // HEAD
NA_KH = 8
NA_KW = 16

LANES = 128
SUBLANES = 8
VMEM_LIMIT_BYTES = 56 * 1024 * 1024

TILE_ROWS = 256
LAT_TILE_ROWS = 512
CHUNK = 64
NEG_BIAS = -1e30


def _sigmoid(x):
    return 1.0 / (1.0 + jnp.exp(-x))


def _layer_norm(z, g, b):
    mu = jnp.mean(z, axis=-1, keepdims=True)
    zc = z - mu
    var = jnp.mean(zc * zc, axis=-1, keepdims=True)
    return zc * lax.rsqrt(var + LN_EPS) * g + b


def _dot(a, b):
    return jnp.dot(a, b, preferred_element_type=F32)


def _dot_nt(a, b):
    return lax.dot_general(a, b, (((1,), (1,)), ((), ())), preferred_element_type=F32)


def _seg_sum(x, seg):
    xb = x.astype(BF16)
    return jnp.concatenate([_dot(xb[:, j:j + LANES], seg) for j in range(0, x.shape[1], LANES)], axis=1)


def _params(*sem):
    return pltpu.CompilerParams(dimension_semantics=sem, vmem_limit_bytes=VMEM_LIMIT_BYTES)


def _const_spec(shape):
    nd = len(shape)
    return pl.BlockSpec(shape, lambda *_: (0,) * nd)


def _mod_spec(d, n_lat_tiles):
    return pl.BlockSpec((1, SUBLANES, d), lambda i: ((i < n_lat_tiles).astype(jnp.int32), 0, 0))


N_COND = 2


def _ada_kernel(cst_ref, w_ref, b_ref, o_ref):
    w = w_ref[0]
    rows = []
    for r in range(N_COND):
        s = cst_ref[:, r:r + 1]
        rows.append(jnp.sum(w * (s * _sigmoid(s)), axis=0, keepdims=True) + b_ref[0])
    o_ref[0] = jnp.concatenate(rows + [jnp.zeros((SUBLANES - N_COND, w.shape[1]), F32)], axis=0)


def _ada_modulation(cs_t, ada_w, ada_b):
    depth, d, n = ada_w.shape
    nb = 1536
    return pl.pallas_call(
        _ada_kernel,
        grid=(depth, n // nb),
        in_specs=[
            pl.BlockSpec((d, LANES), lambda i, j: (0, 0)),
            pl.BlockSpec((1, d, nb), lambda i, j: (i, 0, j)),
            pl.BlockSpec((1, 1, nb), lambda i, j: (i, 0, j)),
        ],
        out_specs=pl.BlockSpec((1, SUBLANES, nb), lambda i, j: (i, 0, j)),
        out_shape=jax.ShapeDtypeStruct((depth, SUBLANES, n), F32),
        compiler_params=_params("arbitrary", "arbitrary"),
        name="ada_modulation",
    )(cs_t, ada_w, ada_b.reshape(depth, 1, n))


def _prep_kernel(x_ref, ctx_ref, xhp_ref, xhn_ref, chp_ref, chn_ref, mod_ref, w_ref, mu_ref, w0_ref, w2_ref,
                 a0_ref, a2_ref, g2_ref, kk_ref, ka_ref, rk_ref, seg_ref,
                 r_o, v_o, kn_o, ld_o, kd_o, bd_o, g_o, bonus_o, pool_o, xa_o, *, tm, n_lat_tiles, n_tiles):
    i = pl.program_id(0)
    is_lat = i < n_lat_tiles
    first = jnp.logical_or(i == 0, i == n_lat_tiles)
    last = jnp.logical_or(i == n_lat_tiles - 1, i == n_tiles - 1)
    row = lax.broadcasted_iota(jnp.int32, (tm, 1), 0)
    n_ext = tm + 2 * SUBLANES

    xt = jnp.where(is_lat, x_ref[...], ctx_ref[...])
    xa_o[...] = xt
    x_ext = jnp.concatenate([jnp.where(is_lat, xhp_ref[...], chp_ref[...]), xt,
                             jnp.where(is_lat, xhn_ref[...], chn_ref[...])], axis=0)
    m = mod_ref[0]
    h_ext = (x_ext * (1.0 + m[1:2]) + m[0:1]).astype(BF16)
    mu_prev = mu_ref[0:1]
    mu_next = mu_ref[1:2]
    mu_self = 1.0 - mu_prev - mu_next
    inner = slice(SUBLANES, SUBLANES + tm)

    def project(c0, c1):
        p = _dot(h_ext, w_ref[:, c0:c1])
        return jnp.concatenate([jnp.where(first, 0.0, p[:SUBLANES]), p[inner],
                                jnp.where(last, 0.0, p[SUBLANES + tm:])], axis=0)

    def shift_mix(p, c0, c1):
        return (p[inner] * mu_self[:, c0:c1] + pltpu.roll(p, 1, 0)[inner] * mu_prev[:, c0:c1]
                + pltpu.roll(p, n_ext - 1, 0)[inner] * mu_next[:, c0:c1])

    s1, s2, s3 = D_A, 2 * D_A, 3 * D_A
    s4 = s3 + 2 * LORA_W
    s5 = s4 + 2 * LORA_A
    lora = shift_mix(project(s3, C_RW), s3, C_RW)
    k_ext = project(s1, s2)
    lw = _dot(jnp.tanh(lora[:, :s4 - s3]).astype(BF16), w2_ref[...])
    la = _dot(lora[:, s4 - s3:s5 - s3].astype(BF16), a2_ref[...])
    g = _dot(_sigmoid(lora[:, s5 - s3:]).astype(BF16), g2_ref[...])
    g_o[...] = g.astype(g_o.dtype)

    r_ext = project(0, s1)
    k = shift_mix(k_ext, s1, s2)
    kr = k * kk_ref[...]
    ss = _seg_sum(kr * kr, seg_ref[...])
    kn = kr * lax.rsqrt(jnp.maximum(ss, 1e-24))
    kn_o[...] = kn.astype(kn_o.dtype)

    v_ext = project(s2, s3)
    decay_scale = math.exp(-0.5)
    kd_sum = jnp.zeros_like(k)
    for d in range(2):
        sl = slice(d * D_A, (d + 1) * D_A)
        ld_o[d] = -decay_scale * _sigmoid(w0_ref[d:d + 1] + lw[:, sl])
        a = _sigmoid(a0_ref[d:d + 1] + la[:, sl])
        kd = k * (1.0 + (a - 1.0) * ka_ref[...])
        kd_o[d] = kd.astype(kd_o.dtype)
        bd_o[d] = (kn * a).astype(bd_o.dtype)
        kd_sum = kd_sum + kd

    xe = project(C_RW, D_IN_EVEN)
    r = shift_mix(r_ext, 0, s1)
    r_o[...] = r.astype(r_o.dtype)
    coef = _seg_sum(r * kd_sum * rk_ref[...], seg_ref[...])
    v = shift_mix(v_ext, s2, s3)
    v_o[...] = v.astype(v_o.dtype)
    bonus_o[...] = (coef * v).astype(bonus_o.dtype)

    xp = xe[inner]

    def at_offset(a, s):
        return pltpu.roll(a, (-s) % n_ext, 0)

    sums = {}
    acc = xe + at_offset(xe, -1)
    sums[2] = acc
    half = 1
    for win in (4, 8, 16):
        acc = at_offset(acc, -half) + at_offset(acc, half)
        sums[win] = acc
        half *= 2

    in_ctx = i >= n_lat_tiles
    tile0 = jnp.where(in_ctx, n_lat_tiles, 0)
    seq_len = jnp.where(in_ctx, (n_tiles - n_lat_tiles) * tm, n_lat_tiles * tm)
    t_seq = (i - tile0) * tm + row
    grp = lax.broadcasted_iota(jnp.int32, (1, D_B), 1) // POOL_GROUP
    mean = jnp.zeros((tm, D_B), F32)
    for gi, win in enumerate(POOL_WINDOWS):
        cnt = jnp.minimum(t_seq + win // 2, seq_len) - jnp.maximum(t_seq - win // 2, 0)
        m = sums[win][SUBLANES:SUBLANES + tm] / cnt.astype(F32)
        mean = jnp.where(grp == gi, m, mean)
    pool_o[...] = mean - xp


def _rwkv_prepare(x, ctx, mod, w_in, mu, w0, w2bd, a0, a2bd, g2, k_k, k_a, r_k, seg):
    n_lat, d = x.shape
    n_ctx = ctx.shape[0]
    t = n_lat + n_ctx
    tm = TILE_ROWS
    n_tiles = t // tm
    n_lat_tiles = n_lat // tm
    hpt = tm // SUBLANES
    lat_halos = n_lat // SUBLANES
    ctx_halos = n_ctx // SUBLANES

    def lat_tile(i):
        return jnp.minimum(i, n_lat_tiles - 1)

    def ctx_tile(i):
        return jnp.maximum(i - n_lat_tiles, 0)

    halo = (SUBLANES, d)
    tok = jax.ShapeDtypeStruct((t, D_A), BF16)
    tok2 = jax.ShapeDtypeStruct((2, t, D_A), BF16)
    ld2 = jax.ShapeDtypeStruct((2, t, D_A), F32)
    tok_spec = pl.BlockSpec((tm, D_A), lambda i: (i, 0))
    tok2_spec = pl.BlockSpec((2, tm, D_A), lambda i: (0, i, 0))
    kern = functools.partial(_prep_kernel, tm=tm, n_lat_tiles=n_lat_tiles, n_tiles=n_tiles)
    return pl.pallas_call(
        kern,
        grid=(n_tiles,),
        in_specs=[
            pl.BlockSpec((tm, d), lambda i: (lat_tile(i), 0)),
            pl.BlockSpec((tm, d), lambda i: (ctx_tile(i), 0)),
            pl.BlockSpec(halo, lambda i: (jnp.maximum(lat_tile(i) * hpt - 1, 0), 0)),
            pl.BlockSpec(halo, lambda i: (jnp.minimum((lat_tile(i) + 1) * hpt, lat_halos - 1), 0)),
            pl.BlockSpec(halo, lambda i: (jnp.maximum(ctx_tile(i) * hpt - 1, 0), 0)),
            pl.BlockSpec(halo, lambda i: (jnp.minimum((ctx_tile(i) + 1) * hpt, ctx_halos - 1), 0)),
            _mod_spec(d, n_lat_tiles), _const_spec(w_in.shape),
            _const_spec(mu.shape), _const_spec(w0.shape), _const_spec(w2bd.shape), _const_spec(a0.shape),
            _const_spec(a2bd.shape), _const_spec(g2.shape), _const_spec(k_k.shape), _const_spec(k_a.shape),
            _const_spec(r_k.shape), _const_spec(seg.shape),
        ],
        out_specs=[tok_spec, tok_spec, tok_spec, tok2_spec, tok2_spec, tok2_spec, tok_spec, tok_spec,
                   pl.BlockSpec((tm, D_B), lambda i: (i, 0)), pl.BlockSpec((tm, d), lambda i: (i, 0))],
        out_shape=[tok, tok, tok, ld2, tok2, tok2, tok, tok, jax.ShapeDtypeStruct((t, D_B), F32),
                   jax.ShapeDtypeStruct((t, d), x.dtype)],
        compiler_params=_params("arbitrary"),
        name="rwkv_prepare",
    )(x, ctx, x, x, ctx, ctx, mod, w_in, mu, w0, w2bd, a0, a2bd, g2, k_k, k_a, r_k, seg)


WKV_PAIRS_PER_STEP = 6
WKV_STAGE_STAGGER = 1


def _wkv_kernel(rf_ref, vf_ref, knf_ref, ldf_ref, kdf_ref, bdf_ref, rr_ref, vr_ref, knr_ref, ldr_ref, kdr_ref,
                bdr_ref, yf_ref, yr_ref, h_ref, lhs_s, add_s, pb_s, ends_s, dec_s, v_s, *, chunks_per_block,
                pairs_per_step):
    step = pl.program_id(1)
    cur = lax.rem(step, 2)
    prev = 1 - cur

    @pl.when(step == 0)
    def _():
        h_ref[...] = jnp.zeros_like(h_ref)
        for ref in (lhs_s, add_s, pb_s, ends_s, dec_s, v_s):
            ref[1] = jnp.zeros(ref.shape[1:], ref.dtype)

    c = CHUNK
    ri = lax.broadcasted_iota(jnp.int32, (c, c), 0)
    ci = lax.broadcasted_iota(jnp.int32, (c, c), 1)
    rp = lax.broadcasted_iota(jnp.int32, (c, LANES), 0)
    cp = lax.broadcasted_iota(jnp.int32, (c, LANES), 1) & (c - 1)
    eye_p = jnp.where(rp == cp, 1.0, 0.0)
    r2 = lax.broadcasted_iota(jnp.int32, (LANES, LANES), 0)
    c2 = lax.broadcasted_iota(jnp.int32, (LANES, LANES), 1)
    same_head = (r2 < HEAD) == (c2 < HEAD)
    head0 = lax.broadcasted_iota(jnp.int32, (1, LANES), 1) < HEAD

    def bdiag(x):
        zero = jnp.zeros_like(x)
        return jnp.concatenate([jnp.where(head0, x, zero), jnp.where(head0, zero, x)], axis=0)

    masks = {}
    for rev in (False, True):
        sgn = -1 if rev else 1
        order = (rp - cp) * sgn
        masks[rev] = (jnp.where((ri - ci) * sgn >= 0, 1.0, 0.0).astype(BF16), order > 0, order >= 0)

    chains = []
    for rev, refs, y_ref in ((False, (rf_ref, vf_ref, knf_ref, ldf_ref, kdf_ref, bdf_ref), yf_ref),
                             (True, (rr_ref, vr_ref, knr_ref, ldr_ref, kdr_ref, bdr_ref), yr_ref)):
        for q in range(pairs_per_step):
            chains.append((rev, refs, y_ref, slice(q * LANES, (q + 1) * LANES)))

    insts = []
    for ch, (rev, refs, y_ref, lanes) in enumerate(chains):
        for j in range(chunks_per_block):
            cj = chunks_per_block - 1 - j if rev else j
            insts.append(dict(idx=len(insts), ch=ch, j=j, rev=rev, refs=refs, lanes=lanes,
                              rows=slice(cj * c, (cj + 1) * c)))
    by_key = {(it["ch"], it["j"]): it for it in insts}

    states = [h_ref[ch] for ch in range(len(chains))]
    pending = {}

    def state_group(k):
        j, second = divmod(k, 2)
        for ch, (rev, refs, y_ref, lanes) in enumerate(chains):
            it = by_key[(ch, j)]
            idx = it["idx"]
            if not second:
                pending[ch] = _dot(lhs_s[prev, idx], states[ch].astype(BF16))
            else:
                m1 = pending.pop(ch)
                add = add_s[prev, idx]
                ub = (-m1[:c] - add[:c]).astype(BF16)
                y_ref[it["rows"], lanes] = m1[c:] + add[c:] + _dot(pb_s[prev, idx], bdiag(ub))
                upd = _dot(ends_s[prev, idx], jnp.concatenate([ub, v_s[prev, idx]], axis=0))
                states[ch] = states[ch] * dec_s[prev, idx] + jnp.where(same_head, upd, 0.0)

    n_state_groups = 2 * chunks_per_block
    emitted = [0]

    def interleave_state_group():
        if emitted[0] < n_state_groups:
            state_group(emitted[0])
            emitted[0] += 1

    def stage_cumsum(group):
        for it in group:
            ld = it["refs"][3][0, it["rows"], it["lanes"]]
            tri = masks[it["rev"]][0]
            ld_hi = ld.astype(BF16)
            rem = ld - ld_hi.astype(F32)
            ld_mid = rem.astype(BF16)
            ld_lo = (rem - ld_mid.astype(F32)).astype(BF16)
            part = _dot(tri, jnp.concatenate([ld_hi, ld_mid], axis=1))
            it["cum"] = part[:, :LANES] + part[:, LANES:] + _dot(tri, ld_lo)

    def stage_gram(group):
        for it in group:
            r_ref, v_ref, kn_ref, ld_ref, kd_ref, bd_ref = it["refs"]
            rows, lanes = it["rows"], it["lanes"]
            _, strict, incl = masks[it["rev"]]
            ld = ld_ref[0, rows, lanes]
            kd = kd_ref[0, rows, lanes].astype(F32)
            bd = bd_ref[0, rows, lanes].astype(F32)
            cum = it.pop("cum")
            total = jnp.sum(ld, axis=0, keepdims=True)
            inv = jnp.exp(-cum)
            to_end = inv * jnp.exp(total)
            kq = (kn_ref[rows, lanes].astype(F32) * jnp.exp(cum - ld)).astype(BF16)
            rq = (r_ref[rows, lanes].astype(F32) * jnp.exp(cum)).astype(BF16)
            g = _dot_nt(jnp.concatenate([kq, rq], axis=0),
                        jnp.concatenate([bdiag((bd * inv).astype(BF16)), bdiag((kd * inv).astype(BF16))], axis=0))
            it["q"] = jnp.where(strict, -g[:c, :LANES], 0.0)
            it["bm"] = jnp.where(strict, g[:c, LANES:], 0.0).astype(BF16)
            it["pk"] = jnp.where(incl, g[c:, LANES:], 0.0).astype(BF16)
            it["kq_bd"], it["rq"] = bdiag(kq), rq
            v = v_ref[rows, lanes]
            it["v_bd"] = bdiag(v)
            idx = it["idx"]
            pb_s[cur, idx] = jnp.where(incl, g[c:, :LANES], 0.0).astype(BF16)
            v_s[cur, idx] = v
            ends_s[cur, idx] = jnp.concatenate([bd * to_end, kd * to_end], axis=0).T.astype(BF16)
            dec_s[cur, idx] = jnp.broadcast_to(jnp.exp(jnp.sum(ld.T, axis=1, keepdims=True)), (LANES, LANES))

    def stage_square(group):
        for it in group:
            q = it.pop("q")
            it["t_inv"] = eye_p + q
            qb = q.astype(BF16)
            it["pw"] = _dot(qb, bdiag(qb)).astype(BF16)

    n_levels = int(math.log2(c)) - 1

    def stage_level(level):
        def run(group):
            for it in group:
                pw = it["pw"]
                t_bd = bdiag(it["t_inv"].astype(BF16))
                if level < n_levels - 1:
                    res = _dot(pw, jnp.concatenate([bdiag(pw), t_bd], axis=1))
                    it["pw"] = res[:, :LANES].astype(BF16)
                    it["t_inv"] = it["t_inv"] + res[:, LANES:]
                else:
                    it["t_inv"] = it["t_inv"] + _dot(pw, t_bd)
        return run

    def stage_bmv(group):
        for it in group:
            it["bmv"] = _dot(it.pop("bm"), it["v_bd"]).astype(BF16)

    def stage_apply(group):
        for it in group:
            wu = _dot(it.pop("t_inv").astype(BF16),
                      jnp.concatenate([it.pop("kq_bd"), bdiag(it.pop("bmv"))], axis=1))
            lhs_s[cur, it["idx"]] = jnp.concatenate([wu[:, :LANES].astype(BF16), it.pop("rq")], axis=0)
            it["ut"] = wu[:, LANES:]

    def stage_pkv(group):
        for it in group:
            pkv = _dot(it.pop("pk"), it["v_bd"])
            add_s[cur, it["idx"]] = jnp.concatenate([it.pop("ut"), pkv], axis=0)

    stages = ([stage_cumsum, stage_gram, stage_square] + [stage_level(lv) for lv in range(n_levels)]
              + [stage_bmv, stage_apply, stage_pkv])
    half = len(insts) // 2
    groups = (insts[:half], insts[half:])
    for step_i in range(len(stages) + WKV_STAGE_STAGGER):
        if step_i < len(stages):
            stages[step_i](groups[0])
        if 0 <= step_i - WKV_STAGE_STAGGER < len(stages):
            stages[step_i - WKV_STAGE_STAGGER](groups[1])
        interleave_state_group()
    while emitted[0] < n_state_groups:
        interleave_state_group()
    for ch in range(len(chains)):
        h_ref[ch] = states[ch]


def _wkv_scan(r, v, kn, ld, kd, bd, *, n_lat_blocks):
    t, da = r.shape
    rows = TILE_ROWS
    n_blocks = t // rows
    pps = WKV_PAIRS_PER_STEP
    width = pps * LANES
    n_groups = da // width

    n_ctx_blocks = n_blocks - n_lat_blocks

    def fwd_blk(s):
        return jnp.where(s < n_ctx_blocks, n_lat_blocks + s, s - n_ctx_blocks)

    def rev_blk(s):
        return jnp.where(s < n_ctx_blocks, n_blocks - 1 - s, n_lat_blocks - 1 - (s - n_ctx_blocks))

    def cur(s):
        return jnp.minimum(s, n_blocks - 1)

    def prv(s):
        return jnp.maximum(s - 1, 0)

    fwd = pl.BlockSpec((rows, width), lambda g, s: (fwd_blk(cur(s)), g))
    bwd = pl.BlockSpec((rows, width), lambda g, s: (rev_blk(cur(s)), g))
    fwd_d = pl.BlockSpec((1, rows, width), lambda g, s: (0, fwd_blk(cur(s)), g))
    bwd_d = pl.BlockSpec((1, rows, width), lambda g, s: (1, rev_blk(cur(s)), g))
    y_fwd = pl.BlockSpec((rows, width), lambda g, s: (fwd_blk(prv(s)), g))
    y_bwd = pl.BlockSpec((rows, width), lambda g, s: (rev_blk(prv(s)), g))
    chunks = rows // CHUNK
    n_inst = 2 * pps * chunks
    kern = functools.partial(_wkv_kernel, chunks_per_block=chunks, pairs_per_step=pps)
    out = jax.ShapeDtypeStruct((t, da), F32)
    return pl.pallas_call(
        kern,
        grid=(n_groups, n_blocks + 1),
        in_specs=[fwd, fwd, fwd, fwd_d, fwd_d, fwd_d, bwd, bwd, bwd, bwd_d, bwd_d, bwd_d],
        out_specs=[y_fwd, y_bwd],
        out_shape=[out, out],
        scratch_shapes=[
            pltpu.VMEM((2 * pps, LANES, LANES), F32),
            pltpu.VMEM((2, n_inst, 2 * CHUNK, LANES), BF16),
            pltpu.VMEM((2, n_inst, 2 * CHUNK, LANES), F32),
            pltpu.VMEM((2, n_inst, CHUNK, LANES), BF16),
            pltpu.VMEM((2, n_inst, LANES, LANES), BF16),
            pltpu.VMEM((2, n_inst, LANES, LANES), F32),
            pltpu.VMEM((2, n_inst, CHUNK, LANES), BF16),
        ],
        compiler_params=_params("arbitrary", "arbitrary"),
        name="wkv_scan",
    )(r, v, kn, ld, kd, bd, r, v, kn, ld, kd, bd)


def _ffn_sublayer(x, m, w1_ref, w3_ref, w2_ref, lng_ref, lnb_ref):
    h = (x * (1.0 + m[4:5]) + m[3:4]).astype(BF16)
    a = _dot(h, w1_ref[...])
    b = _dot(h, w3_ref[...])
    act = (a * _sigmoid(a) * b).astype(BF16)
    out = _dot(act, w2_ref[...])
    return _layer_norm(ALPHA * x + m[5:6] * out, lng_ref[...], lnb_ref[...])


def _readout_kernel(yf_ref, yr_ref, bonus_ref, g_ref, pool_ref, x_ref, mod_ref, seg_ref, lnxg_ref, lnxb_ref,
                    poolw_ref, pools_ref, wout_ref, lng_ref, lnb_ref, w1_ref, w3_ref, w2_ref, lng2_ref, lnb2_ref,
                    mod_next_ref, w_next_ref, o_ref, proj_ref):
    y = yf_ref[...] + yr_ref[...]
    seg = seg_ref[...]
    mu = _seg_sum(y, seg) * (1.0 / HEAD)
    yc = y - mu
    var = _seg_sum(yc * yc, seg) * (1.0 / HEAD)
    yn = yc * lax.rsqrt(var + GN_EPS) * lnxg_ref[...] + lnxb_ref[...]
    a = (yn + bonus_ref[...].astype(F32)) * g_ref[...].astype(F32)
    b = _dot(pool_ref[...].astype(BF16), poolw_ref[...]) * pools_ref[...]
    out = _dot(a.astype(BF16), wout_ref[:D_A, :]) + _dot(b.astype(BF16), wout_ref[D_A:, :])
    m = mod_ref[0]
    x1 = _layer_norm(ALPHA * x_ref[...] + m[2:3] * out, lng_ref[...], lnb_ref[...])
    x2 = _ffn_sublayer(x1, m, w1_ref, w3_ref, w2_ref, lng2_ref, lnb2_ref)
    o_ref[...] = x2
    mn = mod_next_ref[0]
    proj_ref[...] = _dot((x2 * (1.0 + mn[1:2]) + mn[0:1]).astype(BF16), w_next_ref[...]).astype(proj_ref.dtype)


def _single_spec(shape):
    nd = len(shape)
    return pl.BlockSpec(shape, lambda *_: (0,) * nd, pipeline_mode=pl.Buffered(1))


def _rwkv_readout_ffn(y_fwd, y_rev, bonus, g, pooled, x, mod, seg, lnx_g, lnx_b, pool_wbd, pool_scale, w_out, ln_g,
                      ln_b, w1, w3, w2, ln_g2, ln_b2, mod_next, w_next, *, n_lat_tiles):
    t, d = x.shape
    tm = TILE_ROWS
    tok = pl.BlockSpec((tm, D_A), lambda i: (i, 0))
    return pl.pallas_call(
        _readout_kernel,
        grid=(t // tm,),
        in_specs=[
            tok, tok, tok, tok,
            pl.BlockSpec((tm, D_B), lambda i: (i, 0)),
            pl.BlockSpec((tm, d), lambda i: (i, 0)),
            _mod_spec(d, n_lat_tiles),
            _const_spec(seg.shape), _const_spec(lnx_g.shape), _const_spec(lnx_b.shape),
            _const_spec(pool_wbd.shape), _const_spec(pool_scale.shape), _const_spec(w_out.shape),
            _const_spec(ln_g.shape), _const_spec(ln_b.shape),
            _single_spec(w1.shape), _single_spec(w3.shape), _single_spec(w2.shape),
            _const_spec(ln_g2.shape), _const_spec(ln_b2.shape),
            _mod_spec(d, n_lat_tiles), _single_spec(w_next.shape),
        ],
        out_specs=[pl.BlockSpec((tm, d), lambda i: (i, 0)), pl.BlockSpec((tm, w_next.shape[1]), lambda i: (i, 0))],
        out_shape=[jax.ShapeDtypeStruct((t, d), F32), jax.ShapeDtypeStruct((t, w_next.shape[1]), BF16)],
        compiler_params=_params("arbitrary"),
        name="rwkv_readout_ffn",
    )(y_fwd, y_rev, bonus, g, pooled, x, mod, seg, lnx_g, lnx_b, pool_wbd, pool_scale, w_out, ln_g, ln_b,
      w1, w3, w2, ln_g2, ln_b2, mod_next, w_next)


ATTN_PV_LAG = 4
ATTN_ROWS_PER_STEP = 32


def _attn_kernel(q_ref, k_ref, v_ref, bias_ref, o_ref, *, q_rows, n_rows, n_lat, n_ctx, kh):
    rb = pl.program_id(1)
    head0 = lax.broadcasted_iota(jnp.int32, (1, LANES), 1) < HEAD
    kc = k_ref[n_lat:n_lat + n_ctx, :]
    vc = v_ref[n_lat:n_lat + n_ctx, :]
    n_loc = kh * GRID_W
    q_all = q_ref[...] * jnp.asarray(HEAD ** -0.5, q_ref.dtype)
    zero = jnp.zeros((GRID_W, LANES), q_all.dtype)

    rows = [dict() for _ in range(q_rows)]

    def scores(j):
        it = rows[j]
        rr = rb * q_rows + j
        sr = jnp.clip(rr - kh // 2, 0, n_rows - kh)
        t_var = sr - rr + (NA_KH - 1)
        q = q_all[j * GRID_W:(j + 1) * GRID_W]
        q2 = jnp.concatenate([jnp.where(head0, q, zero), jnp.where(head0, zero, q)], axis=0)
        it["krows"] = pl.ds(pl.multiple_of(sr * GRID_W, GRID_W), n_loc)
        bias = jnp.concatenate(
            [jnp.concatenate([bias_ref[hh, t_var + 2 * a] for a in range(kh // 2)], axis=1) for hh in range(2)],
            axis=0)
        it["s_loc"] = _dot_nt(q2, k_ref[it["krows"], :]) + bias
        it["s_ctx"] = _dot_nt(q2, kc)

    def softmax(j):
        it = rows[j]
        s_loc, s_ctx = it.pop("s_loc"), it.pop("s_ctx")
        m = jnp.maximum(jnp.max(s_loc, axis=-1, keepdims=True), jnp.max(s_ctx, axis=-1, keepdims=True))
        p_loc = jnp.exp(s_loc - m)
        p_ctx = jnp.exp(s_ctx - m)
        it["denom"] = jnp.sum(p_loc, axis=-1, keepdims=True) + jnp.sum(p_ctx, axis=-1, keepdims=True)
        it["p_loc"] = p_loc.astype(BF16)
        it["p_ctx"] = p_ctx.astype(BF16)

    def values(j):
        it = rows[j]
        o2 = (_dot(it.pop("p_loc"), v_ref[it["krows"], :]) + _dot(it.pop("p_ctx"), vc)) / it["denom"]
        o_ref[j * GRID_W:(j + 1) * GRID_W, :] = jnp.where(head0, o2[:GRID_W], o2[GRID_W:]).astype(o_ref.dtype)

    for j in range(q_rows + ATTN_PV_LAG):
        if j < q_rows:
            scores(j)
        if 0 <= j - 1 < q_rows:
            softmax(j - 1)
        if 0 <= j - ATTN_PV_LAG < q_rows:
            values(j - ATTN_PV_LAG)


def _neighbourhood_attention(qkv, bias_tab, *, n_ctx, n_lat):
    t_all = qkv.shape[0]
    d = D_MODEL
    n_rows = n_lat // GRID_W
    kh = min(NA_KH, n_rows)
    q_rows = ATTN_ROWS_PER_STEP
    q_tile = q_rows * GRID_W
    n_pairs = d // LANES
    kern = functools.partial(_attn_kernel, q_rows=q_rows, n_rows=n_rows, n_lat=n_lat, n_ctx=n_ctx, kh=kh)
    assert kh % 2 == 0
    return pl.pallas_call(
        kern,
        grid=(n_pairs, n_lat // q_tile),
        in_specs=[
            pl.BlockSpec((q_tile, LANES), lambda p, b: (b, p)),
            pl.BlockSpec((t_all, LANES), lambda p, b: (0, n_pairs + p)),
            pl.BlockSpec((t_all, LANES), lambda p, b: (0, 2 * n_pairs + p)),
            pl.BlockSpec((2,) + bias_tab.shape[1:], lambda p, b: (p, 0, 0, 0)),
        ],
        out_specs=pl.BlockSpec((q_tile, LANES), lambda p, b: (b, p)),
        out_shape=jax.ShapeDtypeStruct((n_lat, d), BF16),
        compiler_params=_params("arbitrary", "arbitrary"),
        name="neighbourhood_attention",
    )(qkv, qkv, qkv, bias_tab)


def _attention_bias_table(rpb):
    cols = np.arange(GRID_W)
    col_start = np.clip(cols - NA_KW // 2, 0, GRID_W - NA_KW)
    in_win = (cols[None, :] >= col_start[:, None]) & (cols[None, :] < col_start[:, None] + NA_KW)
    col_off = cols[None, :] - cols[:, None] + (NA_KW - 1)
    onehot = (col_off[None] == np.arange(2 * NA_KW - 1)[:, None, None]) & in_win[None]
    dense = jnp.einsum('hrj,jck->hrck', rpb, jnp.asarray(onehot, F32), precision=HIGHEST)
    dense = jnp.where(jnp.asarray(in_win)[None, None], dense, NEG_BIAS)
    return jnp.concatenate([dense[:, :-1], dense[:, 1:]], axis=-1).astype(F32)


def _proj_ffn_kernel(o_ref, x_ref, mod_ref, w_ref, lng_ref, lnb_ref, w1_ref, w3_ref, w2_ref, lng2_ref, lnb2_ref,
                     out_ref):
    m = mod_ref[0]
    y = _dot(o_ref[...], w_ref[...])
    x1 = _layer_norm(ALPHA * x_ref[...] + m[2:3] * y, lng_ref[...], lnb_ref[...])
    out_ref[...] = _ffn_sublayer(x1, m, w1_ref, w3_ref, w2_ref, lng2_ref, lnb2_ref)


def _attn_proj_ffn(o, x, mod, w, ln_g, ln_b, w1, w3, w2, ln_g2, ln_b2, *, tm):
    t, d = o.shape
    return pl.pallas_call(
        _proj_ffn_kernel,
        grid=(t // tm,),
        in_specs=[
            pl.BlockSpec((tm, d), lambda i: (i, 0)),
            pl.BlockSpec((tm, d), lambda i: (i, 0)),
            pl.BlockSpec((1, SUBLANES, d), lambda i: (1, 0, 0)),
            _single_spec(w.shape), _const_spec(ln_g.shape), _const_spec(ln_b.shape),
            _single_spec(w1.shape), _single_spec(w3.shape), _single_spec(w2.shape),
            _const_spec(ln_g2.shape), _const_spec(ln_b2.shape),
        ],
        out_specs=pl.BlockSpec((tm, d), lambda i: (i, 0)),
        out_shape=jax.ShapeDtypeStruct((t, d), F32),
        compiler_params=_params("arbitrary"),
        name="attn_proj_ffn",
    )(o, x, mod, w, ln_g, ln_b, w1, w3, w2, ln_g2, ln_b2)


def _block_diag(w):
    n, r, c = w.shape
    on_diag = (np.arange(n * r)[:, None] // r) == (np.arange(n * c)[None, :] // c)
    return jnp.where(jnp.asarray(on_diag), jnp.tile(w.reshape(n * r, c), (1, n)), 0.0)


def kernel(x, c, ctx, c_ctx, ada_w, ada_b, ln_g, ln_b, ffn_w1, ffn_w3, ffn_w2, ev_w_in, ev_shift_mu, ev_w0,
           ev_w2, ev_a0, ev_a2, ev_g2, ev_k_k, ev_k_a, ev_r_k, ev_lnx_g, ev_lnx_b, ev_pool_w, ev_pool_scale,
           ev_w_out, od_w_in, od_rpb, od_w_out):
    batch, n_lat, d = x.shape
    n_ctx = ctx.shape[1]
    assert batch == 1 and d == D_MODEL
    assert n_ctx % TILE_ROWS == 0 and n_lat % LAT_TILE_ROWS == 0 and n_lat % (ATTN_ROWS_PER_STEP * GRID_W) == 0
    n_lat_tiles = n_lat // TILE_ROWS

    cs_t = jnp.concatenate([c_ctx[:, None], c.T, jnp.zeros((d, LANES - N_COND), F32)], axis=1)
    mod_all = _ada_modulation(cs_t, ada_w, ada_b)
    mod_all = mod_all[:, :N_COND].reshape(DEPTH, N_COND, 6, d)
    mod_all = jnp.pad(mod_all, ((0, 0), (0, 0), (0, SUBLANES - 6), (0, 0)))

    mod = mod_all[0]
    seg = jnp.asarray(np.arange(LANES)[:, None] // HEAD == np.arange(LANES)[None, :] // HEAD, BF16)
    r, v, kn, ld, kd, bd, g, bonus, pooled, xa = _rwkv_prepare(
        x[0], ctx[0], mod, ev_w_in[0].astype(BF16), ev_shift_mu[0], ev_w0[0],
        _block_diag(ev_w2[0]).astype(BF16), ev_a0[0], _block_diag(ev_a2[0]).astype(BF16),
        ev_g2[0].astype(BF16), ev_k_k[0].reshape(1, D_A), ev_k_a[0].reshape(1, D_A),
        ev_r_k[0].reshape(1, D_A), seg)
    y_fwd, y_rev = _wkv_scan(r, v, kn, ld, kd, bd, n_lat_blocks=n_lat_tiles)
    pool_wbd = _block_diag(ev_pool_w[0])
    xa, qkv = _rwkv_readout_ffn(
        y_fwd, y_rev, bonus, g, pooled, xa, mod, seg, ev_lnx_g[0].reshape(1, D_A), ev_lnx_b[0].reshape(1, D_A),
        pool_wbd.astype(BF16), ev_pool_scale[0].reshape(1, D_B), ev_w_out[0].astype(BF16),
        ln_g[0, 0].reshape(1, d), ln_b[0, 0].reshape(1, d),
        ffn_w1[0].astype(BF16), ffn_w3[0].astype(BF16), ffn_w2[0].astype(BF16),
        ln_g[0, 1].reshape(1, d), ln_b[0, 1].reshape(1, d), mod_all[1], od_w_in[0].astype(BF16),
        n_lat_tiles=n_lat_tiles)

    mod = mod_all[1]
    bias_tab = _attention_bias_table(od_rpb[0])
    o = _neighbourhood_attention(qkv, bias_tab, n_ctx=n_ctx, n_lat=n_lat)
    xl = _attn_proj_ffn(
        o, xa, mod, od_w_out[0].astype(BF16), ln_g[1, 0].reshape(1, d), ln_b[1, 0].reshape(1, d),
        ffn_w1[1].astype(BF16), ffn_w3[1].astype(BF16), ffn_w2[1].astype(BF16),
        ln_g[1, 1].reshape(1, d), ln_b[1, 1].reshape(1, d), tm=LAT_TILE_ROWS)
    return xl[None]
```

```python
import functools
import math

import jax
import jax.numpy as jnp
import numpy as np
from jax import lax
from jax.experimental import pallas as pl
from jax.experimental.pallas import tpu as pltpu

F32 = jnp.float32
BF16 = jnp.bfloat16
HIGHEST = lax.Precision.HIGHEST

D_MODEL = 1024
DEPTH = 2
GRID_W = 64
ALPHA = (2 * DEPTH) ** 0.25
LN_EPS = 1e-6
HEAD = 64
D_A = 3 * D_MODEL // 4
H_A = D_A // HEAD
LORA_W = 64
LORA_A = 64
LORA_G = 128
GN_EPS = 64e-5
D_B = D_MODEL - D_A
POOL_WINDOWS = (2, 4, 8, 16)
POOL_GROUP = D_B // len(POOL_WINDOWS)
C_RW = 3 * D_A + 2 * LORA_W + 2 * LORA_A + LORA_G
D_IN_EVEN = C_RW + D_B
H_C = D_MODEL // HEAD
NA_KH = 8
NA_KW = 16

LANES = 128
SUBLANES = 8
VMEM_LIMIT_BYTES = 56 * 1024 * 1024

TILE_ROWS = 256
LAT_TILE_ROWS = 512
CHUNK = 64
NEG_BIAS = -1e30


def _sigmoid(x):
    return 1.0 / (1.0 + jnp.exp(-x))


def _layer_norm(z, g, b):
    mu = jnp.mean(z, axis=-1, keepdims=True)
    zc = z - mu
    var = jnp.mean(zc * zc, axis=-1, keepdims=True)
    return zc * lax.rsqrt(var + LN_EPS) * g + b


def _dot(a, b):
    return jnp.dot(a, b, preferred_element_type=F32)


def _dot_nt(a, b):
    return lax.dot_general(a, b, (((1,), (1,)), ((), ())), preferred_element_type=F32)


def _seg_sum(x, seg):
    xb = x.astype(BF16)
    return jnp.concatenate([_dot(xb[:, j:j + LANES], seg) for j in range(0, x.shape[1], LANES)], axis=1)


def _params(*sem):
    return pltpu.CompilerParams(dimension_semantics=sem, vmem_limit_bytes=VMEM_LIMIT_BYTES)


def _const_spec(shape):
    nd = len(shape)
    return pl.BlockSpec(shape, lambda *_: (0,) * nd)


def _mod_spec(d, n_lat_tiles):
    return pl.BlockSpec((1, SUBLANES, d), lambda i: ((i < n_lat_tiles).astype(jnp.int32), 0, 0))


N_COND = 2


def _ada_kernel(cst_ref, w_ref, b_ref, o_ref):
    w = w_ref[0]
    rows = []
    for r in range(N_COND):
        s = cst_ref[:, r:r + 1]
        rows.append(jnp.sum(w * (s * _sigmoid(s)), axis=0, keepdims=True) + b_ref[0])
    o_ref[0] = jnp.concatenate(rows + [jnp.zeros((SUBLANES - N_COND, w.shape[1]), F32)], axis=0)


def _ada_modulation(cs_t, ada_w, ada_b):
    depth, d, n = ada_w.shape
    nb = 1536
    return pl.pallas_call(
        _ada_kernel,
        grid=(depth, n // nb),
        in_specs=[
            pl.BlockSpec((d, LANES), lambda i, j: (0, 0)),
            pl.BlockSpec((1, d, nb), lambda i, j: (i, 0, j)),
            pl.BlockSpec((1, 1, nb), lambda i, j: (i, 0, j)),
        ],
        out_specs=pl.BlockSpec((1, SUBLANES, nb), lambda i, j: (i, 0, j)),
        out_shape=jax.ShapeDtypeStruct((depth, SUBLANES, n), F32),
        compiler_params=_params("arbitrary", "arbitrary"),
        name="ada_modulation",
    )(cs_t, ada_w, ada_b.reshape(depth, 1, n))


def _prep_kernel(x_ref, ctx_ref, xhp_ref, xhn_ref, chp_ref, chn_ref, mod_ref, w_ref, mu_ref, w0_ref, w2_ref,
                 a0_ref, a2_ref, g2_ref, kk_ref, ka_ref, rk_ref, seg_ref,
                 r_o, v_o, kn_o, ld_o, kd_o, bd_o, g_o, bonus_o, pool_o, xa_o, *, tm, n_lat_tiles, n_tiles):
    i = pl.program_id(0)
    is_lat = i < n_lat_tiles
    first = jnp.logical_or(i == 0, i == n_lat_tiles)
    last = jnp.logical_or(i == n_lat_tiles - 1, i == n_tiles - 1)
    row = lax.broadcasted_iota(jnp.int32, (tm, 1), 0)
    n_ext = tm + 2 * SUBLANES

    xt = jnp.where(is_lat, x_ref[...], ctx_ref[...])
    xa_o[...] = xt
    x_ext = jnp.concatenate([jnp.where(is_lat, xhp_ref[...], chp_ref[...]), xt,
                             jnp.where(is_lat, xhn_ref[...], chn_ref[...])], axis=0)
    m = mod_ref[0]
    h_ext = (x_ext * (1.0 + m[1:2]) + m[0:1]).astype(BF16)
    mu_prev = mu_ref[0:1]
    mu_next = mu_ref[1:2]
    mu_self = 1.0 - mu_prev - mu_next
    inner = slice(SUBLANES, SUBLANES + tm)

    def project(c0, c1):
        p = _dot(h_ext, w_ref[:, c0:c1])
        return jnp.concatenate([jnp.where(first, 0.0, p[:SUBLANES]), p[inner],
                                jnp.where(last, 0.0, p[SUBLANES + tm:])], axis=0)

    def shift_mix(p, c0, c1):
        return (p[inner] * mu_self[:, c0:c1] + pltpu.roll(p, 1, 0)[inner] * mu_prev[:, c0:c1]
                + pltpu.roll(p, n_ext - 1, 0)[inner] * mu_next[:, c0:c1])

    s1, s2, s3 = D_A, 2 * D_A, 3 * D_A
    s4 = s3 + 2 * LORA_W
    s5 = s4 + 2 * LORA_A
    lora = shift_mix(project(s3, C_RW), s3, C_RW)
    k_ext = project(s1, s2)
    lw = _dot(jnp.tanh(lora[:, :s4 - s3]).astype(BF16), w2_ref[...])
    la = _dot(lora[:, s4 - s3:s5 - s3].astype(BF16), a2_ref[...])
    g = _dot(_sigmoid(lora[:, s5 - s3:]).astype(BF16), g2_ref[...])
    g_o[...] = g.astype(g_o.dtype)

    r_ext = project(0, s1)
    k = shift_mix(k_ext, s1, s2)
    kr = k * kk_ref[...]
    ss = _seg_sum(kr * kr, seg_ref[...])
    kn = kr * lax.rsqrt(jnp.maximum(ss, 1e-24))
    kn_o[...] = kn.astype(kn_o.dtype)

    v_ext = project(s2, s3)
    decay_scale = math.exp(-0.5)
    kd_sum = jnp.zeros_like(k)
    for d in range(2):
        sl = slice(d * D_A, (d + 1) * D_A)
        ld_o[d] = -decay_scale * _sigmoid(w0_ref[d:d + 1] + lw[:, sl])
        a = _sigmoid(a0_ref[d:d + 1] + la[:, sl])
        kd = k * (1.0 + (a - 1.0) * ka_ref[...])
        kd_o[d] = kd.astype(kd_o.dtype)
        bd_o[d] = (kn * a).astype(bd_o.dtype)
        kd_sum = kd_sum + kd

    xe = project(C_RW, D_IN_EVEN)
    r = shift_mix(r_ext, 0, s1)
    r_o[...] = r.astype(r_o.dtype)
    coef = _seg_sum(r * kd_sum * rk_ref[...], seg_ref[...])
    v = shift_mix(v_ext, s2, s3)
    v_o[...] = v.astype(v_o.dtype)
    bonus_o[...] = (coef * v).astype(bonus_o.dtype)

    xp = xe[inner]

    def at_offset(a, s):
        return pltpu.roll(a, (-s) % n_ext, 0)

    sums = {}
    acc = xe + at_offset(xe, -1)
    sums[2] = acc
    half = 1
    for win in (4, 8, 16):
        acc = at_offset(acc, -half) + at_offset(acc, half)
        sums[win] = acc
        half *= 2

    in_ctx = i >= n_lat_tiles
    tile0 = jnp.where(in_ctx, n_lat_tiles, 0)
    seq_len = jnp.where(in_ctx, (n_tiles - n_lat_tiles) * tm, n_lat_tiles * tm)
    t_seq = (i - tile0) * tm + row
    grp = lax.broadcasted_iota(jnp.int32, (1, D_B), 1) // POOL_GROUP
    mean = jnp.zeros((tm, D_B), F32)
    for gi, win in enumerate(POOL_WINDOWS):
        cnt = jnp.minimum(t_seq + win // 2, seq_len) - jnp.maximum(t_seq - win // 2, 0)
        m = sums[win][SUBLANES:SUBLANES + tm] / cnt.astype(F32)
        mean = jnp.where(grp == gi, m, mean)
    pool_o[...] = mean - xp


def _rwkv_prepare(x, ctx, mod, w_in, mu, w0, w2bd, a0, a2bd, g2, k_k, k_a, r_k, seg):
    n_lat, d = x.shape
    n_ctx = ctx.shape[0]
    t = n_lat + n_ctx
    tm = TILE_ROWS
    n_tiles = t // tm
    n_lat_tiles = n_lat // tm
    hpt = tm // SUBLANES
    lat_halos = n_lat // SUBLANES
    ctx_halos = n_ctx // SUBLANES

    def lat_tile(i):
        return jnp.minimum(i, n_lat_tiles - 1)

    def ctx_tile(i):
        return jnp.maximum(i - n_lat_tiles, 0)

    halo = (SUBLANES, d)
    tok = jax.ShapeDtypeStruct((t, D_A), BF16)
    tok2 = jax.ShapeDtypeStruct((2, t, D_A), BF16)
    ld2 = jax.ShapeDtypeStruct((2, t, D_A), F32)
    tok_spec = pl.BlockSpec((tm, D_A), lambda i: (i, 0))
    tok2_spec = pl.BlockSpec((2, tm, D_A), lambda i: (0, i, 0))
    kern = functools.partial(_prep_kernel, tm=tm, n_lat_tiles=n_lat_tiles, n_tiles=n_tiles)
    return pl.pallas_call(
        kern,
        grid=(n_tiles,),
        in_specs=[
            pl.BlockSpec((tm, d), lambda i: (lat_tile(i), 0)),
            pl.BlockSpec((tm, d), lambda i: (ctx_tile(i), 0)),
            pl.BlockSpec(halo, lambda i: (jnp.maximum(lat_tile(i) * hpt - 1, 0), 0)),
            pl.BlockSpec(halo, lambda i: (jnp.minimum((lat_tile(i) + 1) * hpt, lat_halos - 1), 0)),
            pl.BlockSpec(halo, lambda i: (jnp.maximum(ctx_tile(i) * hpt - 1, 0), 0)),
            pl.BlockSpec(halo, lambda i: (jnp.minimum((ctx_tile(i) + 1) * hpt, ctx_halos - 1), 0)),
            _mod_spec(d, n_lat_tiles), _const_spec(w_in.shape),
            _const_spec(mu.shape), _const_spec(w0.shape), _const_spec(w2bd.shape), _const_spec(a0.shape),
            _const_spec(a2bd.shape), _const_spec(g2.shape), _const_spec(k_k.shape), _const_spec(k_a.shape),
            _const_spec(r_k.shape), _const_spec(seg.shape),
        ],
        out_specs=[tok_spec, tok_spec, tok_spec, tok2_spec, tok2_spec, tok2_spec, tok_spec, tok_spec,
                   pl.BlockSpec((tm, D_B), lambda i: (i, 0)), pl.BlockSpec((tm, d), lambda i: (i, 0))],
        out_shape=[tok, tok, tok, ld2, tok2, tok2, tok, tok, jax.ShapeDtypeStruct((t, D_B), F32),
                   jax.ShapeDtypeStruct((t, d), x.dtype)],
        compiler_params=_params("arbitrary"),
        name="rwkv_prepare",
    )(x, ctx, x, x, ctx, ctx, mod, w_in, mu, w0, w2bd, a0, a2bd, g2, k_k, k_a, r_k, seg)


WKV_PAIRS_PER_STEP = 6
WKV_STAGE_STAGGER = 1


def _wkv_kernel(rf_ref, vf_ref, knf_ref, ldf_ref, kdf_ref, bdf_ref, rr_ref, vr_ref, knr_ref, ldr_ref, kdr_ref,
                bdr_ref, yf_ref, yr_ref, h_ref, lhs_s, add_s, pb_s, ends_s, dec_s, v_s, *, chunks_per_block,
                pairs_per_step):
    step = pl.program_id(1)
    cur = lax.rem(step, 2)
    prev = 1 - cur

    @pl.when(step == 0)
    def _():
        h_ref[...] = jnp.zeros_like(h_ref)
        for ref in (lhs_s, add_s, pb_s, ends_s, dec_s, v_s):
            ref[1] = jnp.zeros(ref.shape[1:], ref.dtype)

    c = CHUNK
    ri = lax.broadcasted_iota(jnp.int32, (c, c), 0)
    ci = lax.broadcasted_iota(jnp.int32, (c, c), 1)
    rp = lax.broadcasted_iota(jnp.int32, (c, LANES), 0)
    cp = lax.broadcasted_iota(jnp.int32, (c, LANES), 1) & (c - 1)
    eye_p = jnp.where(rp == cp, 1.0, 0.0)
    r2 = lax.broadcasted_iota(jnp.int32, (LANES, LANES), 0)
    c2 = lax.broadcasted_iota(jnp.int32, (LANES, LANES), 1)
    same_head = (r2 < HEAD) == (c2 < HEAD)
    head0 = lax.broadcasted_iota(jnp.int32, (1, LANES), 1) < HEAD

    def bdiag(x):
        zero = jnp.zeros_like(x)
        return jnp.concatenate([jnp.where(head0, x, zero), jnp.where(head0, zero, x)], axis=0)

    masks = {}
    for rev in (False, True):
        sgn = -1 if rev else 1
        order = (rp - cp) * sgn
        masks[rev] = (jnp.where((ri - ci) * sgn >= 0, 1.0, 0.0).astype(BF16), order > 0, order >= 0)

    chains = []
    for rev, refs, y_ref in ((False, (rf_ref, vf_ref, knf_ref, ldf_ref, kdf_ref, bdf_ref), yf_ref),
                             (True, (rr_ref, vr_ref, knr_ref, ldr_ref, kdr_ref, bdr_ref), yr_ref)):
        for q in range(pairs_per_step):
            chains.append((rev, refs, y_ref, slice(q * LANES, (q + 1) * LANES)))

    insts = []
    for ch, (rev, refs, y_ref, lanes) in enumerate(chains):
        for j in range(chunks_per_block):
            cj = chunks_per_block - 1 - j if rev else j
            insts.append(dict(idx=len(insts), ch=ch, j=j, rev=rev, refs=refs, lanes=lanes,
                              rows=slice(cj * c, (cj + 1) * c)))
    by_key = {(it["ch"], it["j"]): it for it in insts}

    states = [h_ref[ch] for ch in range(len(chains))]
    pending = {}

    def state_group(k):
        j, second = divmod(k, 2)
        for ch, (rev, refs, y_ref, lanes) in enumerate(chains):
            it = by_key[(ch, j)]
            idx = it["idx"]
            if not second:
                pending[ch] = _dot(lhs_s[prev, idx], states[ch].astype(BF16))
            else:
                m1 = pending.pop(ch)
                add = add_s[prev, idx]
                ub = (-m1[:c] - add[:c]).astype(BF16)
                y_ref[it["rows"], lanes] = m1[c:] + add[c:] + _dot(pb_s[prev, idx], bdiag(ub))
                upd = _dot(ends_s[prev, idx], jnp.concatenate([ub, v_s[prev, idx]], axis=0))
                states[ch] = states[ch] * dec_s[prev, idx] + jnp.where(same_head, upd, 0.0)

    n_state_groups = 2 * chunks_per_block
    emitted = [0]

    def interleave_state_group():
        if emitted[0] < n_state_groups:
            state_group(emitted[0])
            emitted[0] += 1

    def stage_cumsum(group):
        for it in group:
            ld = it["refs"][3][0, it["rows"], it["lanes"]]
            tri = masks[it["rev"]][0]
            ld_hi = ld.astype(BF16)
            rem = ld - ld_hi.astype(F32)
            ld_mid = rem.astype(BF16)
            ld_lo = (rem - ld_mid.astype(F32)).astype(BF16)
            part = _dot(tri, jnp.concatenate([ld_hi, ld_mid], axis=1))
            it["cum"] = part[:, :LANES] + part[:, LANES:] + _dot(tri, ld_lo)

    def stage_gram(group):
        for it in group:
            r_ref, v_ref, kn_ref, ld_ref, kd_ref, bd_ref = it["refs"]
            rows, lanes = it["rows"], it["lanes"]
            _, strict, incl = masks[it["rev"]]
            ld = ld_ref[0, rows, lanes]
            kd = kd_ref[0, rows, lanes].astype(F32)
            bd = bd_ref[0, rows, lanes].astype(F32)
            cum = it.pop("cum")
            total = jnp.sum(ld, axis=0, keepdims=True)
            inv = jnp.exp(-cum)
            to_end = inv * jnp.exp(total)
            kq = (kn_ref[rows, lanes].astype(F32) * jnp.exp(cum - ld)).astype(BF16)
            rq = (r_ref[rows, lanes].astype(F32) * jnp.exp(cum)).astype(BF16)
            g = _dot_nt(jnp.concatenate([kq, rq], axis=0),
                        jnp.concatenate([bdiag((bd * inv).astype(BF16)), bdiag((kd * inv).astype(BF16))], axis=0))
            it["q"] = jnp.where(strict, -g[:c, :LANES], 0.0)
            it["bm"] = jnp.where(strict, g[:c, LANES:], 0.0).astype(BF16)
            it["pk"] = jnp.where(incl, g[c:, LANES:], 0.0).astype(BF16)
            it["kq_bd"], it["rq"] = bdiag(kq), rq
            v = v_ref[rows, lanes]
            it["v_bd"] = bdiag(v)
            idx = it["idx"]
            pb_s[cur, idx] = jnp.where(incl, g[c:, :LANES], 0.0).astype(BF16)
            v_s[cur, idx] = v
            ends_s[cur, idx] = jnp.concatenate([bd * to_end, kd * to_end], axis=0).T.astype(BF16)
            dec_s[cur, idx] = jnp.broadcast_to(jnp.exp(jnp.sum(ld.T, axis=1, keepdims=True)), (LANES, LANES))

    def stage_square(group):
        for it in group:
            q = it.pop("q")
            it["t_inv"] = eye_p + q
            qb = q.astype(BF16)
            it["pw"] = _dot(qb, bdiag(qb)).astype(BF16)

    n_levels = int(math.log2(c)) - 1

    def stage_level(level):
        def run(group):
            for it in group:
                pw = it["pw"]
                t_bd = bdiag(it["t_inv"].astype(BF16))
                if level < n_levels - 1:
                    res = _dot(pw, jnp.concatenate([bdiag(pw), t_bd], axis=1))
                    it["pw"] = res[:, :LANES].astype(BF16)
                    it["t_inv"] = it["t_inv"] + res[:, LANES:]
                else:
                    it["t_inv"] = it["t_inv"] + _dot(pw, t_bd)
        return run

    def stage_bmv(group):
        for it in group:
            it["bmv"] = _dot(it.pop("bm"), it["v_bd"]).astype(BF16)

    def stage_apply(group):
        for it in group:
            wu = _dot(it.pop("t_inv").astype(BF16),
                      jnp.concatenate([it.pop("kq_bd"), bdiag(it.pop("bmv"))], axis=1))
            lhs_s[cur, it["idx"]] = jnp.concatenate([wu[:, :LANES].astype(BF16), it.pop("rq")], axis=0)
            it["ut"] = wu[:, LANES:]

    def stage_pkv(group):
        for it in group:
            pkv = _dot(it.pop("pk"), it["v_bd"])
            add_s[cur, it["idx"]] = jnp.concatenate([it.pop("ut"), pkv], axis=0)

    stages = ([stage_cumsum, stage_gram, stage_square] + [stage_level(lv) for lv in range(n_levels)]
              + [stage_bmv, stage_apply, stage_pkv])
    half = len(insts) // 2
    groups = (insts[:half], insts[half:])
    for step_i in range(len(stages) + WKV_STAGE_STAGGER):
        if step_i < len(stages):
            stages[step_i](groups[0])
        if 0 <= step_i - WKV_STAGE_STAGGER < len(stages):
            stages[step_i - WKV_STAGE_STAGGER](groups[1])
        interleave_state_group()
    while emitted[0] < n_state_groups:
        interleave_state_group()
    for ch in range(len(chains)):
        h_ref[ch] = states[ch]


def _wkv_scan(r, v, kn, ld, kd, bd, *, n_lat_blocks):
    t, da = r.shape
    rows = TILE_ROWS
    n_blocks = t // rows
    pps = WKV_PAIRS_PER_STEP
    width = pps * LANES
    n_groups = da // width

    n_ctx_blocks = n_blocks - n_lat_blocks

    def fwd_blk(s):
        return jnp.where(s < n_ctx_blocks, n_lat_blocks + s, s - n_ctx_blocks)

    def rev_blk(s):
        return jnp.where(s < n_ctx_blocks, n_blocks - 1 - s, n_lat_blocks - 1 - (s - n_ctx_blocks))

    def cur(s):
        return jnp.minimum(s, n_blocks - 1)

    def prv(s):
        return jnp.maximum(s - 1, 0)

    fwd = pl.BlockSpec((rows, width), lambda g, s: (fwd_blk(cur(s)), g))
    bwd = pl.BlockSpec((rows, width), lambda g, s: (rev_blk(cur(s)), g))
    fwd_d = pl.BlockSpec((1, rows, width), lambda g, s: (0, fwd_blk(cur(s)), g))
    bwd_d = pl.BlockSpec((1, rows, width), lambda g, s: (1, rev_blk(cur(s)), g))
    y_fwd = pl.BlockSpec((rows, width), lambda g, s: (fwd_blk(prv(s)), g))
    y_bwd = pl.BlockSpec((rows, width), lambda g, s: (rev_blk(prv(s)), g))
    chunks = rows // CHUNK
    n_inst = 2 * pps * chunks
    kern = functools.partial(_wkv_kernel, chunks_per_block=chunks, pairs_per_step=pps)
    out = jax.ShapeDtypeStruct((t, da), F32)
    return pl.pallas_call(
        kern,
        grid=(n_groups, n_blocks + 1),
        in_specs=[fwd, fwd, fwd, fwd_d, fwd_d, fwd_d, bwd, bwd, bwd, bwd_d, bwd_d, bwd_d],
        out_specs=[y_fwd, y_bwd],
        out_shape=[out, out],
        scratch_shapes=[
            pltpu.VMEM((2 * pps, LANES, LANES), F32),
            pltpu.VMEM((2, n_inst, 2 * CHUNK, LANES), BF16),
            pltpu.VMEM((2, n_inst, 2 * CHUNK, LANES), F32),
            pltpu.VMEM((2, n_inst, CHUNK, LANES), BF16),
            pltpu.VMEM((2, n_inst, LANES, LANES), BF16),
            pltpu.VMEM((2, n_inst, LANES, LANES), F32),
            pltpu.VMEM((2, n_inst, CHUNK, LANES), BF16),
        ],
        compiler_params=_params("arbitrary", "arbitrary"),
        name="wkv_scan",
    )(r, v, kn, ld, kd, bd, r, v, kn, ld, kd, bd)


FFN_SUB_TILES = 2


def _sub_tiles(tm):
    rows = tm // FFN_SUB_TILES
    return [slice(s * rows, (s + 1) * rows) for s in range(FFN_SUB_TILES)]


def _ffn_sublayer(xs, m, w1_ref, w3_ref, w2_ref, lng_ref, lnb_ref):
    hs = [(x * (1.0 + m[4:5]) + m[3:4]).astype(BF16) for x in xs]
    ups = [(_dot(h, w1_ref[...]), _dot(h, w3_ref[...])) for h in hs]
    acts = [(a * _sigmoid(a) * b).astype(BF16) for a, b in ups]
    outs = [_dot(act, w2_ref[...]) for act in acts]
    return [_layer_norm(ALPHA * x + m[5:6] * out, lng_ref[...], lnb_ref[...]) for x, out in zip(xs, outs)]


def _readout_kernel(yf_ref, yr_ref, bonus_ref, g_ref, pool_ref, x_ref, mod_ref, seg_ref, lnxg_ref, lnxb_ref,
                    poolw_ref, pools_ref, wout_ref, lng_ref, lnb_ref, w1_ref, w3_ref, w2_ref, lng2_ref, lnb2_ref,
                    mod_next_ref, w_next_ref, o_ref, proj_ref):
    seg = seg_ref[...]
    m = mod_ref[0]
    x1s = []
    tiles = _sub_tiles(x_ref.shape[0])
    for rows in tiles:
        y = yf_ref[rows, :] + yr_ref[rows, :]
        mu = _seg_sum(y, seg) * (1.0 / HEAD)
        yc = y - mu
        var = _seg_sum(yc * yc, seg) * (1.0 / HEAD)
        yn = yc * lax.rsqrt(var + GN_EPS) * lnxg_ref[...] + lnxb_ref[...]
        a = (yn + bonus_ref[rows, :].astype(F32)) * g_ref[rows, :].astype(F32)
        b = _dot(pool_ref[rows, :].astype(BF16), poolw_ref[...]) * pools_ref[...]
        out = _dot(a.astype(BF16), wout_ref[:D_A, :]) + _dot(b.astype(BF16), wout_ref[D_A:, :])
        x1s.append(_layer_norm(ALPHA * x_ref[rows, :] + m[2:3] * out, lng_ref[...], lnb_ref[...]))
    x2s = _ffn_sublayer(x1s, m, w1_ref, w3_ref, w2_ref, lng2_ref, lnb2_ref)
    mn = mod_next_ref[0]
    for rows, x2 in zip(tiles, x2s):
        o_ref[rows, :] = x2
        proj_ref[rows, :] = _dot((x2 * (1.0 + mn[1:2]) + mn[0:1]).astype(BF16),
                                 w_next_ref[...]).astype(proj_ref.dtype)


def _single_spec(shape):
    nd = len(shape)
    return pl.BlockSpec(shape, lambda *_: (0,) * nd, pipeline_mode=pl.Buffered(1))


def _rwkv_readout_ffn(y_fwd, y_rev, bonus, g, pooled, x, mod, seg, lnx_g, lnx_b, pool_wbd, pool_scale, w_out, ln_g,
                      ln_b, w1, w3, w2, ln_g2, ln_b2, mod_next, w_next, *, n_lat_tiles):
    t, d = x.shape
    tm = TILE_ROWS
    tok = pl.BlockSpec((tm, D_A), lambda i: (i, 0))
    return pl.pallas_call(
        _readout_kernel,
        grid=(t // tm,),
        in_specs=[
            tok, tok, tok, tok,
            pl.BlockSpec((tm, D_B), lambda i: (i, 0)),
            pl.BlockSpec((tm, d), lambda i: (i, 0)),
            _mod_spec(d, n_lat_tiles),
            _const_spec(seg.shape), _const_spec(lnx_g.shape), _const_spec(lnx_b.shape),
            _const_spec(pool_wbd.shape), _const_spec(pool_scale.shape), _const_spec(w_out.shape),
            _const_spec(ln_g.shape), _const_spec(ln_b.shape),
            _single_spec(w1.shape), _single_spec(w3.shape), _single_spec(w2.shape),
            _const_spec(ln_g2.shape), _const_spec(ln_b2.shape),
            _mod_spec(d, n_lat_tiles), _single_spec(w_next.shape),
        ],
        out_specs=[pl.BlockSpec((tm, d), lambda i: (i, 0)), pl.BlockSpec((tm, w_next.shape[1]), lambda i: (i, 0))],
        out_shape=[jax.ShapeDtypeStruct((t, d), F32), jax.ShapeDtypeStruct((t, w_next.shape[1]), BF16)],
        compiler_params=_params("arbitrary"),
        name="rwkv_readout_ffn",
    )(y_fwd, y_rev, bonus, g, pooled, x, mod, seg, lnx_g, lnx_b, pool_wbd, pool_scale, w_out, ln_g, ln_b,
      w1, w3, w2, ln_g2, ln_b2, mod_next, w_next)


ATTN_PV_LAG = 4
ATTN_ROWS_PER_STEP = 32


def _attn_kernel(q_ref, k_ref, v_ref, bias_ref, o_ref, *, q_rows, n_rows, n_lat, n_ctx, kh):
    rb = pl.program_id(1)
    head0 = lax.broadcasted_iota(jnp.int32, (1, LANES), 1) < HEAD
    kc = k_ref[n_lat:n_lat + n_ctx, :]
    vc = v_ref[n_lat:n_lat + n_ctx, :]
    n_loc = kh * GRID_W
    q_all = q_ref[...] * jnp.asarray(HEAD ** -0.5, q_ref.dtype)
    zero = jnp.zeros((GRID_W, LANES), q_all.dtype)

    rows = [dict() for _ in range(q_rows)]

    def scores(j):
        it = rows[j]
        rr = rb * q_rows + j
        sr = jnp.clip(rr - kh // 2, 0, n_rows - kh)
        t_var = sr - rr + (NA_KH - 1)
        q = q_all[j * GRID_W:(j + 1) * GRID_W]
        q2 = jnp.concatenate([jnp.where(head0, q, zero), jnp.where(head0, zero, q)], axis=0)
        it["krows"] = pl.ds(pl.multiple_of(sr * GRID_W, GRID_W), n_loc)
        bias = jnp.concatenate(
            [jnp.concatenate([bias_ref[hh, t_var + 2 * a] for a in range(kh // 2)], axis=1) for hh in range(2)],
            axis=0)
        it["s_loc"] = _dot_nt(q2, k_ref[it["krows"], :]) + bias
        it["s_ctx"] = _dot_nt(q2, kc)

    def softmax(j):
        it = rows[j]
        s_loc, s_ctx = it.pop("s_loc"), it.pop("s_ctx")
        m = jnp.maximum(jnp.max(s_loc, axis=-1, keepdims=True), jnp.max(s_ctx, axis=-1, keepdims=True))
        p_loc = jnp.exp(s_loc - m)
        p_ctx = jnp.exp(s_ctx - m)
        it["denom"] = jnp.sum(p_loc, axis=-1, keepdims=True) + jnp.sum(p_ctx, axis=-1, keepdims=True)
        it["p_loc"] = p_loc.astype(BF16)
        it["p_ctx"] = p_ctx.astype(BF16)

    def values(j):
        it = rows[j]
        o2 = (_dot(it.pop("p_loc"), v_ref[it["krows"], :]) + _dot(it.pop("p_ctx"), vc)) / it["denom"]
        o_ref[j * GRID_W:(j + 1) * GRID_W, :] = jnp.where(head0, o2[:GRID_W], o2[GRID_W:]).astype(o_ref.dtype)

    for j in range(q_rows + ATTN_PV_LAG):
        if j < q_rows:
            scores(j)
        if 0 <= j - 1 < q_rows:
            softmax(j - 1)
        if 0 <= j - ATTN_PV_LAG < q_rows:
            values(j - ATTN_PV_LAG)


def _neighbourhood_attention(qkv, bias_tab, *, n_ctx, n_lat):
    t_all = qkv.shape[0]
    d = D_MODEL
    n_rows = n_lat // GRID_W
    kh = min(NA_KH, n_rows)
    q_rows = ATTN_ROWS_PER_STEP
    q_tile = q_rows * GRID_W
    n_pairs = d // LANES
    kern = functools.partial(_attn_kernel, q_rows=q_rows, n_rows=n_rows, n_lat=n_lat, n_ctx=n_ctx, kh=kh)
    assert kh % 2 == 0
    return pl.pallas_call(
        kern,
        grid=(n_pairs, n_lat // q_tile),
        in_specs=[
            pl.BlockSpec((q_tile, LANES), lambda p, b: (b, p)),
            pl.BlockSpec((t_all, LANES), lambda p, b: (0, n_pairs + p)),
            pl.BlockSpec((t_all, LANES), lambda p, b: (0, 2 * n_pairs + p)),
            pl.BlockSpec((2,) + bias_tab.shape[1:], lambda p, b: (p, 0, 0, 0)),
        ],
        out_specs=pl.BlockSpec((q_tile, LANES), lambda p, b: (b, p)),
        out_shape=jax.ShapeDtypeStruct((n_lat, d), BF16),
        compiler_params=_params("arbitrary", "arbitrary"),
        name="neighbourhood_attention",
    )(qkv, qkv, qkv, bias_tab)


def _attention_bias_table(rpb):
    cols = np.arange(GRID_W)
    col_start = np.clip(cols - NA_KW // 2, 0, GRID_W - NA_KW)
    in_win = (cols[None, :] >= col_start[:, None]) & (cols[None, :] < col_start[:, None] + NA_KW)
    col_off = cols[None, :] - cols[:, None] + (NA_KW - 1)
    onehot = (col_off[None] == np.arange(2 * NA_KW - 1)[:, None, None]) & in_win[None]
    dense = jnp.einsum('hrj,jck->hrck', rpb, jnp.asarray(onehot, F32), precision=HIGHEST)
    dense = jnp.where(jnp.asarray(in_win)[None, None], dense, NEG_BIAS)
    return jnp.concatenate([dense[:, :-1], dense[:, 1:]], axis=-1).astype(F32)


def _proj_ffn_kernel(o_ref, x_ref, mod_ref, w_ref, lng_ref, lnb_ref, w1_ref, w3_ref, w2_ref, lng2_ref, lnb2_ref,
                     out_ref):
    m = mod_ref[0]
    tiles = _sub_tiles(x_ref.shape[0])
    ys = [_dot(o_ref[rows, :], w_ref[...]) for rows in tiles]
    x1s = [_layer_norm(ALPHA * x_ref[rows, :] + m[2:3] * y, lng_ref[...], lnb_ref[...]) for rows, y in zip(tiles, ys)]
    x2s = _ffn_sublayer(x1s, m, w1_ref, w3_ref, w2_ref, lng2_ref, lnb2_ref)
    for rows, x2 in zip(tiles, x2s):
        out_ref[rows, :] = x2


def _attn_proj_ffn(o, x, mod, w, ln_g, ln_b, w1, w3, w2, ln_g2, ln_b2, *, tm):
    t, d = o.shape
    return pl.pallas_call(
        _proj_ffn_kernel,
        grid=(t // tm,),
        in_specs=[
            pl.BlockSpec((tm, d), lambda i: (i, 0)),
            pl.BlockSpec((tm, d), lambda i: (i, 0)),
            pl.BlockSpec((1, SUBLANES, d), lambda i: (1, 0, 0)),
            _single_spec(w.shape), _const_spec(ln_g.shape), _const_spec(ln_b.shape),
            _single_spec(w1.shape), _single_spec(w3.shape), _single_spec(w2.shape),
            _const_spec(ln_g2.shape), _const_spec(ln_b2.shape),
        ],
        out_specs=pl.BlockSpec((tm, d), lambda i: (i, 0)),
        out_shape=jax.ShapeDtypeStruct((t, d), F32),
        compiler_params=_params("arbitrary"),
        name="attn_proj_ffn",
    )(o, x, mod, w, ln_g, ln_b, w1, w3, w2, ln_g2, ln_b2)


def _block_diag(w):
    n, r, c = w.shape
    on_diag = (np.arange(n * r)[:, None] // r) == (np.arange(n * c)[None, :] // c)
    return jnp.where(jnp.asarray(on_diag), jnp.tile(w.reshape(n * r, c), (1, n)), 0.0)


def kernel(x, c, ctx, c_ctx, ada_w, ada_b, ln_g, ln_b, ffn_w1, ffn_w3, ffn_w2, ev_w_in, ev_shift_mu, ev_w0,
           ev_w2, ev_a0, ev_a2, ev_g2, ev_k_k, ev_k_a, ev_r_k, ev_lnx_g, ev_lnx_b, ev_pool_w, ev_pool_scale,
           ev_w_out, od_w_in, od_rpb, od_w_out):
    batch, n_lat, d = x.shape
    n_ctx = ctx.shape[1]
    assert batch == 1 and d == D_MODEL
    assert n_ctx % TILE_ROWS == 0 and n_lat % LAT_TILE_ROWS == 0 and n_lat % (ATTN_ROWS_PER_STEP * GRID_W) == 0
    n_lat_tiles = n_lat // TILE_ROWS

    cs_t = jnp.concatenate([c_ctx[:, None], c.T, jnp.zeros((d, LANES - N_COND), F32)], axis=1)
    mod_all = _ada_modulation(cs_t, ada_w, ada_b)
    mod_all = mod_all[:, :N_COND].reshape(DEPTH, N_COND, 6, d)
    mod_all = jnp.pad(mod_all, ((0, 0), (0, 0), (0, SUBLANES - 6), (0, 0)))

    mod = mod_all[0]
    seg = jnp.asarray(np.arange(LANES)[:, None] // HEAD == np.arange(LANES)[None, :] // HEAD, BF16)
    r, v, kn, ld, kd, bd, g, bonus, pooled, xa = _rwkv_prepare(
        x[0], ctx[0], mod, ev_w_in[0].astype(BF16), ev_shift_mu[0], ev_w0[0],
        _block_diag(ev_w2[0]).astype(BF16), ev_a0[0], _block_diag(ev_a2[0]).astype(BF16),
        ev_g2[0].astype(BF16), ev_k_k[0].reshape(1, D_A), ev_k_a[0].reshape(1, D_A),
        ev_r_k[0].reshape(1, D_A), seg)
    y_fwd, y_rev = _wkv_scan(r, v, kn, ld, kd, bd, n_lat_blocks=n_lat_tiles)
    pool_wbd = _block_diag(ev_pool_w[0])
    xa, qkv = _rwkv_readout_ffn(
        y_fwd, y_rev, bonus, g, pooled, xa, mod, seg, ev_lnx_g[0].reshape(1, D_A), ev_lnx_b[0].reshape(1, D_A),
        pool_wbd.astype(BF16), ev_pool_scale[0].reshape(1, D_B), ev_w_out[0].astype(BF16),
        ln_g[0, 0].reshape(1, d), ln_b[0, 0].reshape(1, d),
        ffn_w1[0].astype(BF16), ffn_w3[0].astype(BF16), ffn_w2[0].astype(BF16),
        ln_g[0, 1].reshape(1, d), ln_b[0, 1].reshape(1, d), mod_all[1], od_w_in[0].astype(BF16),
        n_lat_tiles=n_lat_tiles)

    mod = mod_all[1]
    bias_tab = _attention_bias_table(od_rpb[0])
    o = _neighbourhood_attention(qkv, bias_tab, n_ctx=n_ctx, n_lat=n_lat)
    xl = _attn_proj_ffn(
        o, xa, mod, od_w_out[0].astype(BF16), ln_g[1, 0].reshape(1, d), ln_b[1, 0].reshape(1, d),
        ffn_w1[1].astype(BF16), ffn_w3[1].astype(BF16), ffn_w2[1].astype(BF16),
        ln_g[1, 1].reshape(1, d), ln_b[1, 1].reshape(1, d), tm=LAT_TILE_ROWS)
    return xl[None]
```

```python
import functools
import math

import jax
import jax.numpy as jnp
import numpy as np
from jax import lax
from jax.experimental import pallas as pl
from jax.experimental.pallas import tpu as pltpu

F32 = jnp.float32
BF16 = jnp.bfloat16

D_MODEL = 1024
DEPTH = 2
GRID_W = 64
ALPHA = (2 * DEPTH) ** 0.25
LN_EPS = 1e-6
HEAD = 64
D_A = 3 * D_MODEL // 4
H_A = D_A // HEAD
LORA_W = 64
LORA_A = 64
LORA_G = 128
GN_EPS = 64e-5
D_B = D_MODEL - D_A
POOL_WINDOWS = (2, 4, 8, 16)
POOL_GROUP = D_B // len(POOL_WINDOWS)
C_RW = 3 * D_A + 2 * LORA_W + 2 * LORA_A + LORA_G
D_IN_EVEN = C_RW + D_B
H_C = D_MODEL // HEAD
NA_KH = 8
NA_KW = 16

LANES = 128
SUBLANES = 8
VMEM_LIMIT_BYTES = 56 * 1024 * 1024

TILE_ROWS = 256
LAT_TILE_ROWS = 512
CHUNK = 64
NEG_BIAS = -1e30


def _sigmoid(x):
    return 1.0 / (1.0 + jnp.exp(-x))


def _layer_norm(z, g, b):
    mu = jnp.mean(z, axis=-1, keepdims=True)
    zc = z - mu
    var = jnp.mean(zc * zc, axis=-1, keepdims=True)
    return zc * lax.rsqrt(var + LN_EPS) * g + b


def _dot(a, b):
    return jnp.dot(a, b, preferred_element_type=F32)


def _dot_nt(a, b):
    return lax.dot_general(a, b, (((1,), (1,)), ((), ())), preferred_element_type=F32)


def _seg_sum(x, seg):
    xb = x.astype(BF16)
    return jnp.concatenate([_dot(xb[:, j:j + LANES], seg) for j in range(0, x.shape[1], LANES)], axis=1)


def _params(*sem):
    return pltpu.CompilerParams(dimension_semantics=sem, vmem_limit_bytes=VMEM_LIMIT_BYTES)


def _const_spec(shape):
    nd = len(shape)
    return pl.BlockSpec(shape, lambda *_: (0,) * nd)


def _mod_spec(d, n_lat_tiles):
    return pl.BlockSpec((1, SUBLANES, d), lambda i: ((i < n_lat_tiles).astype(jnp.int32), 0, 0))


N_COND = 2


def _ada_kernel(cst_ref, w_ref, b_ref, o_ref):
    w = w_ref[0]
    rows = []
    for r in range(N_COND):
        s = cst_ref[:, r:r + 1]
        rows.append(jnp.sum(w * (s * _sigmoid(s)), axis=0, keepdims=True) + b_ref[0])
    o_ref[0] = jnp.concatenate(rows + [jnp.zeros((SUBLANES - N_COND, w.shape[1]), F32)], axis=0)


def _ada_modulation(cs_t, ada_w, ada_b):
    depth, d, n = ada_w.shape
    nb = 1536
    return pl.pallas_call(
        _ada_kernel,
        grid=(depth, n // nb),
        in_specs=[
            pl.BlockSpec((d, LANES), lambda i, j: (0, 0)),
            pl.BlockSpec((1, d, nb), lambda i, j: (i, 0, j)),
            pl.BlockSpec((1, 1, nb), lambda i, j: (i, 0, j)),
        ],
        out_specs=pl.BlockSpec((1, SUBLANES, nb), lambda i, j: (i, 0, j)),
        out_shape=jax.ShapeDtypeStruct((depth, SUBLANES, n), F32),
        compiler_params=_params("arbitrary", "arbitrary"),
        name="ada_modulation",
    )(cs_t, ada_w, ada_b.reshape(depth, 1, n))


def _prep_kernel(x_ref, ctx_ref, xhp_ref, xhn_ref, chp_ref, chn_ref, mod_ref, w_ref, mu_ref, w0_ref, w2_ref,
                 a0_ref, a2_ref, g2_ref, kk_ref, ka_ref, rk_ref, seg_ref,
                 r_o, v_o, kn_o, ld_o, kd_o, bd_o, g_o, bonus_o, pool_o, xa_o, *, tm, n_lat_tiles, n_tiles):
    i = pl.program_id(0)
    is_lat = i < n_lat_tiles
    first = jnp.logical_or(i == 0, i == n_lat_tiles)
    last = jnp.logical_or(i == n_lat_tiles - 1, i == n_tiles - 1)
    row = lax.broadcasted_iota(jnp.int32, (tm, 1), 0)
    n_ext = tm + 2 * SUBLANES

    xt = jnp.where(is_lat, x_ref[...], ctx_ref[...])
    xa_o[...] = xt
    x_ext = jnp.concatenate([jnp.where(is_lat, xhp_ref[...], chp_ref[...]), xt,
                             jnp.where(is_lat, xhn_ref[...], chn_ref[...])], axis=0)
    m = mod_ref[0]
    h_ext = (x_ext * (1.0 + m[1:2]) + m[0:1]).astype(BF16)
    mu_prev = mu_ref[0:1]
    mu_next = mu_ref[1:2]
    mu_self = 1.0 - mu_prev - mu_next
    inner = slice(SUBLANES, SUBLANES + tm)

    def project(c0, c1):
        p = _dot(h_ext, w_ref[:, c0:c1])
        return jnp.concatenate([jnp.where(first, 0.0, p[:SUBLANES]), p[inner],
                                jnp.where(last, 0.0, p[SUBLANES + tm:])], axis=0)

    def shift_mix(p, c0, c1):
        return (p[inner] * mu_self[:, c0:c1] + pltpu.roll(p, 1, 0)[inner] * mu_prev[:, c0:c1]
                + pltpu.roll(p, n_ext - 1, 0)[inner] * mu_next[:, c0:c1])

    s1, s2, s3 = D_A, 2 * D_A, 3 * D_A
    s4 = s3 + 2 * LORA_W
    s5 = s4 + 2 * LORA_A
    lora = shift_mix(project(s3, C_RW), s3, C_RW)
    k_ext = project(s1, s2)
    lw = _dot(jnp.tanh(lora[:, :s4 - s3]).astype(BF16), w2_ref[...])
    la = _dot(lora[:, s4 - s3:s5 - s3].astype(BF16), a2_ref[...])
    g = _dot(_sigmoid(lora[:, s5 - s3:]).astype(BF16), g2_ref[...])
    g_o[...] = g.astype(g_o.dtype)

    r_ext = project(0, s1)
    k = shift_mix(k_ext, s1, s2)
    kr = k * kk_ref[...]
    ss = _seg_sum(kr * kr, seg_ref[...])
    kn = kr * lax.rsqrt(jnp.maximum(ss, 1e-24))
    kn_o[...] = kn.astype(kn_o.dtype)

    v_ext = project(s2, s3)
    decay_scale = math.exp(-0.5)
    kd_sum = jnp.zeros_like(k)
    for d in range(2):
        sl = slice(d * D_A, (d + 1) * D_A)
        ld_o[d] = -decay_scale * _sigmoid(w0_ref[d:d + 1] + lw[:, sl])
        a = _sigmoid(a0_ref[d:d + 1] + la[:, sl])
        kd = k * (1.0 + (a - 1.0) * ka_ref[...])
        kd_o[d] = kd.astype(kd_o.dtype)
        bd_o[d] = (kn * a).astype(bd_o.dtype)
        kd_sum = kd_sum + kd

    xe = project(C_RW, D_IN_EVEN)
    r = shift_mix(r_ext, 0, s1)
    r_o[...] = r.astype(r_o.dtype)
    coef = _seg_sum(r * kd_sum * rk_ref[...], seg_ref[...])
    v = shift_mix(v_ext, s2, s3)
    v_o[...] = v.astype(v_o.dtype)
    bonus_o[...] = (coef * v).astype(bonus_o.dtype)

    xp = xe[inner]

    def at_offset(a, s):
        return pltpu.roll(a, (-s) % n_ext, 0)

    sums = {}
    acc = xe + at_offset(xe, -1)
    sums[2] = acc
    half = 1
    for win in (4, 8, 16):
        acc = at_offset(acc, -half) + at_offset(acc, half)
        sums[win] = acc
        half *= 2

    in_ctx = i >= n_lat_tiles
    tile0 = jnp.where(in_ctx, n_lat_tiles, 0)
    seq_len = jnp.where(in_ctx, (n_tiles - n_lat_tiles) * tm, n_lat_tiles * tm)
    t_seq = (i - tile0) * tm + row
    grp = lax.broadcasted_iota(jnp.int32, (1, D_B), 1) // POOL_GROUP
    mean = jnp.zeros((tm, D_B), F32)
    for gi, win in enumerate(POOL_WINDOWS):
        cnt = jnp.minimum(t_seq + win // 2, seq_len) - jnp.maximum(t_seq - win // 2, 0)
        m = sums[win][SUBLANES:SUBLANES + tm] / cnt.astype(F32)
        mean = jnp.where(grp == gi, m, mean)
    pool_o[...] = mean - xp


def _rwkv_prepare(x, ctx, mod, w_in, mu, w0, w2bd, a0, a2bd, g2, k_k, k_a, r_k, seg):
    n_lat, d = x.shape
    n_ctx = ctx.shape[0]
    t = n_lat + n_ctx
    tm = TILE_ROWS
    n_tiles = t // tm
    n_lat_tiles = n_lat // tm
    hpt = tm // SUBLANES
    lat_halos = n_lat // SUBLANES
    ctx_halos = n_ctx // SUBLANES

    def lat_tile(i):
        return jnp.minimum(i, n_lat_tiles - 1)

    def ctx_tile(i):
        return jnp.maximum(i - n_lat_tiles, 0)

    halo = (SUBLANES, d)
    tok = jax.ShapeDtypeStruct((t, D_A), BF16)
    tok2 = jax.ShapeDtypeStruct((2, t, D_A), BF16)
    ld2 = jax.ShapeDtypeStruct((2, t, D_A), F32)
    tok_spec = pl.BlockSpec((tm, D_A), lambda i: (i, 0))
    tok2_spec = pl.BlockSpec((2, tm, D_A), lambda i: (0, i, 0))
    kern = functools.partial(_prep_kernel, tm=tm, n_lat_tiles=n_lat_tiles, n_tiles=n_tiles)
    return pl.pallas_call(
        kern,
        grid=(n_tiles,),
        in_specs=[
            pl.BlockSpec((tm, d), lambda i: (lat_tile(i), 0)),
            pl.BlockSpec((tm, d), lambda i: (ctx_tile(i), 0)),
            pl.BlockSpec(halo, lambda i: (jnp.maximum(lat_tile(i) * hpt - 1, 0), 0)),
            pl.BlockSpec(halo, lambda i: (jnp.minimum((lat_tile(i) + 1) * hpt, lat_halos - 1), 0)),
            pl.BlockSpec(halo, lambda i: (jnp.maximum(ctx_tile(i) * hpt - 1, 0), 0)),
            pl.BlockSpec(halo, lambda i: (jnp.minimum((ctx_tile(i) + 1) * hpt, ctx_halos - 1), 0)),
            _mod_spec(d, n_lat_tiles), _const_spec(w_in.shape),
            _const_spec(mu.shape), _const_spec(w0.shape), _const_spec(w2bd.shape), _const_spec(a0.shape),
            _const_spec(a2bd.shape), _const_spec(g2.shape), _const_spec(k_k.shape), _const_spec(k_a.shape),
            _const_spec(r_k.shape), _const_spec(seg.shape),
        ],
        out_specs=[tok_spec, tok_spec, tok_spec, tok2_spec, tok2_spec, tok2_spec, tok_spec, tok_spec,
                   pl.BlockSpec((tm, D_B), lambda i: (i, 0)), pl.BlockSpec((tm, d), lambda i: (i, 0))],
        out_shape=[tok, tok, tok, ld2, tok2, tok2, tok, tok, jax.ShapeDtypeStruct((t, D_B), F32),
                   jax.ShapeDtypeStruct((t, d), x.dtype)],
        compiler_params=_params("arbitrary"),
        name="rwkv_prepare",
    )(x, ctx, x, x, ctx, ctx, mod, w_in, mu, w0, w2bd, a0, a2bd, g2, k_k, k_a, r_k, seg)


WKV_PAIRS_PER_STEP = 6
WKV_STAGE_STAGGER = 1


def _wkv_kernel(rf_ref, vf_ref, knf_ref, ldf_ref, kdf_ref, bdf_ref, rr_ref, vr_ref, knr_ref, ldr_ref, kdr_ref,
                bdr_ref, yf_ref, yr_ref, h_ref, lhs_s, add_s, pb_s, ends_s, dec_s, v_s, *, chunks_per_block,
                pairs_per_step):
    step = pl.program_id(1)
    cur = lax.rem(step, 2)
    prev = 1 - cur

    @pl.when(step == 0)
    def _():
        h_ref[...] = jnp.zeros_like(h_ref)
        for ref in (lhs_s, add_s, pb_s, ends_s, dec_s, v_s):
            ref[1] = jnp.zeros(ref.shape[1:], ref.dtype)

    c = CHUNK
    ri = lax.broadcasted_iota(jnp.int32, (c, c), 0)
    ci = lax.broadcasted_iota(jnp.int32, (c, c), 1)
    rp = lax.broadcasted_iota(jnp.int32, (c, LANES), 0)
    cp = lax.broadcasted_iota(jnp.int32, (c, LANES), 1) & (c - 1)
    eye_p = jnp.where(rp == cp, 1.0, 0.0)
    r2 = lax.broadcasted_iota(jnp.int32, (LANES, LANES), 0)
    c2 = lax.broadcasted_iota(jnp.int32, (LANES, LANES), 1)
    same_head = (r2 < HEAD) == (c2 < HEAD)
    head0 = lax.broadcasted_iota(jnp.int32, (1, LANES), 1) < HEAD

    def bdiag(x):
        zero = jnp.zeros_like(x)
        return jnp.concatenate([jnp.where(head0, x, zero), jnp.where(head0, zero, x)], axis=0)

    masks = {}
    for rev in (False, True):
        sgn = -1 if rev else 1
        order = (rp - cp) * sgn
        masks[rev] = (jnp.where((ri - ci) * sgn >= 0, 1.0, 0.0).astype(BF16), order > 0, order >= 0)

    chains = []
    for rev, refs, y_ref in ((False, (rf_ref, vf_ref, knf_ref, ldf_ref, kdf_ref, bdf_ref), yf_ref),
                             (True, (rr_ref, vr_ref, knr_ref, ldr_ref, kdr_ref, bdr_ref), yr_ref)):
        for q in range(pairs_per_step):
            chains.append((rev, refs, y_ref, slice(q * LANES, (q + 1) * LANES)))

    insts = []
    for ch, (rev, refs, y_ref, lanes) in enumerate(chains):
        for j in range(chunks_per_block):
            cj = chunks_per_block - 1 - j if rev else j
            insts.append(dict(idx=len(insts), ch=ch, j=j, rev=rev, refs=refs, lanes=lanes,
                              rows=slice(cj * c, (cj + 1) * c)))
    by_key = {(it["ch"], it["j"]): it for it in insts}

    states = [h_ref[ch] for ch in range(len(chains))]
    pending = {}

    def state_group(k):
        j, second = divmod(k, 2)
        for ch, (rev, refs, y_ref, lanes) in enumerate(chains):
            it = by_key[(ch, j)]
            idx = it["idx"]
            if not second:
                pending[ch] = _dot(lhs_s[prev, idx], states[ch].astype(BF16))
            else:
                m1 = pending.pop(ch)
                add = add_s[prev, idx]
                ub = (-m1[:c] - add[:c]).astype(BF16)
                y_ref[it["rows"], lanes] = m1[c:] + add[c:] + _dot(pb_s[prev, idx], bdiag(ub))
                upd = _dot(ends_s[prev, idx], jnp.concatenate([ub, v_s[prev, idx]], axis=0))
                states[ch] = states[ch] * dec_s[prev, idx] + jnp.where(same_head, upd, 0.0)

    n_state_groups = 2 * chunks_per_block
    emitted = [0]

    def interleave_state_group():
        if emitted[0] < n_state_groups:
            state_group(emitted[0])
            emitted[0] += 1

    def stage_cumsum(group):
        for it in group:
            ld = it["refs"][3][0, it["rows"], it["lanes"]]
            tri = masks[it["rev"]][0]
            ld_hi = ld.astype(BF16)
            rem = ld - ld_hi.astype(F32)
            ld_mid = rem.astype(BF16)
            ld_lo = (rem - ld_mid.astype(F32)).astype(BF16)
            part = _dot(tri, jnp.concatenate([ld_hi, ld_mid], axis=1))
            it["cum"] = part[:, :LANES] + part[:, LANES:] + _dot(tri, ld_lo)

    def stage_gram(group):
        for it in group:
            r_ref, v_ref, kn_ref, ld_ref, kd_ref, bd_ref = it["refs"]
            rows, lanes = it["rows"], it["lanes"]
            _, strict, incl = masks[it["rev"]]
            ld = ld_ref[0, rows, lanes]
            kd = kd_ref[0, rows, lanes].astype(F32)
            bd = bd_ref[0, rows, lanes].astype(F32)
            cum = it.pop("cum")
            total = jnp.sum(ld, axis=0, keepdims=True)
            inv = jnp.exp(-cum)
            to_end = inv * jnp.exp(total)
            kq = (kn_ref[rows, lanes].astype(F32) * jnp.exp(cum - ld)).astype(BF16)
            rq = (r_ref[rows, lanes].astype(F32) * jnp.exp(cum)).astype(BF16)
            g = _dot_nt(jnp.concatenate([kq, rq], axis=0),
                        jnp.concatenate([bdiag((bd * inv).astype(BF16)), bdiag((kd * inv).astype(BF16))], axis=0))
            it["q"] = jnp.where(strict, -g[:c, :LANES], 0.0)
            it["bm"] = jnp.where(strict, g[:c, LANES:], 0.0).astype(BF16)
            it["pk"] = jnp.where(incl, g[c:, LANES:], 0.0).astype(BF16)
            it["kq_bd"], it["rq"] = bdiag(kq), rq
            v = v_ref[rows, lanes]
            it["v_bd"] = bdiag(v)
            idx = it["idx"]
            pb_s[cur, idx] = jnp.where(incl, g[c:, :LANES], 0.0).astype(BF16)
            v_s[cur, idx] = v
            ends_s[cur, idx] = jnp.concatenate([bd * to_end, kd * to_end], axis=0).T.astype(BF16)
            dec_s[cur, idx] = jnp.broadcast_to(jnp.exp(jnp.sum(ld.T, axis=1, keepdims=True)), (LANES, LANES))

    def stage_square(group):
        for it in group:
            q = it.pop("q")
            it["t_inv"] = eye_p + q
            qb = q.astype(BF16)
            it["pw"] = _dot(qb, bdiag(qb)).astype(BF16)

    n_levels = int(math.log2(c)) - 1

    def stage_level(level):
        def run(group):
            for it in group:
                pw = it["pw"]
                t_bd = bdiag(it["t_inv"].astype(BF16))
                if level < n_levels - 1:
                    res = _dot(pw, jnp.concatenate([bdiag(pw), t_bd], axis=1))
                    it["pw"] = res[:, :LANES].astype(BF16)
                    it["t_inv"] = it["t_inv"] + res[:, LANES:]
                else:
                    it["t_inv"] = it["t_inv"] + _dot(pw, t_bd)
        return run

    def stage_bmv(group):
        for it in group:
            it["bmv"] = _dot(it.pop("bm"), it["v_bd"]).astype(BF16)

    def stage_apply(group):
        for it in group:
            wu = _dot(it.pop("t_inv").astype(BF16),
                      jnp.concatenate([it.pop("kq_bd"), bdiag(it.pop("bmv"))], axis=1))
            lhs_s[cur, it["idx"]] = jnp.concatenate([wu[:, :LANES].astype(BF16), it.pop("rq")], axis=0)
            it["ut"] = wu[:, LANES:]

    def stage_pkv(group):
        for it in group:
            pkv = _dot(it.pop("pk"), it["v_bd"])
            add_s[cur, it["idx"]] = jnp.concatenate([it.pop("ut"), pkv], axis=0)

    stages = ([stage_cumsum, stage_gram, stage_square] + [stage_level(lv) for lv in range(n_levels)]
              + [stage_bmv, stage_apply, stage_pkv])
    half = len(insts) // 2
    groups = (insts[:half], insts[half:])
    for step_i in range(len(stages) + WKV_STAGE_STAGGER):
        if step_i < len(stages):
            stages[step_i](groups[0])
        if 0 <= step_i - WKV_STAGE_STAGGER < len(stages):
            stages[step_i - WKV_STAGE_STAGGER](groups[1])
        interleave_state_group()
    while emitted[0] < n_state_groups:
        interleave_state_group()
    for ch in range(len(chains)):
        h_ref[ch] = states[ch]


def _wkv_scan(r, v, kn, ld, kd, bd, *, n_lat_blocks):
    t, da = r.shape
    rows = TILE_ROWS
    n_blocks = t // rows
    pps = WKV_PAIRS_PER_STEP
    width = pps * LANES
    n_groups = da // width

    n_ctx_blocks = n_blocks - n_lat_blocks

    def fwd_blk(s):
        return jnp.where(s < n_ctx_blocks, n_lat_blocks + s, s - n_ctx_blocks)

    def rev_blk(s):
        return jnp.where(s < n_ctx_blocks, n_blocks - 1 - s, n_lat_blocks - 1 - (s - n_ctx_blocks))

    def cur(s):
        return jnp.minimum(s, n_blocks - 1)

    def prv(s):
        return jnp.maximum(s - 1, 0)

    fwd = pl.BlockSpec((rows, width), lambda g, s: (fwd_blk(cur(s)), g))
    bwd = pl.BlockSpec((rows, width), lambda g, s: (rev_blk(cur(s)), g))
    fwd_d = pl.BlockSpec((1, rows, width), lambda g, s: (0, fwd_blk(cur(s)), g))
    bwd_d = pl.BlockSpec((1, rows, width), lambda g, s: (1, rev_blk(cur(s)), g))
    y_fwd = pl.BlockSpec((rows, width), lambda g, s: (fwd_blk(prv(s)), g))
    y_bwd = pl.BlockSpec((rows, width), lambda g, s: (rev_blk(prv(s)), g))
    chunks = rows // CHUNK
    n_inst = 2 * pps * chunks
    kern = functools.partial(_wkv_kernel, chunks_per_block=chunks, pairs_per_step=pps)
    out = jax.ShapeDtypeStruct((t, da), F32)
    return pl.pallas_call(
        kern,
        grid=(n_groups, n_blocks + 1),
        in_specs=[fwd, fwd, fwd, fwd_d, fwd_d, fwd_d, bwd, bwd, bwd, bwd_d, bwd_d, bwd_d],
        out_specs=[y_fwd, y_bwd],
        out_shape=[out, out],
        scratch_shapes=[
            pltpu.VMEM((2 * pps, LANES, LANES), F32),
            pltpu.VMEM((2, n_inst, 2 * CHUNK, LANES), BF16),
            pltpu.VMEM((2, n_inst, 2 * CHUNK, LANES), F32),
            pltpu.VMEM((2, n_inst, CHUNK, LANES), BF16),
            pltpu.VMEM((2, n_inst, LANES, LANES), BF16),
            pltpu.VMEM((2, n_inst, LANES, LANES), F32),
            pltpu.VMEM((2, n_inst, CHUNK, LANES), BF16),
        ],
        compiler_params=_params("arbitrary", "arbitrary"),
        name="wkv_scan",
    )(r, v, kn, ld, kd, bd, r, v, kn, ld, kd, bd)


FFN_SUB_TILES = 2


def _sub_tiles(tm):
    rows = tm // FFN_SUB_TILES
    return [slice(s * rows, (s + 1) * rows) for s in range(FFN_SUB_TILES)]


def _ffn_sublayer(xs, m, w1_ref, w3_ref, w2_ref, lng_ref, lnb_ref):
    hs = [(x * (1.0 + m[4:5]) + m[3:4]).astype(BF16) for x in xs]
    ups = [(_dot(h, w1_ref[...]), _dot(h, w3_ref[...])) for h in hs]
    acts = [(a * _sigmoid(a) * b).astype(BF16) for a, b in ups]
    outs = [_dot(act, w2_ref[...]) for act in acts]
    return [_layer_norm(ALPHA * x + m[5:6] * out, lng_ref[...], lnb_ref[...]) for x, out in zip(xs, outs)]


def _readout_kernel(yf_ref, yr_ref, bonus_ref, g_ref, pool_ref, x_ref, mod_ref, seg_ref, lnxg_ref, lnxb_ref,
                    poolw_ref, pools_ref, wout_ref, lng_ref, lnb_ref, w1_ref, w3_ref, w2_ref, lng2_ref, lnb2_ref,
                    mod_next_ref, w_next_ref, o_ref, proj_ref):
    seg = seg_ref[...]
    m = mod_ref[0]
    x1s = []
    tiles = _sub_tiles(x_ref.shape[0])
    for rows in tiles:
        y = yf_ref[rows, :] + yr_ref[rows, :]
        mu = _seg_sum(y, seg) * (1.0 / HEAD)
        yc = y - mu
        var = _seg_sum(yc * yc, seg) * (1.0 / HEAD)
        yn = yc * lax.rsqrt(var + GN_EPS) * lnxg_ref[...] + lnxb_ref[...]
        a = (yn + bonus_ref[rows, :].astype(F32)) * g_ref[rows, :].astype(F32)
        b = _dot(pool_ref[rows, :].astype(BF16), poolw_ref[...]) * pools_ref[...]
        out = _dot(a.astype(BF16), wout_ref[:D_A, :]) + _dot(b.astype(BF16), wout_ref[D_A:, :])
        x1s.append(_layer_norm(ALPHA * x_ref[rows, :] + m[2:3] * out, lng_ref[...], lnb_ref[...]))
    x2s = _ffn_sublayer(x1s, m, w1_ref, w3_ref, w2_ref, lng2_ref, lnb2_ref)
    mn = mod_next_ref[0]
    for rows, x2 in zip(tiles, x2s):
        o_ref[rows, :] = x2
        proj = _dot((x2 * (1.0 + mn[1:2]) + mn[0:1]).astype(BF16), w_next_ref[...]).astype(proj_ref.dtype)
        for j in range(proj_ref.shape[0]):
            proj_ref[j, rows, :] = proj[:, j * LANES:(j + 1) * LANES]


def _single_spec(shape):
    nd = len(shape)
    return pl.BlockSpec(shape, lambda *_: (0,) * nd, pipeline_mode=pl.Buffered(1))


def _rwkv_readout_ffn(y_fwd, y_rev, bonus, g, pooled, x, mod, seg, lnx_g, lnx_b, pool_wbd, pool_scale, w_out, ln_g,
                      ln_b, w1, w3, w2, ln_g2, ln_b2, mod_next, w_next, *, n_lat_tiles):
    t, d = x.shape
    tm = TILE_ROWS
    tok = pl.BlockSpec((tm, D_A), lambda i: (i, 0))
    return pl.pallas_call(
        _readout_kernel,
        grid=(t // tm,),
        in_specs=[
            tok, tok, tok, tok,
            pl.BlockSpec((tm, D_B), lambda i: (i, 0)),
            pl.BlockSpec((tm, d), lambda i: (i, 0)),
            _mod_spec(d, n_lat_tiles),
            _const_spec(seg.shape), _const_spec(lnx_g.shape), _const_spec(lnx_b.shape),
            _const_spec(pool_wbd.shape), _const_spec(pool_scale.shape), _const_spec(w_out.shape),
            _const_spec(ln_g.shape), _const_spec(ln_b.shape),
            _single_spec(w1.shape), _single_spec(w3.shape), _single_spec(w2.shape),
            _const_spec(ln_g2.shape), _const_spec(ln_b2.shape),
            _mod_spec(d, n_lat_tiles), _single_spec(w_next.shape),
        ],
        out_specs=[pl.BlockSpec((tm, d), lambda i: (i, 0)),
                   pl.BlockSpec((w_next.shape[1] // LANES, tm, LANES), lambda i: (0, i, 0))],
        out_shape=[jax.ShapeDtypeStruct((t, d), F32),
                   jax.ShapeDtypeStruct((w_next.shape[1] // LANES, t, LANES), BF16)],
        compiler_params=_params("arbitrary"),
        name="rwkv_readout_ffn",
    )(y_fwd, y_rev, bonus, g, pooled, x, mod, seg, lnx_g, lnx_b, pool_wbd, pool_scale, w_out, ln_g, ln_b,
      w1, w3, w2, ln_g2, ln_b2, mod_next, w_next)


ATTN_PV_LAG = 4
ATTN_ROWS_PER_STEP = 32


def _attn_kernel(q_ref, k_ref, v_ref, bias_ref, o_ref, *, q_rows, n_rows, n_lat, n_ctx, kh):
    rb = pl.program_id(1)
    head0 = lax.broadcasted_iota(jnp.int32, (1, LANES), 1) < HEAD
    kc = k_ref[0, n_lat:n_lat + n_ctx, :]
    vc = v_ref[0, n_lat:n_lat + n_ctx, :]
    n_loc = kh * GRID_W
    q_all = q_ref[0] * jnp.asarray(HEAD ** -0.5, q_ref.dtype)
    zero = jnp.zeros((GRID_W, LANES), q_all.dtype)

    rows = [dict() for _ in range(q_rows)]

    def scores(j):
        it = rows[j]
        rr = rb * q_rows + j
        sr = jnp.clip(rr - kh // 2, 0, n_rows - kh)
        t_var = sr - rr + (NA_KH - 1)
        q = q_all[j * GRID_W:(j + 1) * GRID_W]
        q2 = jnp.concatenate([jnp.where(head0, q, zero), jnp.where(head0, zero, q)], axis=0)
        it["krows"] = pl.ds(pl.multiple_of(sr * GRID_W, GRID_W), n_loc)
        bias = jnp.concatenate(
            [jnp.concatenate([bias_ref[hh, t_var + 2 * a] for a in range(kh // 2)], axis=1) for hh in range(2)],
            axis=0)
        it["s_loc"] = _dot_nt(q2, k_ref[0, it["krows"], :]) + bias
        it["s_ctx"] = _dot_nt(q2, kc)

    def softmax(j):
        it = rows[j]
        s_loc, s_ctx = it.pop("s_loc"), it.pop("s_ctx")
        m = jnp.maximum(jnp.max(s_loc, axis=-1, keepdims=True), jnp.max(s_ctx, axis=-1, keepdims=True))
        p_loc = jnp.exp(s_loc - m)
        p_ctx = jnp.exp(s_ctx - m)
        it["denom"] = jnp.sum(p_loc, axis=-1, keepdims=True) + jnp.sum(p_ctx, axis=-1, keepdims=True)
        it["p_loc"] = p_loc.astype(BF16)
        it["p_ctx"] = p_ctx.astype(BF16)

    def values(j):
        it = rows[j]
        o2 = (_dot(it.pop("p_loc"), v_ref[0, it["krows"], :]) + _dot(it.pop("p_ctx"), vc)) / it["denom"]
        o_ref[0, j * GRID_W:(j + 1) * GRID_W, :] = jnp.where(head0, o2[:GRID_W], o2[GRID_W:]).astype(o_ref.dtype)

    for j in range(q_rows + ATTN_PV_LAG):
        if j < q_rows:
            scores(j)
        if 0 <= j - 1 < q_rows:
            softmax(j - 1)
        if 0 <= j - ATTN_PV_LAG < q_rows:
            values(j - ATTN_PV_LAG)


def _neighbourhood_attention(qkv, bias_tab, *, n_ctx, n_lat):
    t_all = qkv.shape[1]
    d = D_MODEL
    n_rows = n_lat // GRID_W
    kh = min(NA_KH, n_rows)
    q_rows = ATTN_ROWS_PER_STEP
    q_tile = q_rows * GRID_W
    n_pairs = d // LANES
    kern = functools.partial(_attn_kernel, q_rows=q_rows, n_rows=n_rows, n_lat=n_lat, n_ctx=n_ctx, kh=kh)
    assert kh % 2 == 0
    return pl.pallas_call(
        kern,
        grid=(n_pairs, n_lat // q_tile),
        in_specs=[
            pl.BlockSpec((1, q_tile, LANES), lambda p, b: (p, b, 0)),
            pl.BlockSpec((1, t_all, LANES), lambda p, b: (n_pairs + p, 0, 0)),
            pl.BlockSpec((1, t_all, LANES), lambda p, b: (2 * n_pairs + p, 0, 0)),
            pl.BlockSpec((2,) + bias_tab.shape[1:], lambda p, b: (p, 0, 0, 0)),
        ],
        out_specs=pl.BlockSpec((1, q_tile, LANES), lambda p, b: (p, b, 0)),
        out_shape=jax.ShapeDtypeStruct((n_pairs, n_lat, LANES), BF16),
        compiler_params=_params("arbitrary", "arbitrary"),
        name="neighbourhood_attention",
    )(qkv, qkv, qkv, bias_tab)


def _attention_bias_table(rpb):
    h, n_ro, n_co = rpb.shape
    cols = np.arange(GRID_W)
    col_start = np.clip(cols - NA_KW // 2, 0, GRID_W - NA_KW)
    in_win = (cols[None, :] >= col_start[:, None]) & (cols[None, :] < col_start[:, None] + NA_KW)
    period = GRID_W + n_co
    rows = jnp.broadcast_to(jnp.pad(rpb, ((0, 0), (0, 0), (0, GRID_W)))[:, :, None, :], (h, n_ro, GRID_W, period))
    skew = rows.reshape(h, n_ro, GRID_W * period)[:, :, :GRID_W * (period - 1)].reshape(h, n_ro, GRID_W, period - 1)
    dense = skew[..., NA_KW - 1:NA_KW - 1 + GRID_W]
    dense = jnp.where(jnp.asarray(in_win)[None, None], dense, NEG_BIAS)
    return jnp.concatenate([dense[:, :-1], dense[:, 1:]], axis=-1).astype(F32)


def _proj_ffn_kernel(o_ref, x_ref, mod_ref, w_ref, lng_ref, lnb_ref, w1_ref, w3_ref, w2_ref, lng2_ref, lnb2_ref,
                     out_ref):
    m = mod_ref[0]
    tiles = _sub_tiles(x_ref.shape[0])
    ys = [_dot(jnp.concatenate([o_ref[j, rows, :] for j in range(o_ref.shape[0])], axis=1), w_ref[...])
          for rows in tiles]
    x1s = [_layer_norm(ALPHA * x_ref[rows, :] + m[2:3] * y, lng_ref[...], lnb_ref[...]) for rows, y in zip(tiles, ys)]
    x2s = _ffn_sublayer(x1s, m, w1_ref, w3_ref, w2_ref, lng2_ref, lnb2_ref)
    for rows, x2 in zip(tiles, x2s):
        out_ref[rows, :] = x2


def _attn_proj_ffn(o, x, mod, w, ln_g, ln_b, w1, w3, w2, ln_g2, ln_b2, *, tm):
    n_slabs, t, _ = o.shape
    d = n_slabs * LANES
    return pl.pallas_call(
        _proj_ffn_kernel,
        grid=(t // tm,),
        in_specs=[
            pl.BlockSpec((n_slabs, tm, LANES), lambda i: (0, i, 0)),
            pl.BlockSpec((tm, d), lambda i: (i, 0)),
            pl.BlockSpec((1, SUBLANES, d), lambda i: (1, 0, 0)),
            _single_spec(w.shape), _const_spec(ln_g.shape), _const_spec(ln_b.shape),
            _single_spec(w1.shape), _single_spec(w3.shape), _single_spec(w2.shape),
            _const_spec(ln_g2.shape), _const_spec(ln_b2.shape),
        ],
        out_specs=pl.BlockSpec((tm, d), lambda i: (i, 0)),
        out_shape=jax.ShapeDtypeStruct((t, d), F32),
        compiler_params=_params("arbitrary"),
        name="attn_proj_ffn",
    )(o, x, mod, w, ln_g, ln_b, w1, w3, w2, ln_g2, ln_b2)


def _block_diag(w):
    n, r, c = w.shape
    on_diag = (np.arange(n * r)[:, None] // r) == (np.arange(n * c)[None, :] // c)
    return jnp.where(jnp.asarray(on_diag), jnp.tile(w.reshape(n * r, c), (1, n)), 0.0)


def kernel(x, c, ctx, c_ctx, ada_w, ada_b, ln_g, ln_b, ffn_w1, ffn_w3, ffn_w2, ev_w_in, ev_shift_mu, ev_w0,
           ev_w2, ev_a0, ev_a2, ev_g2, ev_k_k, ev_k_a, ev_r_k, ev_lnx_g, ev_lnx_b, ev_pool_w, ev_pool_scale,
           ev_w_out, od_w_in, od_rpb, od_w_out):
    batch, n_lat, d = x.shape
    n_ctx = ctx.shape[1]
    assert batch == 1 and d == D_MODEL
    assert n_ctx % TILE_ROWS == 0 and n_lat % LAT_TILE_ROWS == 0 and n_lat % (ATTN_ROWS_PER_STEP * GRID_W) == 0
    n_lat_tiles = n_lat // TILE_ROWS

    cs_t = jnp.concatenate([c_ctx[:, None], c.T, jnp.zeros((d, LANES - N_COND), F32)], axis=1)
    mod_all = _ada_modulation(cs_t, ada_w, ada_b)
    mod_all = mod_all[:, :N_COND].reshape(DEPTH, N_COND, 6, d)
    mod_all = jnp.pad(mod_all, ((0, 0), (0, 0), (0, SUBLANES - 6), (0, 0)))

    mod = mod_all[0]
    seg = jnp.asarray(np.arange(LANES)[:, None] // HEAD == np.arange(LANES)[None, :] // HEAD, BF16)
    r, v, kn, ld, kd, bd, g, bonus, pooled, xa = _rwkv_prepare(
        x[0], ctx[0], mod, ev_w_in[0].astype(BF16), ev_shift_mu[0], ev_w0[0],
        _block_diag(ev_w2[0]).astype(BF16), ev_a0[0], _block_diag(ev_a2[0]).astype(BF16),
        ev_g2[0].astype(BF16), ev_k_k[0].reshape(1, D_A), ev_k_a[0].reshape(1, D_A),
        ev_r_k[0].reshape(1, D_A), seg)
    y_fwd, y_rev = _wkv_scan(r, v, kn, ld, kd, bd, n_lat_blocks=n_lat_tiles)
    pool_wbd = _block_diag(ev_pool_w[0])
    xa, qkv = _rwkv_readout_ffn(
        y_fwd, y_rev, bonus, g, pooled, xa, mod, seg, ev_lnx_g[0].reshape(1, D_A), ev_lnx_b[0].reshape(1, D_A),
        pool_wbd.astype(BF16), ev_pool_scale[0].reshape(1, D_B), ev_w_out[0].astype(BF16),
        ln_g[0, 0].reshape(1, d), ln_b[0, 0].reshape(1, d),
        ffn_w1[0].astype(BF16), ffn_w3[0].astype(BF16), ffn_w2[0].astype(BF16),
        ln_g[0, 1].reshape(1, d), ln_b[0, 1].reshape(1, d), mod_all[1], od_w_in[0].astype(BF16),
        n_lat_tiles=n_lat_tiles)

    mod = mod_all[1]
    bias_tab = _attention_bias_table(od_rpb[0])
    o = _neighbourhood_attention(qkv, bias_tab, n_ctx=n_ctx, n_lat=n_lat)
    xl = _attn_proj_ffn(
        o, xa, mod, od_w_out[0].astype(BF16), ln_g[1, 0].reshape(1, d), ln_b[1, 0].reshape(1, d),
        ffn_w1[1].astype(BF16), ffn_w3[1].astype(BF16), ffn_w2[1].astype(BF16),
        ln_g[1, 1].reshape(1, d), ln_b[1, 1].reshape(1, d), tm=LAT_TILE_ROWS)
    return xl[None]
```

```python
import functools
import math

import jax
import jax.numpy as jnp
import numpy as np
from jax import lax
from jax.experimental import pallas as pl
from jax.experimental.pallas import tpu as pltpu

F32 = jnp.float32
BF16 = jnp.bfloat16

D_MODEL = 1024
DEPTH = 2
GRID_W = 64
ALPHA = (2 * DEPTH) ** 0.25
LN_EPS = 1e-6
HEAD = 64
D_A = 3 * D_MODEL // 4
H_A = D_A // HEAD
LORA_W = 64
LORA_A = 64
LORA_G = 128
GN_EPS = 64e-5
D_B = D_MODEL - D_A
POOL_WINDOWS = (2, 4, 8, 16)
POOL_GROUP = D_B // len(POOL_WINDOWS)
C_RW = 3 * D_A + 2 * LORA_W + 2 * LORA_A + LORA_G
D_IN_EVEN = C_RW + D_B
H_C = D_MODEL // HEAD
NA_KH = 8
NA_KW = 16

LANES = 128
SUBLANES = 8
VMEM_LIMIT_BYTES = 56 * 1024 * 1024

TILE_ROWS = 256
LAT_TILE_ROWS = 512
CHUNK = 64
NEG_BIAS = -1e30


def _sigmoid(x):
    return 1.0 / (1.0 + jnp.exp(-x))


def _layer_norm(z, g, b):
    mu = jnp.mean(z, axis=-1, keepdims=True)
    zc = z - mu
    var = jnp.mean(zc * zc, axis=-1, keepdims=True)
    return zc * lax.rsqrt(var + LN_EPS) * g + b


def _dot(a, b):
    return jnp.dot(a, b, preferred_element_type=F32)


def _dot_nt(a, b):
    return lax.dot_general(a, b, (((1,), (1,)), ((), ())), preferred_element_type=F32)


def _seg_sum(x, seg):
    xb = x.astype(BF16)
    return jnp.concatenate([_dot(xb[:, j:j + LANES], seg) for j in range(0, x.shape[1], LANES)], axis=1)


def _params(*sem):
    return pltpu.CompilerParams(dimension_semantics=sem, vmem_limit_bytes=VMEM_LIMIT_BYTES)


def _const_spec(shape):
    nd = len(shape)
    return pl.BlockSpec(shape, lambda *_: (0,) * nd)


def _mod_spec(d, n_lat_tiles):
    return pl.BlockSpec((1, SUBLANES, d), lambda i: ((i < n_lat_tiles).astype(jnp.int32), 0, 0))


N_COND = 2


def _ada_kernel(cst_ref, w_ref, b_ref, o_ref):
    w = w_ref[0]
    rows = []
    for r in range(N_COND):
        s = cst_ref[:, r:r + 1]
        rows.append(jnp.sum(w * (s * _sigmoid(s)), axis=0, keepdims=True) + b_ref[0])
    o_ref[0] = jnp.concatenate(rows + [jnp.zeros((SUBLANES - N_COND, w.shape[1]), F32)], axis=0)


def _ada_modulation(cs_t, ada_w, ada_b):
    depth, d, n = ada_w.shape
    nb = 1536
    return pl.pallas_call(
        _ada_kernel,
        grid=(depth, n // nb),
        in_specs=[
            pl.BlockSpec((d, LANES), lambda i, j: (0, 0)),
            pl.BlockSpec((1, d, nb), lambda i, j: (i, 0, j)),
            pl.BlockSpec((1, 1, nb), lambda i, j: (i, 0, j)),
        ],
        out_specs=pl.BlockSpec((1, SUBLANES, nb), lambda i, j: (i, 0, j)),
        out_shape=jax.ShapeDtypeStruct((depth, SUBLANES, n), F32),
        compiler_params=_params("arbitrary", "arbitrary"),
        name="ada_modulation",
    )(cs_t, ada_w, ada_b.reshape(depth, 1, n))


def _prep_kernel(x_ref, ctx_ref, xhp_ref, xhn_ref, chp_ref, chn_ref, mod_ref, w_ref, mu_ref, w0_ref, w2_ref,
                 a0_ref, a2_ref, g2_ref, kk_ref, ka_ref, rk_ref, seg_ref,
                 r_o, v_o, kn_o, ld_o, kd_o, bd_o, g_o, bonus_o, pool_o, xa_o, *, tm, n_lat_tiles, n_tiles):
    i = pl.program_id(0)
    is_lat = i < n_lat_tiles
    first = jnp.logical_or(i == 0, i == n_lat_tiles)
    last = jnp.logical_or(i == n_lat_tiles - 1, i == n_tiles - 1)
    row = lax.broadcasted_iota(jnp.int32, (tm, 1), 0)
    n_ext = tm + 2 * SUBLANES

    xt = jnp.where(is_lat, x_ref[...], ctx_ref[...])
    xa_o[...] = xt
    x_ext = jnp.concatenate([jnp.where(is_lat, xhp_ref[...], chp_ref[...]), xt,
                             jnp.where(is_lat, xhn_ref[...], chn_ref[...])], axis=0)
    m = mod_ref[0]
    h_ext = (x_ext * (1.0 + m[1:2]) + m[0:1]).astype(BF16)
    mu_prev = mu_ref[0:1]
    mu_next = mu_ref[1:2]
    mu_self = 1.0 - mu_prev - mu_next
    inner = slice(SUBLANES, SUBLANES + tm)

    def project(c0, c1):
        p = _dot(h_ext, w_ref[:, c0:c1])
        return jnp.concatenate([jnp.where(first, 0.0, p[:SUBLANES]), p[inner],
                                jnp.where(last, 0.0, p[SUBLANES + tm:])], axis=0)

    def shift_mix(p, c0, c1):
        return (p[inner] * mu_self[:, c0:c1] + pltpu.roll(p, 1, 0)[inner] * mu_prev[:, c0:c1]
                + pltpu.roll(p, n_ext - 1, 0)[inner] * mu_next[:, c0:c1])

    s1, s2, s3 = D_A, 2 * D_A, 3 * D_A
    s4 = s3 + 2 * LORA_W
    s5 = s4 + 2 * LORA_A
    lora = shift_mix(project(s3, C_RW), s3, C_RW)
    k_ext = project(s1, s2)
    lw = _dot(jnp.tanh(lora[:, :s4 - s3]).astype(BF16), w2_ref[...])
    la = _dot(lora[:, s4 - s3:s5 - s3].astype(BF16), a2_ref[...])
    g = _dot(_sigmoid(lora[:, s5 - s3:]).astype(BF16), g2_ref[...])
    g_o[...] = g.astype(g_o.dtype)

    r_ext = project(0, s1)
    k = shift_mix(k_ext, s1, s2)
    kr = k * kk_ref[...]
    ss = _seg_sum(kr * kr, seg_ref[...])
    kn = kr * lax.rsqrt(jnp.maximum(ss, 1e-24))
    kn_o[...] = kn.astype(kn_o.dtype)

    v_ext = project(s2, s3)
    decay_scale = math.exp(-0.5)
    kd_sum = jnp.zeros_like(k)
    for d in range(2):
        sl = slice(d * D_A, (d + 1) * D_A)
        ld_o[d] = -decay_scale * _sigmoid(w0_ref[d:d + 1] + lw[:, sl])
        a = _sigmoid(a0_ref[d:d + 1] + la[:, sl])
        kd = k * (1.0 + (a - 1.0) * ka_ref[...])
        kd_o[d] = kd.astype(kd_o.dtype)
        bd_o[d] = (kn * a).astype(bd_o.dtype)
        kd_sum = kd_sum + kd

    xe = project(C_RW, D_IN_EVEN)
    r = shift_mix(r_ext, 0, s1)
    r_o[...] = r.astype(r_o.dtype)
    coef = _seg_sum(r * kd_sum * rk_ref[...], seg_ref[...])
    v = shift_mix(v_ext, s2, s3)
    v_o[...] = v.astype(v_o.dtype)
    bonus_o[...] = (coef * v).astype(bonus_o.dtype)

    xp = xe[inner]

    def at_offset(a, s):
        return pltpu.roll(a, (-s) % n_ext, 0)

    sums = {}
    acc = xe + at_offset(xe, -1)
    sums[2] = acc
    half = 1
    for win in (4, 8, 16):
        acc = at_offset(acc, -half) + at_offset(acc, half)
        sums[win] = acc
        half *= 2

    in_ctx = i >= n_lat_tiles
    tile0 = jnp.where(in_ctx, n_lat_tiles, 0)
    seq_len = jnp.where(in_ctx, (n_tiles - n_lat_tiles) * tm, n_lat_tiles * tm)
    t_seq = (i - tile0) * tm + row
    grp = lax.broadcasted_iota(jnp.int32, (1, D_B), 1) // POOL_GROUP
    mean = jnp.zeros((tm, D_B), F32)
    for gi, win in enumerate(POOL_WINDOWS):
        cnt = jnp.minimum(t_seq + win // 2, seq_len) - jnp.maximum(t_seq - win // 2, 0)
        m = sums[win][SUBLANES:SUBLANES + tm] / cnt.astype(F32)
        mean = jnp.where(grp == gi, m, mean)
    pool_o[...] = mean - xp


def _rwkv_prepare(x, ctx, mod, w_in, mu, w0, w2bd, a0, a2bd, g2, k_k, k_a, r_k, seg):
    n_lat, d = x.shape
    n_ctx = ctx.shape[0]
    t = n_lat + n_ctx
    tm = TILE_ROWS
    n_tiles = t // tm
    n_lat_tiles = n_lat // tm
    hpt = tm // SUBLANES
    lat_halos = n_lat // SUBLANES
    ctx_halos = n_ctx // SUBLANES

    def lat_tile(i):
        return jnp.minimum(i, n_lat_tiles - 1)

    def ctx_tile(i):
        return jnp.maximum(i - n_lat_tiles, 0)

    halo = (SUBLANES, d)
    tok = jax.ShapeDtypeStruct((t, D_A), BF16)
    tok2 = jax.ShapeDtypeStruct((2, t, D_A), BF16)
    ld2 = jax.ShapeDtypeStruct((2, t, D_A), F32)
    tok_spec = pl.BlockSpec((tm, D_A), lambda i: (i, 0))
    tok2_spec = pl.BlockSpec((2, tm, D_A), lambda i: (0, i, 0))
    kern = functools.partial(_prep_kernel, tm=tm, n_lat_tiles=n_lat_tiles, n_tiles=n_tiles)
    return pl.pallas_call(
        kern,
        grid=(n_tiles,),
        in_specs=[
            pl.BlockSpec((tm, d), lambda i: (lat_tile(i), 0)),
            pl.BlockSpec((tm, d), lambda i: (ctx_tile(i), 0)),
            pl.BlockSpec(halo, lambda i: (jnp.maximum(lat_tile(i) * hpt - 1, 0), 0)),
            pl.BlockSpec(halo, lambda i: (jnp.minimum((lat_tile(i) + 1) * hpt, lat_halos - 1), 0)),
            pl.BlockSpec(halo, lambda i: (jnp.maximum(ctx_tile(i) * hpt - 1, 0), 0)),
            pl.BlockSpec(halo, lambda i: (jnp.minimum((ctx_tile(i) + 1) * hpt, ctx_halos - 1), 0)),
            _mod_spec(d, n_lat_tiles), _const_spec(w_in.shape),
            _const_spec(mu.shape), _const_spec(w0.shape), _const_spec(w2bd.shape), _const_spec(a0.shape),
            _const_spec(a2bd.shape), _const_spec(g2.shape), _const_spec(k_k.shape), _const_spec(k_a.shape),
            _const_spec(r_k.shape), _const_spec(seg.shape),
        ],
        out_specs=[tok_spec, tok_spec, tok_spec, tok2_spec, tok2_spec, tok2_spec, tok_spec, tok_spec,
                   pl.BlockSpec((tm, D_B), lambda i: (i, 0)), pl.BlockSpec((tm, d), lambda i: (i, 0))],
        out_shape=[tok, tok, tok, ld2, tok2, tok2, tok, tok, jax.ShapeDtypeStruct((t, D_B), F32),
                   jax.ShapeDtypeStruct((t, d), x.dtype)],
        compiler_params=_params("arbitrary"),
        name="rwkv_prepare",
    )(x, ctx, x, x, ctx, ctx, mod, w_in, mu, w0, w2bd, a0, a2bd, g2, k_k, k_a, r_k, seg)


WKV_PAIRS_PER_STEP = 6
WKV_STAGE_STAGGER = 1


def _wkv_kernel(rf_ref, vf_ref, knf_ref, ldf_ref, kdf_ref, bdf_ref, rr_ref, vr_ref, knr_ref, ldr_ref, kdr_ref,
                bdr_ref, yf_ref, yr_ref, h_ref, lhs_s, add_s, pb_s, ends_s, dec_s, v_s, *, chunks_per_block,
                pairs_per_step):
    step = pl.program_id(1)
    cur = lax.rem(step, 2)
    prev = 1 - cur

    @pl.when(step == 0)
    def _():
        h_ref[...] = jnp.zeros_like(h_ref)
        for ref in (lhs_s, add_s, pb_s, ends_s, dec_s, v_s):
            ref[1] = jnp.zeros(ref.shape[1:], ref.dtype)

    c = CHUNK
    ri = lax.broadcasted_iota(jnp.int32, (c, c), 0)
    ci = lax.broadcasted_iota(jnp.int32, (c, c), 1)
    rp = lax.broadcasted_iota(jnp.int32, (c, LANES), 0)
    cp = lax.broadcasted_iota(jnp.int32, (c, LANES), 1) & (c - 1)
    eye_p = jnp.where(rp == cp, 1.0, 0.0)
    r2 = lax.broadcasted_iota(jnp.int32, (LANES, LANES), 0)
    c2 = lax.broadcasted_iota(jnp.int32, (LANES, LANES), 1)
    same_head = (r2 < HEAD) == (c2 < HEAD)
    head0 = lax.broadcasted_iota(jnp.int32, (1, LANES), 1) < HEAD

    def bdiag(x):
        zero = jnp.zeros_like(x)
        return jnp.concatenate([jnp.where(head0, x, zero), jnp.where(head0, zero, x)], axis=0)

    masks = {}
    for rev in (False, True):
        sgn = -1 if rev else 1
        order = (rp - cp) * sgn
        masks[rev] = (jnp.where((ri - ci) * sgn >= 0, 1.0, 0.0).astype(BF16), order > 0, order >= 0)

    chains = []
    for rev, refs, y_ref in ((False, (rf_ref, vf_ref, knf_ref, ldf_ref, kdf_ref, bdf_ref), yf_ref),
                             (True, (rr_ref, vr_ref, knr_ref, ldr_ref, kdr_ref, bdr_ref), yr_ref)):
        for q in range(pairs_per_step):
            chains.append((rev, refs, y_ref, slice(q * LANES, (q + 1) * LANES)))

    insts = []
    for ch, (rev, refs, y_ref, lanes) in enumerate(chains):
        for j in range(chunks_per_block):
            cj = chunks_per_block - 1 - j if rev else j
            insts.append(dict(idx=len(insts), ch=ch, j=j, rev=rev, refs=refs, lanes=lanes,
                              rows=slice(cj * c, (cj + 1) * c)))
    by_key = {(it["ch"], it["j"]): it for it in insts}

    states = [h_ref[ch] for ch in range(len(chains))]
    pending = {}

    def state_group(k):
        j, second = divmod(k, 2)
        for ch, (rev, refs, y_ref, lanes) in enumerate(chains):
            it = by_key[(ch, j)]
            idx = it["idx"]
            if not second:
                pending[ch] = _dot(lhs_s[prev, idx], states[ch].astype(BF16))
            else:
                m1 = pending.pop(ch)
                add = add_s[prev, idx]
                ub = (-m1[:c] - add[:c]).astype(BF16)
                y_ref[it["rows"], lanes] = m1[c:] + add[c:] + _dot(pb_s[prev, idx], bdiag(ub))
                upd = _dot(ends_s[prev, idx], jnp.concatenate([ub, v_s[prev, idx]], axis=0))
                states[ch] = states[ch] * dec_s[prev, idx] + jnp.where(same_head, upd, 0.0)

    n_state_groups = 2 * chunks_per_block
    emitted = [0]

    def interleave_state_group():
        if emitted[0] < n_state_groups:
            state_group(emitted[0])
            emitted[0] += 1

    def stage_cumsum(group):
        for it in group:
            ld = it["refs"][3][0, it["rows"], it["lanes"]]
            tri = masks[it["rev"]][0]
            ld_hi = ld.astype(BF16)
            rem = ld - ld_hi.astype(F32)
            ld_mid = rem.astype(BF16)
            ld_lo = (rem - ld_mid.astype(F32)).astype(BF16)
            part = _dot(tri, jnp.concatenate([ld_hi, ld_mid], axis=1))
            it["cum"] = part[:, :LANES] + part[:, LANES:] + _dot(tri, ld_lo)

    def stage_gram(group):
        for it in group:
            r_ref, v_ref, kn_ref, ld_ref, kd_ref, bd_ref = it["refs"]
            rows, lanes = it["rows"], it["lanes"]
            _, strict, incl = masks[it["rev"]]
            ld = ld_ref[0, rows, lanes]
            kd = kd_ref[0, rows, lanes].astype(F32)
            bd = bd_ref[0, rows, lanes].astype(F32)
            cum = it.pop("cum")
            total = jnp.sum(ld, axis=0, keepdims=True)
            inv = jnp.exp(-cum)
            to_end = inv * jnp.exp(total)
            kq = (kn_ref[rows, lanes].astype(F32) * jnp.exp(cum - ld)).astype(BF16)
            rq = (r_ref[rows, lanes].astype(F32) * jnp.exp(cum)).astype(BF16)
            g = _dot_nt(jnp.concatenate([kq, rq], axis=0),
                        jnp.concatenate([bdiag((bd * inv).astype(BF16)), bdiag((kd * inv).astype(BF16))], axis=0))
            it["q"] = jnp.where(strict, -g[:c, :LANES], 0.0)
            it["bm"] = jnp.where(strict, g[:c, LANES:], 0.0).astype(BF16)
            it["pk"] = jnp.where(incl, g[c:, LANES:], 0.0).astype(BF16)
            it["kq_bd"], it["rq"] = bdiag(kq), rq
            v = v_ref[rows, lanes]
            it["v_bd"] = bdiag(v)
            idx = it["idx"]
            pb_s[cur, idx] = jnp.where(incl, g[c:, :LANES], 0.0).astype(BF16)
            v_s[cur, idx] = v
            ends_s[cur, idx] = jnp.concatenate([bd * to_end, kd * to_end], axis=0).T.astype(BF16)
            dec_s[cur, idx] = jnp.broadcast_to(jnp.exp(jnp.sum(ld.T, axis=1, keepdims=True)), (LANES, LANES))

    def stage_square(group):
        for it in group:
            q = it.pop("q")
            it["t_inv"] = eye_p + q
            qb = q.astype(BF16)
            it["pw"] = _dot(qb, bdiag(qb)).astype(BF16)

    n_levels = int(math.log2(c)) - 1

    def stage_level(level):
        def run(group):
            for it in group:
                pw = it["pw"]
                t_bd = bdiag(it["t_inv"].astype(BF16))
                if level < n_levels - 1:
                    res = _dot(pw, jnp.concatenate([bdiag(pw), t_bd], axis=1))
                    it["pw"] = res[:, :LANES].astype(BF16)
                    it["t_inv"] = it["t_inv"] + res[:, LANES:]
                else:
                    it["t_inv"] = it["t_inv"] + _dot(pw, t_bd)
        return run

    def stage_bmv(group):
        for it in group:
            it["bmv"] = _dot(it.pop("bm"), it["v_bd"]).astype(BF16)

    def stage_apply(group):
        for it in group:
            wu = _dot(it.pop("t_inv").astype(BF16),
                      jnp.concatenate([it.pop("kq_bd"), bdiag(it.pop("bmv"))], axis=1))
            lhs_s[cur, it["idx"]] = jnp.concatenate([wu[:, :LANES].astype(BF16), it.pop("rq")], axis=0)
            it["ut"] = wu[:, LANES:]

    def stage_pkv(group):
        for it in group:
            pkv = _dot(it.pop("pk"), it["v_bd"])
            add_s[cur, it["idx"]] = jnp.concatenate([it.pop("ut"), pkv], axis=0)

    stages = ([stage_cumsum, stage_gram, stage_square] + [stage_level(lv) for lv in range(n_levels)]
              + [stage_bmv, stage_apply, stage_pkv])
    half = len(insts) // 2
    groups = (insts[:half], insts[half:])
    for step_i in range(len(stages) + WKV_STAGE_STAGGER):
        if step_i < len(stages):
            stages[step_i](groups[0])
        if 0 <= step_i - WKV_STAGE_STAGGER < len(stages):
            stages[step_i - WKV_STAGE_STAGGER](groups[1])
        interleave_state_group()
    while emitted[0] < n_state_groups:
        interleave_state_group()
    for ch in range(len(chains)):
        h_ref[ch] = states[ch]


def _wkv_scan(r, v, kn, ld, kd, bd, *, n_lat_blocks):
    t, da = r.shape
    rows = TILE_ROWS
    n_blocks = t // rows
    pps = WKV_PAIRS_PER_STEP
    width = pps * LANES
    n_groups = da // width

    n_ctx_blocks = n_blocks - n_lat_blocks

    def fwd_blk(s):
        return jnp.where(s < n_ctx_blocks, n_lat_blocks + s, s - n_ctx_blocks)

    def rev_blk(s):
        return jnp.where(s < n_ctx_blocks, n_blocks - 1 - s, n_lat_blocks - 1 - (s - n_ctx_blocks))

    def cur(s):
        return jnp.minimum(s, n_blocks - 1)

    def prv(s):
        return jnp.maximum(s - 1, 0)

    fwd = pl.BlockSpec((rows, width), lambda g, s: (fwd_blk(cur(s)), g))
    bwd = pl.BlockSpec((rows, width), lambda g, s: (rev_blk(cur(s)), g))
    fwd_d = pl.BlockSpec((1, rows, width), lambda g, s: (0, fwd_blk(cur(s)), g))
    bwd_d = pl.BlockSpec((1, rows, width), lambda g, s: (1, rev_blk(cur(s)), g))
    y_fwd = pl.BlockSpec((rows, width), lambda g, s: (fwd_blk(prv(s)), g))
    y_bwd = pl.BlockSpec((rows, width), lambda g, s: (rev_blk(prv(s)), g))
    chunks = rows // CHUNK
    n_inst = 2 * pps * chunks
    kern = functools.partial(_wkv_kernel, chunks_per_block=chunks, pairs_per_step=pps)
    out = jax.ShapeDtypeStruct((t, da), F32)
    return pl.pallas_call(
        kern,
        grid=(n_groups, n_blocks + 1),
        in_specs=[fwd, fwd, fwd, fwd_d, fwd_d, fwd_d, bwd, bwd, bwd, bwd_d, bwd_d, bwd_d],
        out_specs=[y_fwd, y_bwd],
        out_shape=[out, out],
        scratch_shapes=[
            pltpu.VMEM((2 * pps, LANES, LANES), F32),
            pltpu.VMEM((2, n_inst, 2 * CHUNK, LANES), BF16),
            pltpu.VMEM((2, n_inst, 2 * CHUNK, LANES), F32),
            pltpu.VMEM((2, n_inst, CHUNK, LANES), BF16),
            pltpu.VMEM((2, n_inst, LANES, LANES), BF16),
            pltpu.VMEM((2, n_inst, LANES, LANES), F32),
            pltpu.VMEM((2, n_inst, CHUNK, LANES), BF16),
        ],
        compiler_params=_params("arbitrary", "arbitrary"),
        name="wkv_scan",
    )(r, v, kn, ld, kd, bd, r, v, kn, ld, kd, bd)


FFN_SUB_TILES = 2


def _sub_tiles(tm):
    rows = tm // FFN_SUB_TILES
    return [slice(s * rows, (s + 1) * rows) for s in range(FFN_SUB_TILES)]


def _ffn_sublayer(xs, m, w1_ref, w3_ref, w2_ref, lng_ref, lnb_ref):
    hs = [(x * (1.0 + m[4:5]) + m[3:4]).astype(BF16) for x in xs]
    ups = [(_dot(h, w1_ref[...]), _dot(h, w3_ref[...])) for h in hs]
    acts = [(a * _sigmoid(a) * b).astype(BF16) for a, b in ups]
    outs = [_dot(act, w2_ref[...]) for act in acts]
    return [_layer_norm(ALPHA * x + m[5:6] * out, lng_ref[...], lnb_ref[...]) for x, out in zip(xs, outs)]


def _readout_kernel(yf_ref, yr_ref, bonus_ref, g_ref, pool_ref, x_ref, mod_ref, seg_ref, lnxg_ref, lnxb_ref,
                    poolw_ref, pools_ref, wout_ref, lng_ref, lnb_ref, w1_ref, w3_ref, w2_ref, lng2_ref, lnb2_ref,
                    mod_next_ref, w_next_ref, o_ref, proj_ref):
    seg = seg_ref[...]
    m = mod_ref[0]
    x1s = []
    tiles = _sub_tiles(x_ref.shape[0])
    for rows in tiles:
        y = yf_ref[rows, :] + yr_ref[rows, :]
        mu = _seg_sum(y, seg) * (1.0 / HEAD)
        yc = y - mu
        var = _seg_sum(yc * yc, seg) * (1.0 / HEAD)
        yn = yc * lax.rsqrt(var + GN_EPS) * lnxg_ref[...] + lnxb_ref[...]
        a = (yn + bonus_ref[rows, :].astype(F32)) * g_ref[rows, :].astype(F32)
        b = _dot(pool_ref[rows, :].astype(BF16), poolw_ref[...]) * pools_ref[...]
        out = _dot(a.astype(BF16), wout_ref[:D_A, :]) + _dot(b.astype(BF16), wout_ref[D_A:, :])
        x1s.append(_layer_norm(ALPHA * x_ref[rows, :] + m[2:3] * out, lng_ref[...], lnb_ref[...]))
    x2s = _ffn_sublayer(x1s, m, w1_ref, w3_ref, w2_ref, lng2_ref, lnb2_ref)
    mn = mod_next_ref[0]
    for rows, x2 in zip(tiles, x2s):
        o_ref[rows, :] = x2
        proj = _dot((x2 * (1.0 + mn[1:2]) + mn[0:1]).astype(BF16), w_next_ref[...]).astype(proj_ref.dtype)
        for j in range(proj_ref.shape[0]):
            proj_ref[j, rows, :] = proj[:, j * LANES:(j + 1) * LANES]


def _single_spec(shape):
    nd = len(shape)
    return pl.BlockSpec(shape, lambda *_: (0,) * nd, pipeline_mode=pl.Buffered(1))


def _rwkv_readout_ffn(y_fwd, y_rev, bonus, g, pooled, x, mod, seg, lnx_g, lnx_b, pool_wbd, pool_scale, w_out, ln_g,
                      ln_b, w1, w3, w2, ln_g2, ln_b2, mod_next, w_next, *, n_lat_tiles):
    t, d = x.shape
    tm = TILE_ROWS
    tok = pl.BlockSpec((tm, D_A), lambda i: (i, 0))
    return pl.pallas_call(
        _readout_kernel,
        grid=(t // tm,),
        in_specs=[
            tok, tok, tok, tok,
            pl.BlockSpec((tm, D_B), lambda i: (i, 0)),
            pl.BlockSpec((tm, d), lambda i: (i, 0)),
            _mod_spec(d, n_lat_tiles),
            _const_spec(seg.shape), _const_spec(lnx_g.shape), _const_spec(lnx_b.shape),
            _const_spec(pool_wbd.shape), _const_spec(pool_scale.shape), _const_spec(w_out.shape),
            _const_spec(ln_g.shape), _const_spec(ln_b.shape),
            _single_spec(w1.shape), _single_spec(w3.shape), _single_spec(w2.shape),
            _const_spec(ln_g2.shape), _const_spec(ln_b2.shape),
            _mod_spec(d, n_lat_tiles), _single_spec(w_next.shape),
        ],
        out_specs=[pl.BlockSpec((tm, d), lambda i: (i, 0)),
                   pl.BlockSpec((w_next.shape[1] // LANES, tm, LANES), lambda i: (0, i, 0))],
        out_shape=[jax.ShapeDtypeStruct((t, d), F32),
                   jax.ShapeDtypeStruct((w_next.shape[1] // LANES, t, LANES), BF16)],
        compiler_params=_params("arbitrary"),
        name="rwkv_readout_ffn",
    )(y_fwd, y_rev, bonus, g, pooled, x, mod, seg, lnx_g, lnx_b, pool_wbd, pool_scale, w_out, ln_g, ln_b,
      w1, w3, w2, ln_g2, ln_b2, mod_next, w_next)


ATTN_PV_LAG = 4
ATTN_ROWS_PER_STEP = 64


def _attn_kernel(q_ref, k_ref, v_ref, bias_ref, o_ref, *, q_rows, n_rows, n_lat, n_ctx, kh):
    rb = pl.program_id(1)
    head0 = lax.broadcasted_iota(jnp.int32, (1, LANES), 1) < HEAD
    kc = k_ref[0, n_lat:n_lat + n_ctx, :]
    vc = v_ref[0, n_lat:n_lat + n_ctx, :]
    n_loc = kh * GRID_W
    q_all = q_ref[0] * jnp.asarray(HEAD ** -0.5, q_ref.dtype)
    zero = jnp.zeros((GRID_W, LANES), q_all.dtype)

    rows = [dict() for _ in range(q_rows)]

    def scores(j):
        it = rows[j]
        rr = rb * q_rows + j
        sr = jnp.clip(rr - kh // 2, 0, n_rows - kh)
        t_var = sr - rr + (NA_KH - 1)
        q = q_all[j * GRID_W:(j + 1) * GRID_W]
        q2 = jnp.concatenate([jnp.where(head0, q, zero), jnp.where(head0, zero, q)], axis=0)
        it["krows"] = pl.ds(pl.multiple_of(sr * GRID_W, GRID_W), n_loc)
        bias = jnp.concatenate(
            [jnp.concatenate([bias_ref[hh, t_var + 2 * a] for a in range(kh // 2)], axis=1) for hh in range(2)],
            axis=0)
        it["s_loc"] = _dot_nt(q2, k_ref[0, it["krows"], :]) + bias
        it["s_ctx"] = _dot_nt(q2, kc)

    def softmax(j):
        it = rows[j]
        s_loc, s_ctx = it.pop("s_loc"), it.pop("s_ctx")
        m = jnp.maximum(jnp.max(s_loc, axis=-1, keepdims=True), jnp.max(s_ctx, axis=-1, keepdims=True))
        p_loc = jnp.exp(s_loc - m)
        p_ctx = jnp.exp(s_ctx - m)
        it["denom"] = jnp.sum(p_loc, axis=-1, keepdims=True) + jnp.sum(p_ctx, axis=-1, keepdims=True)
        it["p_loc"] = p_loc.astype(BF16)
        it["p_ctx"] = p_ctx.astype(BF16)

    def values(j):
        it = rows[j]
        o2 = (_dot(it.pop("p_loc"), v_ref[0, it["krows"], :]) + _dot(it.pop("p_ctx"), vc)) / it["denom"]
        o_ref[0, j * GRID_W:(j + 1) * GRID_W, :] = jnp.where(head0, o2[:GRID_W], o2[GRID_W:]).astype(o_ref.dtype)

    for j in range(q_rows + ATTN_PV_LAG):
        if j < q_rows:
            scores(j)
        if 0 <= j - 1 < q_rows:
            softmax(j - 1)
        if 0 <= j - ATTN_PV_LAG < q_rows:
            values(j - ATTN_PV_LAG)


def _neighbourhood_attention(qkv, bias_tab, *, n_ctx, n_lat):
    t_all = qkv.shape[1]
    d = D_MODEL
    n_rows = n_lat // GRID_W
    kh = min(NA_KH, n_rows)
    q_rows = ATTN_ROWS_PER_STEP
    q_tile = q_rows * GRID_W
    n_pairs = d // LANES
    kern = functools.partial(_attn_kernel, q_rows=q_rows, n_rows=n_rows, n_lat=n_lat, n_ctx=n_ctx, kh=kh)
    assert kh % 2 == 0
    return pl.pallas_call(
        kern,
        grid=(n_pairs, n_lat // q_tile),
        in_specs=[
            pl.BlockSpec((1, q_tile, LANES), lambda p, b: (p, b, 0)),
            pl.BlockSpec((1, t_all, LANES), lambda p, b: (n_pairs + p, 0, 0)),
            pl.BlockSpec((1, t_all, LANES), lambda p, b: (2 * n_pairs + p, 0, 0)),
            pl.BlockSpec((2,) + bias_tab.shape[1:], lambda p, b: (p, 0, 0, 0)),
        ],
        out_specs=pl.BlockSpec((1, q_tile, LANES), lambda p, b: (p, b, 0)),
        out_shape=jax.ShapeDtypeStruct((n_pairs, n_lat, LANES), BF16),
        compiler_params=_params("arbitrary", "arbitrary"),
        name="neighbourhood_attention",
    )(qkv, qkv, qkv, bias_tab)


def _attention_bias_table(rpb):
    h, n_ro, n_co = rpb.shape
    cols = np.arange(GRID_W)
    col_start = np.clip(cols - NA_KW // 2, 0, GRID_W - NA_KW)
    in_win = (cols[None, :] >= col_start[:, None]) & (cols[None, :] < col_start[:, None] + NA_KW)
    col_off = np.tile(cols[None, :] - cols[:, None] + (NA_KW - 1), (1, 2))
    win2 = np.tile(in_win, (1, 2))
    second = jnp.asarray(np.arange(2 * GRID_W)[None, :] >= GRID_W)
    tab = jnp.full((h, n_ro - 1, GRID_W, 2 * GRID_W), NEG_BIAS, F32)
    for j in range(n_co):
        val = jnp.where(second, rpb[:, 1:, j][:, :, None, None], rpb[:, :-1, j][:, :, None, None])
        tab = jnp.where(jnp.asarray((col_off == j) & win2), val, tab)
    return tab


def _proj_ffn_kernel(o_ref, x_ref, mod_ref, w_ref, lng_ref, lnb_ref, w1_ref, w3_ref, w2_ref, lng2_ref, lnb2_ref,
                     out_ref):
    m = mod_ref[0]
    tiles = _sub_tiles(x_ref.shape[0])
    ys = [_dot(jnp.concatenate([o_ref[j, rows, :] for j in range(o_ref.shape[0])], axis=1), w_ref[...])
          for rows in tiles]
    x1s = [_layer_norm(ALPHA * x_ref[rows, :] + m[2:3] * y, lng_ref[...], lnb_ref[...]) for rows, y in zip(tiles, ys)]
    x2s = _ffn_sublayer(x1s, m, w1_ref, w3_ref, w2_ref, lng2_ref, lnb2_ref)
    for rows, x2 in zip(tiles, x2s):
        out_ref[rows, :] = x2


def _attn_proj_ffn(o, x, mod, w, ln_g, ln_b, w1, w3, w2, ln_g2, ln_b2, *, tm):
    n_slabs, t, _ = o.shape
    d = n_slabs * LANES
    return pl.pallas_call(
        _proj_ffn_kernel,
        grid=(t // tm,),
        in_specs=[
            pl.BlockSpec((n_slabs, tm, LANES), lambda i: (0, i, 0)),
            pl.BlockSpec((tm, d), lambda i: (i, 0)),
            pl.BlockSpec((1, SUBLANES, d), lambda i: (1, 0, 0)),
            _single_spec(w.shape), _const_spec(ln_g.shape), _const_spec(ln_b.shape),
            _single_spec(w1.shape), _single_spec(w3.shape), _single_spec(w2.shape),
            _const_spec(ln_g2.shape), _const_spec(ln_b2.shape),
        ],
        out_specs=pl.BlockSpec((tm, d), lambda i: (i, 0)),
        out_shape=jax.ShapeDtypeStruct((t, d), F32),
        compiler_params=_params("arbitrary"),
        name="attn_proj_ffn",
    )(o, x, mod, w, ln_g, ln_b, w1, w3, w2, ln_g2, ln_b2)


def _block_diag(w):
    n, r, c = w.shape
    on_diag = (np.arange(n * r)[:, None] // r) == (np.arange(n * c)[None, :] // c)
    return jnp.where(jnp.asarray(on_diag), jnp.tile(w.reshape(n * r, c), (1, n)), 0.0)


def kernel(x, c, ctx, c_ctx, ada_w, ada_b, ln_g, ln_b, ffn_w1, ffn_w3, ffn_w2, ev_w_in, ev_shift_mu, ev_w0,
           ev_w2, ev_a0, ev_a2, ev_g2, ev_k_k, ev_k_a, ev_r_k, ev_lnx_g, ev_lnx_b, ev_pool_w, ev_pool_scale,
           ev_w_out, od_w_in, od_rpb, od_w_out):
    batch, n_lat, d = x.shape
    n_ctx = ctx.shape[1]
    assert batch == 1 and d == D_MODEL
    assert n_ctx % TILE_ROWS == 0 and n_lat % LAT_TILE_ROWS == 0 and n_lat % (ATTN_ROWS_PER_STEP * GRID_W) == 0
    n_lat_tiles = n_lat // TILE_ROWS

    cs_t = jnp.concatenate([c_ctx[:, None], c.T, jnp.zeros((d, LANES - N_COND), F32)], axis=1)
    mod_all = _ada_modulation(cs_t, ada_w, ada_b)
    mod_all = mod_all[:, :N_COND].reshape(DEPTH, N_COND, 6, d)
    mod_all = jnp.pad(mod_all, ((0, 0), (0, 0), (0, SUBLANES - 6), (0, 0)))

    mod = mod_all[0]
    seg = jnp.asarray(np.arange(LANES)[:, None] // HEAD == np.arange(LANES)[None, :] // HEAD, BF16)
    r, v, kn, ld, kd, bd, g, bonus, pooled, xa = _rwkv_prepare(
        x[0], ctx[0], mod, ev_w_in[0].astype(BF16), ev_shift_mu[0], ev_w0[0],
        _block_diag(ev_w2[0]).astype(BF16), ev_a0[0], _block_diag(ev_a2[0]).astype(BF16),
        ev_g2[0].astype(BF16), ev_k_k[0].reshape(1, D_A), ev_k_a[0].reshape(1, D_A),
        ev_r_k[0].reshape(1, D_A), seg)
    y_fwd, y_rev = _wkv_scan(r, v, kn, ld, kd, bd, n_lat_blocks=n_lat_tiles)
    pool_wbd = _block_diag(ev_pool_w[0])
    xa, qkv = _rwkv_readout_ffn(
        y_fwd, y_rev, bonus, g, pooled, xa, mod, seg, ev_lnx_g[0].reshape(1, D_A), ev_lnx_b[0].reshape(1, D_A),
        pool_wbd.astype(BF16), ev_pool_scale[0].reshape(1, D_B), ev_w_out[0].astype(BF16),
        ln_g[0, 0].reshape(1, d), ln_b[0, 0].reshape(1, d),
        ffn_w1[0].astype(BF16), ffn_w3[0].astype(BF16), ffn_w2[0].astype(BF16),
        ln_g[0, 1].reshape(1, d), ln_b[0, 1].reshape(1, d), mod_all[1], od_w_in[0].astype(BF16),
        n_lat_tiles=n_lat_tiles)

    mod = mod_all[1]
    bias_tab = _attention_bias_table(od_rpb[0])
    o = _neighbourhood_attention(qkv, bias_tab, n_ctx=n_ctx, n_lat=n_lat)
    xl = _attn_proj_ffn(
        o, xa, mod, od_w_out[0].astype(BF16), ln_g[1, 0].reshape(1, d), ln_b[1, 0].reshape(1, d),
        ffn_w1[1].astype(BF16), ffn_w3[1].astype(BF16), ffn_w2[1].astype(BF16),
        ln_g[1, 1].reshape(1, d), ln_b[1, 1].reshape(1, d), tm=LAT_TILE_ROWS)
    return xl[None]
```
